```python
import jax, jax.numpy as jnp
from jax import lax
import numpy as np

D_MODEL = 2048
BATCH = 4
SEQ = 2048
DEPTH = 1
DEC_BATCH = 128
DEC_SEQ = 8
PAST_LEN = 16384
PAGE_SIZE = 128

D_MIX = D_MODEL
D_CONV = D_MIX // 2
D_RWKV = D_MIX - D_CONV
HEAD_DIM = 64
N_HEADS = D_RWKV // HEAD_DIM
CONV_WIDTH = 3
LORA_DECAY = 64
LORA_AAA = 64
LORA_GATE = 128
D_RWKV_PROJ = 3 * D_RWKV + LORA_DECAY + LORA_AAA + LORA_GATE
D_IN_PROJ = 3 * D_CONV + D_RWKV_PROJ
RWKV_SPLITS = (D_RWKV, 2 * D_RWKV, 3 * D_RWKV, 3 * D_RWKV + LORA_DECAY,
               3 * D_RWKV + LORA_DECAY + LORA_AAA)
N_GROUPS = 4
EXPERTS_PER_GROUP = 8
N_EXPERTS = N_GROUPS * EXPERTS_PER_GROUP
TOP_K = 2
D_EXPERT = 512
N_MOD = 6
RMS_EPS = 1e-6
GN_EPS = HEAD_DIM * 1e-5

kernel_name = 'hymba_conv_rwkv7_hmoe_adaln_step'


def rms_norm(x, g):
    xf = x.astype(jnp.float32)
    xf = xf * lax.rsqrt(jnp.mean(xf * xf, axis=-1, keepdims=True) + RMS_EPS)
    return (xf * g.astype(jnp.float32)).astype(x.dtype)


def short_conv(u, conv_buf, conv_w):
    t_len = u.shape[1]
    u_pad = jnp.concatenate([conv_buf.astype(u.dtype), u], axis=1)
    y = u_pad[:, CONV_WIDTH - 1:] * conv_w[CONV_WIDTH - 1]
    for j in range(CONV_WIDTH - 1):
        y = y + u_pad[:, j:j + t_len] * conv_w[j]
    return y, u_pad[:, t_len:]


def wkv_scan(r, w, k, v, a, b, s0):
    def step(s, inp):
        r_t, w_t, k_t, v_t, a_t, b_t = inp
        sa = jnp.einsum('bhvk,bhk->bhv', s, a_t)
        s = s * w_t[:, :, None, :] + sa[..., None] * b_t[:, :, None, :] + v_t[..., None] * k_t[:, :, None, :]
        return s, jnp.einsum('bhvk,bhk->bhv', s, r_t)
    xs = tuple(jnp.swapaxes(z, 0, 1) for z in (r, w, k, v, a, b))
    s_fin, ys = lax.scan(step, s0, xs)
    return jnp.swapaxes(ys, 0, 1), s_fin


def token_mixers(h, conv_buf, shift_buf, wkv0, w_in, conv_w, mix_mu, w0, w_decay_up, a0,
                 w_aaa_up, w_gate_up, k_k, k_a, r_k, ln_x_w, ln_x_b, w_out):
    bsz, t_len, _ = h.shape
    f32 = jnp.float32
    proj = h @ w_in
    gate_b, gate_c, h_conv, p_rwkv = jnp.split(proj, [D_CONV, 2 * D_CONV, 3 * D_CONV], axis=-1)
    conv_y, new_conv = short_conv(gate_c * h_conv, conv_buf, conv_w)
    y_conv = gate_b * conv_y
    p_prev = jnp.concatenate([shift_buf[:, None, :].astype(p_rwkv.dtype), p_rwkv[:, :-1]], axis=1)
    new_shift = p_rwkv[:, -1]
    p_mix = p_rwkv + (p_prev - p_rwkv) * mix_mu
    r, k, v, xw, xa, xg = jnp.split(p_mix, list(RWKV_SPLITS), axis=-1)
    w_log = -jax.nn.softplus(-(w0 + jnp.tanh(xw) @ w_decay_up).astype(f32)) - 0.5
    decay = jnp.exp(-jnp.exp(w_log))
    a = jax.nn.sigmoid((a0 + xa @ w_aaa_up).astype(f32))
    g = jax.nn.sigmoid(xg) @ w_gate_up
    heads = lambda z: z.astype(f32).reshape(bsz, t_len, N_HEADS, HEAD_DIM)
    kk = heads(k * k_k)
    kk = kk * lax.rsqrt(jnp.maximum(jnp.sum(kk * kk, axis=-1, keepdims=True), 1e-24))
    k_h = heads(k * (1 + (a - 1) * k_a))
    r_h, v_h, a_h, w_h = heads(r), heads(v), heads(a), heads(decay)
    y, new_wkv = wkv_scan(r_h, w_h, k_h, v_h, -kk, kk * a_h, wkv0.astype(f32))
    mu = jnp.mean(y, axis=-1, keepdims=True)
    var = jnp.mean(jnp.square(y - mu), axis=-1, keepdims=True)
    y_n = ((y - mu) * lax.rsqrt(var + GN_EPS)).reshape(bsz, t_len, D_RWKV)
    y_n = y_n * ln_x_w.astype(f32) + ln_x_b.astype(f32)
    bonus = jnp.sum(r_h * k_h * r_k.astype(f32), axis=-1, keepdims=True) * v_h
    y_rwkv = ((y_n + bonus.reshape(bsz, t_len, D_RWKV)) * g.astype(f32)).astype(h.dtype)
    out = jnp.concatenate([y_conv, y_rwkv], axis=-1) @ w_out
    return out, new_conv, new_shift, new_wkv.astype(wkv0.dtype)


def hier_moe(h, w_router_group, b_router_group, w_router_expert, b_router_expert, w_gate, w_up, w_down):
    bsz, t_len, d = h.shape
    f32 = jnp.float32
    tok = h.reshape(bsz * t_len, d)
    group_logits = (tok @ w_router_group + b_router_group).astype(f32)
    group_prob = jax.nn.softmax(group_logits, axis=-1)
    g_idx = jnp.argmax(group_logits, axis=-1)
    expert_logits = (tok @ w_router_expert + b_router_expert).astype(f32)
    expert_logits = expert_logits.reshape(-1, N_GROUPS, EXPERTS_PER_GROUP)
    in_group = jnp.take_along_axis(expert_logits, g_idx[:, None, None], axis=1)[:, 0]
    top_p, top_i = lax.top_k(jax.nn.softmax(in_group, axis=-1), TOP_K)
    top_w = top_p / jnp.sum(top_p, axis=-1, keepdims=True) * jnp.take_along_axis(group_prob, g_idx[:, None], axis=1)
    expert_id = g_idx[:, None] * EXPERTS_PER_GROUP + top_i
    combine = jnp.einsum('nke,nk->ne', jax.nn.one_hot(expert_id, N_EXPERTS, dtype=f32), top_w)

    def expert_step(acc, e_in):
        wg, wu, wd, ce = e_in
        out = (jax.nn.silu(tok @ wg) * (tok @ wu)) @ wd
        return acc + ce[:, None] * out.astype(f32), None

    acc, _ = lax.scan(expert_step, jnp.zeros((tok.shape[0], d), f32), (w_gate, w_up, w_down, combine.T))
    return acc.astype(h.dtype).reshape(bsz, t_len, d)


def decoder_layer(x, c, conv_buf, shift_buf, wkv0, w_ada, b_ada, norm1_g, w_in, conv_w, mix_mu, w0,
                  w_decay_up, a0, w_aaa_up, w_gate_up, k_k, k_a, r_k, ln_x_w, ln_x_b, w_out, norm2_g,
                  w_router_group, b_router_group, w_router_expert, b_router_expert, w_gate, w_up, w_down):
    mod = (jax.nn.silu(c) @ w_ada + b_ada)[:, None, :]
    sh1, sc1, ga1, sh2, sc2, ga2 = jnp.split(mod, N_MOD, axis=-1)
    h = rms_norm(x, norm1_g) * (1 + sc1) + sh1
    mix, new_conv, new_shift, new_wkv = token_mixers(
        h, conv_buf, shift_buf, wkv0, w_in, conv_w, mix_mu, w0, w_decay_up, a0, w_aaa_up,
        w_gate_up, k_k, k_a, r_k, ln_x_w, ln_x_b, w_out)
    x = x + ga1 * mix
    h = rms_norm(x, norm2_g) * (1 + sc2) + sh2
    x = x + ga2 * hier_moe(h, w_router_group, b_router_group, w_router_expert, b_router_expert,
                           w_gate, w_up, w_down)
    return x, new_conv, new_shift, new_wkv


def setup_inputs(seed: int = 0) -> dict:
    key = jax.random.key(seed)
    ks = iter(jax.random.split(key, 48))
    nrm = lambda shape, scale: jax.random.normal(next(ks), shape, jnp.float32) * scale
    uni = lambda shape, lo, hi: jax.random.uniform(next(ks), shape, jnp.float32, lo, hi)
    L, D = DEPTH, D_MODEL
    return {
        'x_prompt': nrm((BATCH, SEQ, D), 1.0),
        'x_sample': nrm((DEC_BATCH, DEC_SEQ, D), 1.0),
        'c_prompt': nrm((BATCH, D), 1.0),
        'c_sample': nrm((DEC_BATCH, D), 1.0),
        'state_conv': nrm((L, DEC_BATCH, CONV_WIDTH - 1, D_CONV), 1.0),
        'state_shift': nrm((L, DEC_BATCH, D_RWKV_PROJ), 1.0),
        'state_wkv': nrm((L, DEC_BATCH, N_HEADS, HEAD_DIM, HEAD_DIM), 0.3),
        'w_ada': nrm((L, D, N_MOD * D), D ** -0.5),
        'b_ada': nrm((L, N_MOD * D), 0.01),
        'norm1_g': 1.0 + nrm((L, D), 0.05),
        'w_in': nrm((L, D, D_IN_PROJ), D ** -0.5),
        'conv_w': nrm((L, CONV_WIDTH, D_CONV), CONV_WIDTH ** -0.5),
        'mix_mu': uni((L, D_RWKV_PROJ), 0.0, 1.0),
        'w0': uni((L, D_RWKV), -6.0, 1.0),
        'w_decay_up': nrm((L, LORA_DECAY, D_RWKV), LORA_DECAY ** -0.5),
        'a0': nrm((L, D_RWKV), 0.5),
        'w_aaa_up': nrm((L, LORA_AAA, D_RWKV), LORA_AAA ** -0.5),
        'w_gate_up': nrm((L, LORA_GATE, D_RWKV), LORA_GATE ** -0.5),
        'k_k': 0.85 + nrm((L, D_RWKV), 0.05),
        'k_a': 1.0 + nrm((L, D_RWKV), 0.05),
        'r_k': nrm((L, N_HEADS, HEAD_DIM), 0.1),
        'ln_x_w': 1.0 + nrm((L, D_RWKV), 0.05),
        'ln_x_b': nrm((L, D_RWKV), 0.01),
        'w_out': nrm((L, D_MIX, D), D_MIX ** -0.5),
        'norm2_g': 1.0 + nrm((L, D), 0.05),
        'w_router_group': nrm((L, D, N_GROUPS), D ** -0.5),
        'b_router_group': nrm((L, N_GROUPS), 0.01),
        'w_router_expert': nrm((L, D, N_EXPERTS), D ** -0.5),
        'b_router_expert': nrm((L, N_EXPERTS), 0.01),
        'w_gate': nrm((L, N_EXPERTS, D, D_EXPERT), D ** -0.5),
        'w_up': nrm((L, N_EXPERTS, D, D_EXPERT), D ** -0.5),
        'w_down': nrm((L, N_EXPERTS, D_EXPERT, D), D_EXPERT ** -0.5),
        'final_g': 1.0 + nrm((D,), 0.05),
    }


def reference(x_prompt, x_sample, c_prompt, c_sample, state_conv, state_shift, state_wkv,
              w_ada, b_ada, norm1_g, w_in, conv_w, mix_mu, w0, w_decay_up, a0, w_aaa_up, w_gate_up,
              k_k, k_a, r_k, ln_x_w, ln_x_b, w_out, norm2_g, w_router_group, b_router_group,
              w_router_expert, b_router_expert, w_gate, w_up, w_down, final_g):
    layer_params = (w_ada, b_ada, norm1_g, w_in, conv_w, mix_mu, w0, w_decay_up, a0, w_aaa_up,
                    w_gate_up, k_k, k_a, r_k, ln_x_w, ln_x_b, w_out, norm2_g, w_router_group,
                    b_router_group, w_router_expert, b_router_expert, w_gate, w_up, w_down)
    bp = x_prompt.shape[0]
    h_p, h_s = x_prompt, x_sample
    conv_p, shift_p, wkv_p, conv_s, shift_s, wkv_s = [], [], [], [], [], []
    for l in range(DEPTH):
        lp = [p[l] for p in layer_params]
        zero_conv = jnp.zeros((bp, CONV_WIDTH - 1, D_CONV), x_prompt.dtype)
        zero_shift = jnp.zeros((bp, D_RWKV_PROJ), x_prompt.dtype)
        zero_wkv = jnp.zeros((bp, N_HEADS, HEAD_DIM, HEAD_DIM), x_prompt.dtype)
        h_p, cp, sp, wp = decoder_layer(h_p, c_prompt, zero_conv, zero_shift, zero_wkv, *lp)
        h_s, cs, ss, ws = decoder_layer(h_s, c_sample, state_conv[l], state_shift[l], state_wkv[l], *lp)
        conv_p.append(cp); shift_p.append(sp); wkv_p.append(wp)
        conv_s.append(cs); shift_s.append(ss); wkv_s.append(ws)
    y_prompt = rms_norm(h_p, final_g)
    y_sample = rms_norm(h_s, final_g)
    new_conv_prompt = jnp.stack(conv_p)
    new_shift_prompt = jnp.stack(shift_p)
    new_wkv_prompt = jnp.stack(wkv_p)
    new_conv_sample = jnp.stack(conv_s)
    new_shift_sample = jnp.stack(shift_s)
    new_wkv_sample = jnp.stack(wkv_s)
    return (y_prompt, y_sample, new_conv_prompt, new_shift_prompt, new_wkv_prompt,
            new_conv_sample, new_shift_sample, new_wkv_sample)
```

```python
import functools

import jax
import jax.numpy as jnp
from jax import lax
from jax.experimental import pallas as pl
from jax.experimental.pallas import tpu as pltpu

F32 = jnp.float32
BF16 = jnp.bfloat16
I32 = jnp.int32

D_MODEL = 2048
D_CONV = 1024
D_RWKV = 1024
HEAD_DIM = 64
N_HEADS = 16
LORA_DECAY = 64
LORA_AAA = 64
LORA_GATE = 128
D_RWKV_PROJ = 3 * D_RWKV + LORA_DECAY + LORA_AAA + LORA_GATE
D_IN_PROJ = 3 * D_CONV + D_RWKV_PROJ
N_GROUPS = 4
EXPERTS_PER_GROUP = 8
N_EXPERTS = 32
D_EXPERT = 512
N_MOD = 6
RMS_EPS = 1e-6
GN_EPS = HEAD_DIM * 1e-5

SUBLANES = 8
LANES = 128
VMEM_LIMIT_BYTES = 56 * 1024 * 1024

ROW_TILE = 256
EXPERT_TILE = 256
CHAINS = 64
V_HI = HEAD_DIM // 2
SEG = 256


def _params(semantics):
    return pltpu.CompilerParams(dimension_semantics=semantics, vmem_limit_bytes=VMEM_LIMIT_BYTES)


def _sigmoid(x):
    return 1.0 / (1.0 + jnp.exp(-x))


def _store_row_tiles(ref, x):
    for s in range(ref.shape[1]):
        ref[:, s, :] = x[:, s * LANES:(s + 1) * LANES]


def _load_row_tiles(ref):
    return jnp.concatenate([ref[:, s, :] for s in range(ref.shape[1])], axis=-1)


def _head_ones():
    r = lax.broadcasted_iota(I32, (SEG, SEG), 0) // HEAD_DIM
    c = lax.broadcasted_iota(I32, (SEG, SEG), 1) // HEAD_DIM
    return jnp.where(r == c, 1.0, 0.0).astype(BF16)


def _head_sum(x, ones):
    outs = []
    for s in range(x.shape[-1] // SEG):
        xs = x[:, s * SEG:(s + 1) * SEG]
        hi = xs.astype(BF16)
        lo = (xs - hi.astype(F32)).astype(BF16)
        outs.append(jnp.dot(hi, ones, preferred_element_type=F32) + jnp.dot(lo, ones, preferred_element_type=F32))
    return jnp.concatenate(outs, axis=-1)


def _ada_kernel(c_ref, w_ref, b_ref, o_ref):
    c = c_ref[...]
    s = c * _sigmoid(c)
    o_ref[...] = jnp.dot(s.astype(BF16), w_ref[...].astype(BF16), preferred_element_type=F32) + b_ref[...]


def _ada(c_all, w_ada, b_ada):
    mb = c_all.shape[0]
    n = w_ada.shape[1]
    tn = 1024
    return pl.pallas_call(
        _ada_kernel,
        grid=(n // tn,),
        in_specs=[pl.BlockSpec((mb, D_MODEL), lambda j: (0, 0)),
                  pl.BlockSpec((D_MODEL, tn), lambda j: (0, j)),
                  pl.BlockSpec((1, tn), lambda j: (0, j))],
        out_specs=pl.BlockSpec((mb, tn), lambda j: (0, j)),
        out_shape=jax.ShapeDtypeStruct((mb, n), F32),
        compiler_params=_params(("arbitrary",)),
        name="ada",
    )(c_all, w_ada, b_ada)


def _norm1_kernel(x_ref, g_ref, sh_ref, sc_ref, o_ref):
    x = x_ref[...]
    nb, tb, d = x.shape
    ms = jnp.mean(x * x, axis=-1, keepdims=True)
    xn = x * lax.rsqrt(ms + RMS_EPS) * g_ref[...]
    h = xn * (1.0 + sc_ref[...]) + sh_ref[...]
    o_ref[...] = h.reshape(nb * tb, d).astype(BF16)


def _norm1(x, mod, g, nb, tb):
    b, t, d = x.shape
    nt = t // tb
    return pl.pallas_call(
        _norm1_kernel,
        grid=(b // nb, nt),
        in_specs=[pl.BlockSpec((nb, tb, d), lambda i, j: (i, j, 0)),
                  pl.BlockSpec((1, d), lambda i, j: (0, 0)),
                  pl.BlockSpec((nb, 1, d), lambda i, j: (i, 0, 0)),
                  pl.BlockSpec((nb, 1, d), lambda i, j: (i, 0, 1))],
        out_specs=pl.BlockSpec((nb * tb, d), lambda i, j: (i * nt + j, 0)),
        out_shape=jax.ShapeDtypeStruct((b * t, d), BF16),
        compiler_params=_params(("arbitrary", "arbitrary")),
        name="norm1",
    )(x, g, mod, mod)


def _inproj_kernel(h_ref, w_ref, o_ref, wb_ref):
    @pl.when(pl.program_id(1) == 0)
    def _():
        wb_ref[...] = w_ref[...].astype(BF16)

    o_ref[...] = jnp.dot(h_ref[...], wb_ref[...], preferred_element_type=F32)


def _inproj(h, w_in):
    m, k = h.shape
    n = w_in.shape[1]
    tm = min(m, 1024)
    tn = 640
    return pl.pallas_call(
        _inproj_kernel,
        grid=(n // tn, m // tm),
        in_specs=[pl.BlockSpec((tm, k), lambda j, i: (i, 0)),
                  pl.BlockSpec((k, tn), lambda j, i: (0, j))],
        out_specs=pl.BlockSpec((tm, tn), lambda j, i: (i, j)),
        out_shape=jax.ShapeDtypeStruct((m, n), F32),
        scratch_shapes=[pltpu.VMEM((k, tn), BF16)],
        compiler_params=_params(("arbitrary", "arbitrary")),
        name="inproj",
    )(h, w_in)


def _mixprep_kernel(proj_ref, cst_ref, sst_ref, convw_ref, mu_ref, w0_ref, wd_ref, a0_ref, wa_ref, wg_ref,
                    kk_ref, ka_ref, rk_ref,
                    r_o, w_o, k_o, v_o, a_o, b_o, yc_o, g_o, bonus_o, nc_o, ns_o):
    @pl.when(pl.program_id(1) == 0)
    def _():
        nc_o[...] = cst_ref[...]
        ns_o[...] = sst_ref[...]

    nb, tb, _ = proj_ref.shape
    rows = nb * tb
    p = proj_ref[...].reshape(rows, D_IN_PROJ)
    tidx = lax.broadcasted_iota(I32, (rows, 1), 0) & (tb - 1)

    def per_row(c):
        return jnp.broadcast_to(c, (nb, tb, c.shape[-1])).reshape(rows, c.shape[-1])

    gate_b = p[:, 0:D_CONV]
    gate_c = p[:, D_CONV:2 * D_CONV]
    h_conv = p[:, 2 * D_CONV:3 * D_CONV]
    pr = p[:, 3 * D_CONV:]

    u = gate_c * h_conv
    carry = nc_o[...]
    c2 = per_row(carry[:, 0:1, :])
    c1 = per_row(carry[:, 1:2, :])
    u1 = jnp.where(tidx == 0, c1, pltpu.roll(u, 1, 0))
    u2 = jnp.where(tidx == 0, c2, jnp.where(tidx == 1, c1, pltpu.roll(u, 2, 0)))
    cw = convw_ref[...]
    conv_y = u * cw[2:3, :] + u2 * cw[0:1, :] + u1 * cw[1:2, :]
    yc_o[...] = (gate_b * conv_y).astype(BF16)
    nc_o[...] = u.reshape(nb, tb, D_CONV)[:, tb - 2:tb, :]

    p_prev = jnp.where(tidx == 0, per_row(ns_o[...]), pltpu.roll(pr, 1, 0))
    ns_o[...] = pr.reshape(nb, tb, D_RWKV_PROJ)[:, tb - 1:tb, :]
    pm = pr + (p_prev - pr) * mu_ref[...]
    r = pm[:, 0:D_RWKV]
    k = pm[:, D_RWKV:2 * D_RWKV]
    v = pm[:, 2 * D_RWKV:3 * D_RWKV]
    x_lora = pm[:, 3 * D_RWKV:3 * D_RWKV + LORA_DECAY + LORA_AAA]
    xg = pm[:, 3 * D_RWKV + LORA_DECAY + LORA_AAA:]

    d_pre = w0_ref[...] + jnp.dot(jnp.tanh(x_lora).astype(BF16), wd_ref[...].astype(BF16),
                                  preferred_element_type=F32)
    a_pre = a0_ref[...] + jnp.dot(x_lora.astype(BF16), wa_ref[...].astype(BF16), preferred_element_type=F32)
    g = jnp.dot(_sigmoid(xg).astype(BF16), wg_ref[...].astype(BF16), preferred_element_type=F32)
    z = -d_pre
    softplus = jnp.maximum(z, 0.0) + jnp.log(1.0 + jnp.exp(-jnp.abs(z)))
    w_log = -softplus - 0.5
    decay = jnp.exp(-jnp.exp(w_log))
    a_h = _sigmoid(a_pre)

    ones = _head_ones()
    kk = k * kk_ref[...]
    kk = kk * lax.rsqrt(jnp.maximum(_head_sum(kk * kk, ones), 1e-24))
    k_h = k * (1.0 + (a_h - 1.0) * ka_ref[...])
    bonus = _head_sum(r * k_h * rk_ref[...], ones) * v

    r_o[...] = r
    w_o[...] = decay
    k_o[...] = k_h
    v_o[...] = v
    a_o[...] = -kk
    b_o[...] = kk * a_h
    g_o[...] = g
    bonus_o[...] = bonus


def _mixprep(proj3, conv_state, shift_state, lp, nb, tb):
    b, t, _ = proj3.shape
    nt = t // tb
    rows = nb * tb
    m = b * t

    def row_spec(width):
        return pl.BlockSpec((rows, width), lambda i, j: (i * nt + j, 0))

    def full2(a):
        return pl.BlockSpec(a.shape, lambda i, j: (0, 0))

    consts = (lp["conv_w"], lp["mix_mu"], lp["w0"], lp["wd_pad"], lp["a0"], lp["wa_pad"], lp["w_gate_up"],
              lp["k_k"], lp["k_a"], lp["r_k"])
    f32_out = jax.ShapeDtypeStruct((m, D_RWKV), F32)
    return pl.pallas_call(
        _mixprep_kernel,
        grid=(b // nb, nt),
        in_specs=[pl.BlockSpec((nb, tb, D_IN_PROJ), lambda i, j: (i, j, 0)),
                  pl.BlockSpec((nb, 2, D_CONV), lambda i, j: (i, 0, 0)),
                  pl.BlockSpec((nb, 1, D_RWKV_PROJ), lambda i, j: (i, 0, 0))] + [full2(c) for c in consts],
        out_specs=[row_spec(D_RWKV)] * 6 + [row_spec(D_CONV), row_spec(D_RWKV), row_spec(D_RWKV),
                                            pl.BlockSpec((nb, 2, D_CONV), lambda i, j: (i, 0, 0)),
                                            pl.BlockSpec((nb, 1, D_RWKV_PROJ), lambda i, j: (i, 0, 0))],
        out_shape=[f32_out] * 6 + [jax.ShapeDtypeStruct((m, D_CONV), BF16), f32_out, f32_out,
                                   jax.ShapeDtypeStruct((b, 2, D_CONV), F32),
                                   jax.ShapeDtypeStruct((b, 1, D_RWKV_PROJ), F32)],
        compiler_params=_params(("arbitrary", "arbitrary")),
        name="mixprep",
    )(proj3, conv_state, shift_state, *consts)


def _wkv_kernel(a_ref, w_ref, b_ref, k_ref, r_ref, v_ref, s0_ref, y_ref, s_ref):
    @pl.when(pl.program_id(1) == 0)
    def _():
        s_ref[...] = s0_ref[...]

    tt = a_ref.shape[1]

    def step(t, carry):
        a = a_ref[0, t]
        w = w_ref[0, t]
        b = b_ref[0, t]
        k = k_ref[0, t]
        r = r_ref[0, t]
        for vh in range(V_HI):
            s = s_ref[0, vh]
            vrow = v_ref[0, t, vh:vh + 1, :]
            sa = jnp.sum(s * a, axis=0, keepdims=True)
            s = s * w + sa * b + vrow * k
            s_ref[0, vh] = s
            y_ref[0, t, vh:vh + 1, :] = jnp.sum(s * r, axis=0, keepdims=True)
        return carry

    lax.fori_loop(0, tt, step, 0)


def _wkv(a, w, b, k, r, v, s0, tt):
    g, t = a.shape[0], a.shape[1]
    tile = pl.BlockSpec((1, tt, HEAD_DIM, LANES), lambda i, j: (i, j, 0, 0))
    vrow = pl.BlockSpec((1, tt, V_HI, LANES), lambda i, j: (i, j, 0, 0))
    state = pl.BlockSpec((1, V_HI, HEAD_DIM, LANES), lambda i, j: (i, 0, 0, 0))
    return pl.pallas_call(
        _wkv_kernel,
        grid=(g, t // tt),
        in_specs=[tile] * 5 + [vrow, state],
        out_specs=[vrow, state],
        out_shape=[jax.ShapeDtypeStruct((g, t, V_HI, LANES), F32),
                   jax.ShapeDtypeStruct((g, V_HI, HEAD_DIM, LANES), F32)],
        compiler_params=_params(("arbitrary", "arbitrary")),
        name="wkv",
    )(a, w, b, k, r, v, s0)


def _to_chain_tiles(x, g, bl, t):
    x = x.reshape(g, bl, t, N_HEADS, HEAD_DIM).transpose(0, 2, 4, 1, 3)
    x = jnp.broadcast_to(x[..., None], x.shape + (2,))
    return x.reshape(g, t, HEAD_DIM, LANES)


def _to_chain_rows(x, g, bl, t):
    x = x.reshape(g, bl, t, N_HEADS, V_HI, 2).transpose(0, 2, 4, 1, 3, 5)
    return x.reshape(g, t, V_HI, LANES)


def _from_chain_rows(y, g, bl, t):
    y = y.reshape(g, t, V_HI, bl, N_HEADS, 2).transpose(0, 3, 1, 4, 2, 5)
    return y.reshape(g * bl * t, N_HEADS * HEAD_DIM)


def _state_to_chain(s, g, bl):
    s = s.reshape(g, bl, N_HEADS, V_HI, 2, HEAD_DIM).transpose(0, 3, 5, 1, 2, 4)
    return s.reshape(g, V_HI, HEAD_DIM, LANES)


def _state_from_chain(s, g, bl):
    s = s.reshape(g, V_HI, HEAD_DIM, bl, N_HEADS, 2).transpose(0, 3, 4, 1, 5, 2)
    return s.reshape(g * bl, N_HEADS, HEAD_DIM, HEAD_DIM)


def _post_kernel(y_ref, bonus_ref, g_ref, lnw_ref, lnb_ref, o_ref):
    ones = _head_ones()
    y = y_ref[...]
    mu = _head_sum(y, ones) * (1.0 / HEAD_DIM)
    d = y - mu
    var = _head_sum(d * d, ones) * (1.0 / HEAD_DIM)
    yn = d * lax.rsqrt(var + GN_EPS)
    yn = yn * lnw_ref[...] + lnb_ref[...]
    o_ref[...] = ((yn + bonus_ref[...]) * g_ref[...]).astype(BF16)


def _post(y, bonus, g, ln_w, ln_b):
    m = y.shape[0]
    tm = 512
    row = pl.BlockSpec((tm, D_RWKV), lambda i: (i, 0))
    vec = pl.BlockSpec((1, D_RWKV), lambda i: (0, 0))
    return pl.pallas_call(
        _post_kernel,
        grid=(m // tm,),
        in_specs=[row, row, row, vec, vec],
        out_specs=row,
        out_shape=jax.ShapeDtypeStruct((m, D_RWKV), BF16),
        compiler_params=_params(("arbitrary",)),
        name="post",
    )(y, bonus, g, ln_w, ln_b)


def _cast_kernel(x_ref, o_ref):
    o_ref[...] = x_ref[...].astype(o_ref.dtype)


def _cast_bf16(w):
    r, c = w.shape
    tr = 512
    return pl.pallas_call(
        _cast_kernel,
        grid=(r // tr,),
        in_specs=[pl.BlockSpec((tr, c), lambda i: (i, 0))],
        out_specs=pl.BlockSpec((tr, c), lambda i: (i, 0)),
        out_shape=jax.ShapeDtypeStruct((r, c), BF16),
        compiler_params=_params(("arbitrary",)),
        name="cast_bf16",
    )(w)


def _outproj_kernel(yc_ref, yr_ref, wt_ref, wb_ref, x_ref, ga1_ref, sh2_ref, sc2_ref, g2_ref, wr_ref, br_ref,
                    x1_o, h2_o, lg_o):
    nb, tb, d = x_ref.shape
    rows = nb * tb
    mix = (jnp.dot(yc_ref[...], wt_ref[...], preferred_element_type=F32)
           + jnp.dot(yr_ref[...], wb_ref[...], preferred_element_type=F32))
    x1 = x_ref[...] + ga1_ref[...] * mix.reshape(nb, tb, d)
    ms = jnp.mean(x1 * x1, axis=-1, keepdims=True)
    h2 = x1 * lax.rsqrt(ms + RMS_EPS) * g2_ref[...]
    h2 = (h2 * (1.0 + sc2_ref[...]) + sh2_ref[...]).reshape(rows, d)
    x1_o[...] = x1.reshape(rows, d)
    _store_row_tiles(h2_o, h2)
    lg_o[...] = jnp.dot(h2, wr_ref[...], preferred_element_type=F32,
                        precision=lax.Precision.HIGHEST) + br_ref[...]


def _outproj(yc, yr, w_out_bf16, x, mod, g2, w_router, b_router, nb, tb):
    b, t, d = x.shape
    nt = t // tb
    rows = nb * tb
    m = b * t
    half = D_CONV
    row = lambda width: pl.BlockSpec((rows, width), lambda i, j: (i * nt + j, 0))
    modspec = lambda col: pl.BlockSpec((nb, 1, d), lambda i, j: (i, 0, col))
    return pl.pallas_call(
        _outproj_kernel,
        grid=(b // nb, nt),
        in_specs=[row(half), row(half),
                  pl.BlockSpec((half, d), lambda i, j: (0, 0)),
                  pl.BlockSpec((half, d), lambda i, j: (1, 0)),
                  pl.BlockSpec((nb, tb, d), lambda i, j: (i, j, 0)),
                  modspec(2), modspec(3), modspec(4),
                  pl.BlockSpec((1, d), lambda i, j: (0, 0)),
                  pl.BlockSpec((d, LANES), lambda i, j: (0, 0)),
                  pl.BlockSpec((1, LANES), lambda i, j: (0, 0))],
        out_specs=[row(d), pl.BlockSpec((rows, d // LANES, LANES), lambda i, j: (i * nt + j, 0, 0)), row(LANES)],
        out_shape=[jax.ShapeDtypeStruct((m, d), F32), jax.ShapeDtypeStruct((m, d // LANES, LANES), F32),
                   jax.ShapeDtypeStruct((m, LANES), F32)],
        compiler_params=_params(("arbitrary", "arbitrary")),
        name="outproj",
    )(yc, yr, w_out_bf16, w_out_bf16, x, mod, mod, mod, g2, w_router, b_router)


def _route_kernel(lg_ref, eid_o, wt_o):
    lg = lg_ref[...]
    lane = lax.broadcasted_iota(I32, lg.shape, 1)
    neg = -jnp.inf
    big = jnp.int32(1 << 20)

    is_g = lane < N_GROUPS
    gl = jnp.where(is_g, lg, neg)
    gmax = jnp.max(gl, axis=1, keepdims=True)
    gidx = jnp.min(jnp.where(gl == gmax, lane, big), axis=1, keepdims=True)
    gsum = jnp.sum(jnp.where(is_g, jnp.exp(gl - gmax), 0.0), axis=1, keepdims=True)
    gprob = 1.0 / gsum

    e_lane = lane - N_GROUPS
    in_grp = (e_lane >= 0) & (e_lane < N_EXPERTS) & ((e_lane >> 3) == gidx)
    el = jnp.where(in_grp, lg, neg)
    emax = jnp.max(el, axis=1, keepdims=True)
    ex = jnp.where(in_grp, jnp.exp(el - emax), 0.0)
    prob = ex / jnp.sum(ex, axis=1, keepdims=True)
    prob = jnp.where(in_grp, prob, -1.0)
    p1 = jnp.max(prob, axis=1, keepdims=True)
    i1 = jnp.min(jnp.where(prob == p1, lane, big), axis=1, keepdims=True)
    prob2 = jnp.where(lane == i1, -1.0, prob)
    p2 = jnp.max(prob2, axis=1, keepdims=True)
    i2 = jnp.min(jnp.where(prob2 == p2, lane, big), axis=1, keepdims=True)
    psum = p1 + p2
    w1 = p1 / psum * gprob
    w2 = p2 / psum * gprob
    eid_o[...] = jnp.where(lane == 0, i1 - N_GROUPS, jnp.where(lane == 1, i2 - N_GROUPS, 0))
    wt_o[...] = jnp.where(lane == 0, w1, jnp.where(lane == 1, w2, 0.0))


def _route(logits):
    m = logits.shape[0]
    tm = 512
    row = pl.BlockSpec((tm, LANES), lambda i: (i, 0))
    return pl.pallas_call(
        _route_kernel,
        grid=(m // tm,),
        in_specs=[row],
        out_specs=[row, row],
        out_shape=[jax.ShapeDtypeStruct((m, LANES), I32), jax.ShapeDtypeStruct((m, LANES), F32)],
        compiler_params=_params(("arbitrary",)),
        name="route",
    )(logits)


def _gather_rows(idx_ref, src_hbm, dst_ref, sem, n):
    def body(r, c):
        pltpu.make_async_copy(src_hbm.at[pl.ds(idx_ref[0, r], 1)], dst_ref.at[pl.ds(r, 1)], sem).start()
        return c
    lax.fori_loop(0, n, body, 0)


def _wait_rows(src_hbm, dst_ref, sem, n):
    def body(r, c):
        pltpu.make_async_copy(src_hbm.at[pl.ds(0, 1)], dst_ref.at[pl.ds(0, 1)], sem).wait()
        return c
    lax.fori_loop(0, n, body, 0)


def _experts_kernel(te_ref, nu_ref, idx_ref, idxn_ref, h_hbm, wg_ref, wu_ref, wd_ref, o_ref,
                    xbuf, wgb, wub, wdb, sems):
    i = pl.program_id(0)
    n_used = nu_ref[0]
    slot = i & 1

    @pl.when(i == 0)
    def _():
        _gather_rows(idx_ref, h_hbm, xbuf.at[0], sems.at[0], EXPERT_TILE)

    @pl.when(i + 1 < n_used)
    def _():
        _gather_rows(idxn_ref, h_hbm, xbuf.at[1 - slot], sems.at[1 - slot], EXPERT_TILE)

    @pl.when(i < n_used)
    def _():
        new_expert = jnp.logical_or(i == 0, te_ref[i] != te_ref[jnp.maximum(i - 1, 0)])

        @pl.when(new_expert)
        def _():
            wgb[...] = wg_ref[...].astype(BF16)
            wub[...] = wu_ref[...].astype(BF16)
            wdb[...] = wd_ref[...].astype(BF16)

        _wait_rows(h_hbm, xbuf.at[slot], sems.at[slot], EXPERT_TILE)
        x = _load_row_tiles(xbuf.at[slot]).astype(BF16)
        gate = jnp.dot(x, wgb[...], preferred_element_type=F32)
        up = jnp.dot(x, wub[...], preferred_element_type=F32)
        act = (gate * _sigmoid(gate) * up).astype(BF16)
        _store_row_tiles(o_ref, jnp.dot(act, wdb[...], preferred_element_type=F32))

    @pl.when(i >= n_used)
    def _():
        o_ref[...] = jnp.zeros_like(o_ref)


def _experts(tile_expert, n_used, src_rows, h2, w_gate, w_up, w_down):
    n_tiles = tile_expert.shape[0]
    d = w_gate.shape[1]
    d_tiles = d // LANES
    idx3 = src_rows.reshape(n_tiles, 1, EXPERT_TILE)
    smem_idx = lambda f: pl.BlockSpec((None, 1, EXPERT_TILE), f, memory_space=pltpu.SMEM)
    wspec = lambda shape: pl.BlockSpec((None,) + shape, lambda i, te, nu: (te[i], 0, 0))
    grid_spec = pltpu.PrefetchScalarGridSpec(
        num_scalar_prefetch=2,
        grid=(n_tiles,),
        in_specs=[smem_idx(lambda i, te, nu: (i, 0, 0)),
                  smem_idx(lambda i, te, nu: (jnp.minimum(i + 1, n_tiles - 1), 0, 0)),
                  pl.BlockSpec(memory_space=pl.ANY),
                  wspec((d, D_EXPERT)), wspec((d, D_EXPERT)), wspec((D_EXPERT, d))],
        out_specs=pl.BlockSpec((EXPERT_TILE, d_tiles, LANES), lambda i, te, nu: (i, 0, 0)),
        scratch_shapes=[pltpu.VMEM((2, EXPERT_TILE, d_tiles, LANES), F32),
                        pltpu.VMEM((d, D_EXPERT), BF16), pltpu.VMEM((d, D_EXPERT), BF16),
                        pltpu.VMEM((D_EXPERT, d), BF16),
                        pltpu.SemaphoreType.DMA((2,))],
    )
    return pl.pallas_call(
        _experts_kernel,
        grid_spec=grid_spec,
        out_shape=jax.ShapeDtypeStruct((n_tiles * EXPERT_TILE, d_tiles, LANES), F32),
        compiler_params=_params(("arbitrary",)),
        name="experts",
    )(tile_expert, n_used, idx3, idx3, h2, w_gate, w_up, w_down)


def _combine_kernel(idx_ref, idxn_ref, wt_ref, x1_ref, ga2_ref, gf_ref, ys_hbm, o_ref, ybuf, sems):
    i = pl.program_id(0)
    n_steps = pl.num_programs(0)
    slot = i & 1
    nb, tb, d = o_ref.shape
    rows = nb * tb

    def fetch(idx, s):
        for j in range(2):
            def body(r, c):
                pltpu.make_async_copy(ys_hbm.at[pl.ds(idx[0, 2 * r + j], 1)],
                                      ybuf.at[s, j, pl.ds(r, 1)], sems.at[s]).start()
                return c
            lax.fori_loop(0, rows, body, 0)

    @pl.when(i == 0)
    def _():
        fetch(idx_ref, 0)

    @pl.when(i + 1 < n_steps)
    def _():
        fetch(idxn_ref, 1 - slot)

    def wait_body(r, c):
        pltpu.make_async_copy(ys_hbm.at[pl.ds(0, 1)], ybuf.at[slot, 0, pl.ds(0, 1)], sems.at[slot]).wait()
        return c
    lax.fori_loop(0, 2 * rows, wait_body, 0)

    wt = wt_ref[...]
    moe = (wt[:, 0:1] * _load_row_tiles(ybuf.at[slot, 0])
           + wt[:, 1:2] * _load_row_tiles(ybuf.at[slot, 1]))
    x2 = x1_ref[...].reshape(nb, tb, d) + ga2_ref[...] * moe.reshape(nb, tb, d)
    ms = jnp.mean(x2 * x2, axis=-1, keepdims=True)
    o_ref[...] = x2 * lax.rsqrt(ms + RMS_EPS) * gf_ref[...]


def _combine(dest, wts, x1, mod, final_g, ys, b, t, nb, tb):
    d = x1.shape[1]
    nt = t // tb
    rows = nb * tb
    n_steps = (b // nb) * nt
    idx3 = dest.reshape(n_steps, 1, 2 * rows)
    smem_idx = lambda f: pl.BlockSpec((None, 1, 2 * rows), f, memory_space=pltpu.SMEM)
    return pl.pallas_call(
        _combine_kernel,
        grid=(n_steps,),
        in_specs=[smem_idx(lambda i: (i, 0, 0)),
                  smem_idx(lambda i: (jnp.minimum(i + 1, n_steps - 1), 0, 0)),
                  pl.BlockSpec((rows, LANES), lambda i: (i, 0)),
                  pl.BlockSpec((rows, d), lambda i: (i, 0)),
                  pl.BlockSpec((nb, 1, d), lambda i: (i // nt, 0, 5)),
                  pl.BlockSpec((1, d), lambda i: (0, 0)),
                  pl.BlockSpec(memory_space=pl.ANY)],
        out_specs=pl.BlockSpec((nb, tb, d), lambda i: (i // nt, i % nt, 0)),
        out_shape=jax.ShapeDtypeStruct((b, t, d), F32),
        scratch_shapes=[pltpu.VMEM((2, 2, rows, d // LANES, LANES), F32), pltpu.SemaphoreType.DMA((2,))],
        compiler_params=_params(("arbitrary",)),
        name="combine",
    )(idx3, idx3, wts, x1, mod, final_g, ys)


def _dispatch_plan(eid):
    n_pairs = eid.shape[0] * 2
    n_tiles = n_pairs // EXPERT_TILE + N_EXPERTS
    flat = eid.reshape(-1)
    onehot = (flat[:, None] == jnp.arange(N_EXPERTS, dtype=I32)[None, :]).astype(I32)
    csum = jnp.cumsum(onehot, axis=0)
    rank = jnp.sum((csum - onehot) * onehot, axis=1)
    counts = csum[-1]
    tiles_per = (counts + EXPERT_TILE - 1) // EXPERT_TILE
    tile_end = jnp.cumsum(tiles_per)
    row_start = (tile_end - tiles_per) * EXPERT_TILE
    dest = jnp.sum(onehot * row_start[None, :], axis=1) + rank
    n_used = tile_end[-1]
    tile_ids = jnp.arange(n_tiles, dtype=I32)
    tile_expert = jnp.sum((tile_ids[:, None] >= tile_end[None, :]).astype(I32), axis=1)
    last_expert = jnp.max(jnp.where(counts > 0, jnp.arange(N_EXPERTS, dtype=I32), 0))
    tile_expert = jnp.where(tile_ids < n_used, tile_expert, last_expert).astype(I32)
    token = jnp.arange(n_pairs, dtype=I32) // 2
    src_rows = jnp.zeros((n_tiles * EXPERT_TILE,), I32).at[dest].set(token)
    return dest.astype(I32), src_rows, tile_expert, n_used.reshape(1).astype(I32)


def _stream_mix(x, mod, conv_state, shift_state, wkv_state, lp, nb, tb, g, bl, tt):
    b, t, d = x.shape
    h = _norm1(x, mod, lp["norm1_g"], nb, tb)
    proj = _inproj(h, lp["w_in"])
    (r, w, k, v, a, bb, yc, gate, bonus, new_conv, new_shift) = _mixprep(
        proj.reshape(b, t, D_IN_PROJ), conv_state, shift_state, lp, nb, tb)
    tiles = [_to_chain_tiles(z, g, bl, t) for z in (a, w, bb, k, r)]
    y_c, s_c = _wkv(*tiles, _to_chain_rows(v, g, bl, t), _state_to_chain(wkv_state, g, bl), tt)
    y = _from_chain_rows(y_c, g, bl, t)
    new_wkv = _state_from_chain(s_c, g, bl)
    yr = _post(y, bonus, gate, lp["ln_x_w"], lp["ln_x_b"])
    x1, h2, logits = _outproj(yc, yr, lp["w_out_bf16"], x, mod, lp["norm2_g"], lp["w_router"], lp["b_router"],
                              nb, tb)
    eid, wts = _route(logits)
    return x1, h2, eid, wts, new_conv, new_shift.reshape(b, D_RWKV_PROJ), new_wkv


def kernel(x_prompt, x_sample, c_prompt, c_sample, state_conv, state_shift, state_wkv, w_ada, b_ada, norm1_g, w_in, conv_w, mix_mu, w0, w_decay_up, a0, w_aaa_up, w_gate_up, k_k, k_a, r_k, ln_x_w, ln_x_b, w_out, norm2_g, w_router_group, b_router_group, w_router_expert, b_router_expert, w_gate, w_up, w_down, final_g):
    depth = w_ada.shape[0]
    assert depth == 1
    bp, tp, d = x_prompt.shape
    bs, ts, _ = x_sample.shape
    l = 0

    row = lambda z: z.reshape(1, -1)
    zeros_lora = jnp.zeros((LORA_DECAY, D_RWKV), F32)
    w_router = jnp.concatenate([w_router_group[l], w_router_expert[l]], axis=1)
    b_router = jnp.concatenate([b_router_group[l], b_router_expert[l]])
    n_route = N_GROUPS + N_EXPERTS
    lp = dict(
        norm1_g=row(norm1_g[l]), w_in=w_in[l], conv_w=conv_w[l], mix_mu=row(mix_mu[l]), w0=row(w0[l]),
        wd_pad=jnp.concatenate([w_decay_up[l], zeros_lora], axis=0), a0=row(a0[l]),
        wa_pad=jnp.concatenate([zeros_lora, w_aaa_up[l]], axis=0), w_gate_up=w_gate_up[l],
        k_k=row(k_k[l]), k_a=row(k_a[l]), r_k=row(r_k[l]), ln_x_w=row(ln_x_w[l]), ln_x_b=row(ln_x_b[l]),
        w_out_bf16=_cast_bf16(w_out[l]), norm2_g=row(norm2_g[l]),
        w_router=jnp.pad(w_router, ((0, 0), (0, LANES - n_route))),
        b_router=jnp.pad(b_router, (0, LANES - n_route)).reshape(1, LANES),
    )

    n_c = bp + bs
    c_all = jnp.concatenate([c_prompt, c_sample], axis=0)
    c_all = jnp.pad(c_all, ((0, (-n_c) % SUBLANES), (0, 0)))
    mod = _ada(c_all, w_ada[l], row(b_ada[l]))
    mod_p = mod[:bp].reshape(bp, 1, N_MOD * d)
    mod_s = mod[bp:n_c].reshape(bs, 1, N_MOD * d)

    zc = jnp.zeros((bp, 2, D_CONV), F32)
    zs = jnp.zeros((bp, 1, D_RWKV_PROJ), F32)
    zw = jnp.zeros((bp, N_HEADS, HEAD_DIM, HEAD_DIM), F32)
    chains_b = CHAINS // N_HEADS
    x1_p, h2_p, eid_p, wts_p, conv_p, shift_p, wkv_p = _stream_mix(
        x_prompt, mod_p, zc, zs, zw, lp, nb=1, tb=ROW_TILE, g=bp // chains_b, bl=chains_b, tt=32)
    x1_s, h2_s, eid_s, wts_s, conv_s, shift_s, wkv_s = _stream_mix(
        x_sample, mod_s, state_conv[l], state_shift[l].reshape(bs, 1, D_RWKV_PROJ), state_wkv[l], lp,
        nb=ROW_TILE // ts, tb=ts, g=bs // chains_b, bl=chains_b, tt=ts)

    n_p = bp * tp
    h2 = jnp.concatenate([h2_p, h2_s], axis=0)
    eid = jnp.concatenate([eid_p[:, :2], eid_s[:, :2]], axis=0)
    dest, src_rows, tile_expert, n_used = _dispatch_plan(eid)
    ys = _experts(tile_expert, n_used, src_rows, h2, w_gate[l], w_up[l], w_down[l])
    y_prompt = _combine(dest[:2 * n_p], wts_p, x1_p, mod_p, row(final_g), ys, bp, tp, nb=1, tb=ROW_TILE)
    y_sample = _combine(dest[2 * n_p:], wts_s, x1_s, mod_s, row(final_g), ys, bs, ts, nb=ROW_TILE // ts, tb=ts)

    return (y_prompt, y_sample, conv_p[None], shift_p[None], wkv_p[None],
            conv_s[None], shift_s[None], wkv_s[None])
```

```python
import jax
import jax.numpy as jnp
from jax import lax
from jax.experimental import pallas as pl
from jax.experimental.pallas import tpu as pltpu

F32 = jnp.float32
BF16 = jnp.bfloat16
I32 = jnp.int32

D_MODEL = 2048
D_CONV = 1024
D_RWKV = 1024
HEAD_DIM = 64
N_HEADS = 16
LORA_DECAY = 64
LORA_AAA = 64
LORA_GATE = 128
D_RWKV_PROJ = 3 * D_RWKV + LORA_DECAY + LORA_AAA + LORA_GATE
D_IN_PROJ = 3 * D_CONV + D_RWKV_PROJ
N_GROUPS = 4
EXPERTS_PER_GROUP = 8
N_EXPERTS = 32
D_EXPERT = 512
N_MOD = 6
RMS_EPS = 1e-6
GN_EPS = HEAD_DIM * 1e-5

SUBLANES = 8
LANES = 128
VMEM_LIMIT_BYTES = 56 * 1024 * 1024

ROW_TILE = 256
EXPERT_TILE = 256
SEG = 256
D_TILES = D_MODEL // LANES
R_TILES = D_RWKV // LANES
GATHER_PITCH = D_TILES + SUBLANES

WKV_BATCHES = 4
V_LO = LANES // (WKV_BATCHES * R_TILES)
V_HI = HEAD_DIM // V_LO
N_WKV_OPERANDS = 6


def _params(semantics):
    return pltpu.CompilerParams(dimension_semantics=semantics, vmem_limit_bytes=VMEM_LIMIT_BYTES)


def _sigmoid(x):
    return 1.0 / (1.0 + jnp.exp(-x))


def _store_row_tiled(ref, x, pitch):
    rows = x.shape[0]
    for s in range(x.shape[1] // LANES):
        ref[pl.ds(s, rows, stride=pitch), :] = x[:, s * LANES:(s + 1) * LANES]


def _load_row_tiled(ref, rows, n, pitch):
    return jnp.concatenate([ref[pl.ds(s, rows, stride=pitch), :] for s in range(n)], axis=-1)


def _head_ones():
    r = lax.broadcasted_iota(I32, (SEG, SEG), 0) // HEAD_DIM
    c = lax.broadcasted_iota(I32, (SEG, SEG), 1) // HEAD_DIM
    return jnp.where(r == c, 1.0, 0.0).astype(BF16)


def _split_bf16(x):
    hi = x.astype(BF16)
    return hi, (x - hi.astype(F32)).astype(BF16)


def _head_sum(x, ones):
    outs = []
    for s in range(x.shape[-1] // SEG):
        hi, lo = _split_bf16(x[:, s * SEG:(s + 1) * SEG])
        outs.append(jnp.dot(hi, ones, preferred_element_type=F32) + jnp.dot(lo, ones, preferred_element_type=F32))
    return jnp.concatenate(outs, axis=-1)


def _ada_kernel(c_ref, w_ref, b_ref, o_ref):
    c = c_ref[...]
    s = c * _sigmoid(c)
    o_ref[...] = jnp.dot(s.astype(BF16), w_ref[...].astype(BF16), preferred_element_type=F32) + b_ref[...]


def _ada(c_all, w_ada, b_ada):
    mb = c_all.shape[0]
    n = w_ada.shape[1]
    tn = 1024
    return pl.pallas_call(
        _ada_kernel,
        grid=(n // tn,),
        in_specs=[pl.BlockSpec((mb, D_MODEL), lambda j: (0, 0)),
                  pl.BlockSpec((D_MODEL, tn), lambda j: (0, j)),
                  pl.BlockSpec((1, tn), lambda j: (0, j))],
        out_specs=pl.BlockSpec((mb, tn), lambda j: (0, j)),
        out_shape=jax.ShapeDtypeStruct((mb, n), F32),
        compiler_params=_params(("arbitrary",)),
        name="ada",
    )(c_all, w_ada, b_ada)


def _norm1_kernel(x_ref, g_ref, sh_ref, sc_ref, o_ref):
    x = x_ref[...]
    nb, tb, d = x.shape
    ms = jnp.mean(x * x, axis=-1, keepdims=True)
    xn = x * lax.rsqrt(ms + RMS_EPS) * g_ref[...]
    h = xn * (1.0 + sc_ref[...]) + sh_ref[...]
    o_ref[...] = h.reshape(nb * tb, d).astype(BF16)


def _norm1(x, mod, g, nb, tb):
    b, t, d = x.shape
    nt = t // tb
    return pl.pallas_call(
        _norm1_kernel,
        grid=(b // nb, nt),
        in_specs=[pl.BlockSpec((nb, tb, d), lambda i, j: (i, j, 0)),
                  pl.BlockSpec((1, d), lambda i, j: (0, 0)),
                  pl.BlockSpec((nb, 1, d), lambda i, j: (i, 0, 0)),
                  pl.BlockSpec((nb, 1, d), lambda i, j: (i, 0, 1))],
        out_specs=pl.BlockSpec((nb * tb, d), lambda i, j: (i * nt + j, 0)),
        out_shape=jax.ShapeDtypeStruct((b * t, d), BF16),
        compiler_params=_params(("arbitrary", "arbitrary")),
        name="norm1",
    )(x, g, mod, mod)


def _inproj_kernel(h_ref, w_ref, o_ref, wb_ref):
    @pl.when(pl.program_id(1) == 0)
    def _():
        wb_ref[...] = w_ref[...].astype(BF16)

    o_ref[...] = jnp.dot(h_ref[...], wb_ref[...], preferred_element_type=F32)


def _inproj(h, w_in):
    m, k = h.shape
    n = w_in.shape[1]
    tm = min(m, 1024)
    tn = 640
    return pl.pallas_call(
        _inproj_kernel,
        grid=(n // tn, m // tm),
        in_specs=[pl.BlockSpec((tm, k), lambda j, i: (i, 0)),
                  pl.BlockSpec((k, tn), lambda j, i: (0, j))],
        out_specs=pl.BlockSpec((tm, tn), lambda j, i: (i, j)),
        out_shape=jax.ShapeDtypeStruct((m, n), F32),
        scratch_shapes=[pltpu.VMEM((k, tn), BF16)],
        compiler_params=_params(("arbitrary", "arbitrary")),
        name="inproj",
    )(h, w_in)


def _mixprep_kernel(proj_ref, cst_ref, sst_ref, convw_ref, mu_ref, w0_ref, wd_ref, a0_ref, wa_ref, wg_ref,
                    kk_ref, ka_ref, rk_ref,
                    a_o, w_o, b_o, k_o, r_o, v_o, yc_o, g_o, bonus_o, nc_o, ns_o):
    @pl.when(pl.program_id(1) == 0)
    def _():
        nc_o[...] = cst_ref[...]
        ns_o[...] = sst_ref[...]

    nb, tb, _ = proj_ref.shape
    rows = nb * tb
    p = proj_ref[...].reshape(rows, D_IN_PROJ)
    tidx = lax.broadcasted_iota(I32, (rows, 1), 0) & (tb - 1)

    def per_row(c):
        return jnp.broadcast_to(c, (nb, tb, c.shape[-1])).reshape(rows, c.shape[-1])

    gate_b = p[:, 0:D_CONV]
    gate_c = p[:, D_CONV:2 * D_CONV]
    h_conv = p[:, 2 * D_CONV:3 * D_CONV]
    pr = p[:, 3 * D_CONV:]

    u = gate_c * h_conv
    carry = nc_o[...]
    c2 = per_row(carry[:, 0:1, :])
    c1 = per_row(carry[:, 1:2, :])
    u1 = jnp.where(tidx == 0, c1, pltpu.roll(u, 1, 0))
    u2 = jnp.where(tidx == 0, c2, jnp.where(tidx == 1, c1, pltpu.roll(u, 2, 0)))
    cw = convw_ref[...]
    conv_y = u * cw[2:3, :] + u2 * cw[0:1, :] + u1 * cw[1:2, :]
    yc_o[...] = (gate_b * conv_y).astype(BF16)
    nc_o[...] = u.reshape(nb, tb, D_CONV)[:, tb - 2:tb, :]

    p_prev = jnp.where(tidx == 0, per_row(ns_o[...]), pltpu.roll(pr, 1, 0))
    ns_o[...] = pr.reshape(nb, tb, D_RWKV_PROJ)[:, tb - 1:tb, :]
    pm = pr + (p_prev - pr) * mu_ref[...]
    r = pm[:, 0:D_RWKV]
    k = pm[:, D_RWKV:2 * D_RWKV]
    v = pm[:, 2 * D_RWKV:3 * D_RWKV]
    x_lora = pm[:, 3 * D_RWKV:3 * D_RWKV + LORA_DECAY + LORA_AAA]
    xg = pm[:, 3 * D_RWKV + LORA_DECAY + LORA_AAA:]

    d_pre = w0_ref[...] + jnp.dot(jnp.tanh(x_lora).astype(BF16), wd_ref[...].astype(BF16),
                                  preferred_element_type=F32)
    a_pre = a0_ref[...] + jnp.dot(x_lora.astype(BF16), wa_ref[...].astype(BF16), preferred_element_type=F32)
    g = jnp.dot(_sigmoid(xg).astype(BF16), wg_ref[...].astype(BF16), preferred_element_type=F32)
    z = -d_pre
    softplus = jnp.maximum(z, 0.0) + jnp.log(1.0 + jnp.exp(-jnp.abs(z)))
    w_log = -softplus - 0.5
    decay = jnp.exp(-jnp.exp(w_log))
    a_h = _sigmoid(a_pre)

    ones = _head_ones()
    kk = k * kk_ref[...]
    kk = kk * lax.rsqrt(jnp.maximum(_head_sum(kk * kk, ones), 1e-24))
    k_h = k * (1.0 + (a_h - 1.0) * ka_ref[...])
    bonus = _head_sum(r * k_h * rk_ref[...], ones) * v

    _store_row_tiled(a_o, -kk, R_TILES)
    _store_row_tiled(w_o, decay, R_TILES)
    _store_row_tiled(b_o, kk * a_h, R_TILES)
    _store_row_tiled(k_o, k_h, R_TILES)
    _store_row_tiled(r_o, r, R_TILES)
    _store_row_tiled(v_o, v, R_TILES)
    g_o[...] = g
    bonus_o[...] = bonus


def _mixprep(proj3, conv_state, shift_state, lp, nb, tb):
    b, t, _ = proj3.shape
    nt = t // tb
    rows = nb * tb
    m = b * t

    def row_spec(width):
        return pl.BlockSpec((rows, width), lambda i, j: (i * nt + j, 0))

    def full2(a):
        return pl.BlockSpec(a.shape, lambda i, j: (0, 0))

    consts = (lp["conv_w"], lp["mix_mu"], lp["w0"], lp["wd_pad"], lp["a0"], lp["wa_pad"], lp["w_gate_up"],
              lp["k_k"], lp["k_a"], lp["r_k"])
    tiled_spec = pl.BlockSpec((rows * R_TILES, LANES), lambda i, j: (i * nt + j, 0))
    tiled_out = jax.ShapeDtypeStruct((m * R_TILES, LANES), F32)
    f32_out = jax.ShapeDtypeStruct((m, D_RWKV), F32)
    return pl.pallas_call(
        _mixprep_kernel,
        grid=(b // nb, nt),
        in_specs=[pl.BlockSpec((nb, tb, D_IN_PROJ), lambda i, j: (i, j, 0)),
                  pl.BlockSpec((nb, 2, D_CONV), lambda i, j: (i, 0, 0)),
                  pl.BlockSpec((nb, 1, D_RWKV_PROJ), lambda i, j: (i, 0, 0))] + [full2(c) for c in consts],
        out_specs=[tiled_spec] * N_WKV_OPERANDS + [row_spec(D_CONV), row_spec(D_RWKV), row_spec(D_RWKV),
                                                   pl.BlockSpec((nb, 2, D_CONV), lambda i, j: (i, 0, 0)),
                                                   pl.BlockSpec((nb, 1, D_RWKV_PROJ), lambda i, j: (i, 0, 0))],
        out_shape=[tiled_out] * N_WKV_OPERANDS + [jax.ShapeDtypeStruct((m, D_CONV), BF16), f32_out, f32_out,
                                                  jax.ShapeDtypeStruct((b, 2, D_CONV), F32),
                                                  jax.ShapeDtypeStruct((b, 1, D_RWKV_PROJ), F32)],
        compiler_params=_params(("arbitrary", "arbitrary")),
        name="mixprep",
    )(proj3, conv_state, shift_state, *consts)


def _wkv_kernel(a_ref, w_ref, b_ref, k_ref, r_ref, v_ref, s0_ref, y_ref, s_ref, tiles, yq):
    @pl.when(pl.program_id(1) == 0)
    def _():
        s_ref[...] = s0_ref[...]

    tt = a_ref.shape[1] // R_TILES
    srcs = (a_ref, w_ref, b_ref, k_ref, r_ref, v_ref)
    n_chain = WKV_BATCHES * R_TILES
    sub8 = lax.broadcasted_iota(I32, (SUBLANES, LANES), 0)
    grp8 = lax.broadcasted_iota(I32, (SUBLANES, LANES), 1) // n_chain
    diag8 = (sub8 % V_LO) == grp8
    sub4 = lax.broadcasted_iota(I32, (V_LO, LANES), 0)
    grp4 = lax.broadcasted_iota(I32, (V_LO, LANES), 1) // n_chain
    diag4 = sub4 == grp4

    def produce(t, slot):
        row0 = pl.multiple_of(t * R_TILES, R_TILES)
        for ai, ref in enumerate(srcs):
            m = jnp.concatenate([ref[bb, pl.ds(row0, R_TILES), :] for bb in range(WKV_BATCHES)], axis=0)
            xt = jnp.concatenate([m] * V_LO, axis=0).T
            if ai < N_WKV_OPERANDS - 1:
                tiles[slot, ai] = xt
            else:
                for q in range(LANES // SUBLANES):
                    z = jnp.where(diag8, xt[q * SUBLANES:(q + 1) * SUBLANES], 0.0)
                    z = z + pltpu.roll(z, 1, 0)
                    z = z + pltpu.roll(z, 2, 0)
                    tiles[slot, ai, q * SUBLANES:(q + 1) * SUBLANES, :] = z

    def consume(slot):
        for par in range(2):
            lo = par * HEAD_DIM
            a = tiles[slot, 0, lo:lo + HEAD_DIM, :]
            w = tiles[slot, 1, lo:lo + HEAD_DIM, :]
            b = tiles[slot, 2, lo:lo + HEAD_DIM, :]
            k = tiles[slot, 3, lo:lo + HEAD_DIM, :]
            r = tiles[slot, 4, lo:lo + HEAD_DIM, :]
            for vh in range(V_HI):
                s = s_ref[0, par, vh]
                vrow = tiles[slot, 5, lo + vh * V_LO + V_LO - 1:lo + (vh + 1) * V_LO, :]
                sa = jnp.sum(s * a, axis=0, keepdims=True)
                s = s * w + sa * b + vrow * k
                s_ref[0, par, vh] = s
                yrow = jnp.sum(s * r, axis=0, keepdims=True)
                yq[slot, lo + vh * V_LO:lo + (vh + 1) * V_LO, :] = jnp.where(
                    diag4, jnp.broadcast_to(yrow, (V_LO, LANES)), 0.0)

    def finalize(t, slot):
        qt = yq[slot].T
        res = qt[0:n_chain]
        for j in range(1, V_LO):
            res = res + qt[j * n_chain:(j + 1) * n_chain]
        row0 = pl.multiple_of(t * R_TILES, R_TILES)
        for bb in range(WKV_BATCHES):
            y_ref[bb, pl.ds(row0, R_TILES), :] = res[bb * R_TILES:(bb + 1) * R_TILES]

    produce(0, 0)
    produce(1, 1)
    consume(0)

    def body(i, c):
        t1 = 2 * i + 1
        produce(t1 + 1, 0)
        consume(1)
        finalize(t1 - 1, 0)
        produce(jnp.minimum(t1 + 2, tt - 1), 1)
        consume(0)
        finalize(t1, 1)
        return c

    lax.fori_loop(0, tt // 2 - 1, body, 0)
    consume(1)
    finalize(tt - 2, 0)
    finalize(tt - 1, 1)


def _wkv(operands, s0, tt):
    bsz, rows, _ = operands[0].shape
    t = rows // R_TILES
    g = bsz // WKV_BATCHES
    assert tt % 2 == 0 and tt >= 4 and t % tt == 0
    blk = pl.BlockSpec((WKV_BATCHES, tt * R_TILES, LANES), lambda i, j: (i, j, 0))
    state = pl.BlockSpec((1, 2, V_HI, HEAD_DIM, LANES), lambda i, j: (i, 0, 0, 0, 0))
    return pl.pallas_call(
        _wkv_kernel,
        grid=(g, t // tt),
        in_specs=[blk] * N_WKV_OPERANDS + [state],
        out_specs=[blk, state],
        out_shape=[jax.ShapeDtypeStruct((bsz, rows, LANES), F32),
                   jax.ShapeDtypeStruct((g, 2, V_HI, HEAD_DIM, LANES), F32)],
        scratch_shapes=[pltpu.VMEM((2, N_WKV_OPERANDS, LANES, LANES), F32), pltpu.VMEM((2, LANES, LANES), F32)],
        compiler_params=_params(("arbitrary", "arbitrary")),
        name="wkv",
    )(*operands, s0)


def _state_to_chain(s, g):
    s = s.reshape(g, WKV_BATCHES, R_TILES, 2, V_HI, V_LO, HEAD_DIM).transpose(0, 3, 4, 6, 5, 1, 2)
    return s.reshape(g, 2, V_HI, HEAD_DIM, LANES)


def _state_from_chain(s, g):
    s = s.reshape(g, 2, V_HI, HEAD_DIM, V_LO, WKV_BATCHES, R_TILES).transpose(0, 5, 6, 1, 2, 4, 3)
    return s.reshape(g * WKV_BATCHES, N_HEADS, HEAD_DIM, HEAD_DIM)


def _post_kernel(y_ref, bonus_ref, g_ref, lnw_ref, lnb_ref, o_ref):
    ones = _head_ones()
    y = _load_row_tiled(y_ref, o_ref.shape[0], R_TILES, R_TILES)
    mu = _head_sum(y, ones) * (1.0 / HEAD_DIM)
    d = y - mu
    var = _head_sum(d * d, ones) * (1.0 / HEAD_DIM)
    yn = d * lax.rsqrt(var + GN_EPS)
    yn = yn * lnw_ref[...] + lnb_ref[...]
    o_ref[...] = ((yn + bonus_ref[...]) * g_ref[...]).astype(BF16)


def _post(y_tiled, bonus, g, ln_w, ln_b):
    m = bonus.shape[0]
    tm = 512
    row = pl.BlockSpec((tm, D_RWKV), lambda i: (i, 0))
    vec = pl.BlockSpec((1, D_RWKV), lambda i: (0, 0))
    return pl.pallas_call(
        _post_kernel,
        grid=(m // tm,),
        in_specs=[pl.BlockSpec((tm * R_TILES, LANES), lambda i: (i, 0)), row, row, vec, vec],
        out_specs=row,
        out_shape=jax.ShapeDtypeStruct((m, D_RWKV), BF16),
        compiler_params=_params(("arbitrary",)),
        name="post",
    )(y_tiled, bonus, g, ln_w, ln_b)


def _cast_kernel(x_ref, o_ref):
    o_ref[...] = x_ref[...].astype(o_ref.dtype)


def _cast_bf16(w):
    r, c = w.shape
    tr = 512
    return pl.pallas_call(
        _cast_kernel,
        grid=(r // tr,),
        in_specs=[pl.BlockSpec((tr, c), lambda i: (i, 0))],
        out_specs=pl.BlockSpec((tr, c), lambda i: (i, 0)),
        out_shape=jax.ShapeDtypeStruct((r, c), BF16),
        compiler_params=_params(("arbitrary",)),
        name="cast_bf16",
    )(w)


def _outproj_kernel(yc_ref, yr_ref, wt_ref, wb_ref, x_ref, ga1_ref, sh2_ref, sc2_ref, g2_ref, wrh_ref, wrl_ref,
                    br_ref, x1_o, h2_o, lg_o):
    nb, tb, d = x_ref.shape
    rows = nb * tb
    mix = (jnp.dot(yc_ref[...], wt_ref[...], preferred_element_type=F32)
           + jnp.dot(yr_ref[...], wb_ref[...], preferred_element_type=F32))
    x1 = x_ref[...] + ga1_ref[...] * mix.reshape(nb, tb, d)
    ms = jnp.mean(x1 * x1, axis=-1, keepdims=True)
    h2 = x1 * lax.rsqrt(ms + RMS_EPS) * g2_ref[...]
    h2 = (h2 * (1.0 + sc2_ref[...]) + sh2_ref[...]).reshape(rows, d)
    x1_o[...] = x1.reshape(rows, d)
    _store_row_tiled(h2_o, h2, D_TILES)
    h_hi, h_lo = _split_bf16(h2)
    w_hi = wrh_ref[...]
    lg_o[...] = (jnp.dot(h_hi, w_hi, preferred_element_type=F32)
                 + jnp.dot(h_lo, w_hi, preferred_element_type=F32)
                 + jnp.dot(h_hi, wrl_ref[...], preferred_element_type=F32)) + br_ref[...]


def _outproj(yc, yr, w_out_bf16, x, mod, g2, wr_hi, wr_lo, b_router, nb, tb):
    b, t, d = x.shape
    nt = t // tb
    rows = nb * tb
    m = b * t
    half = D_CONV
    row = lambda width: pl.BlockSpec((rows, width), lambda i, j: (i * nt + j, 0))
    modspec = lambda col: pl.BlockSpec((nb, 1, d), lambda i, j: (i, 0, col))
    full = lambda a: pl.BlockSpec(a.shape, lambda i, j: (0, 0))
    return pl.pallas_call(
        _outproj_kernel,
        grid=(b // nb, nt),
        in_specs=[row(half), row(half),
                  pl.BlockSpec((half, d), lambda i, j: (0, 0)),
                  pl.BlockSpec((half, d), lambda i, j: (1, 0)),
                  pl.BlockSpec((nb, tb, d), lambda i, j: (i, j, 0)),
                  modspec(2), modspec(3), modspec(4),
                  full(g2), full(wr_hi), full(wr_lo), full(b_router)],
        out_specs=[row(d), pl.BlockSpec((rows * D_TILES, LANES), lambda i, j: (i * nt + j, 0)), row(LANES)],
        out_shape=[jax.ShapeDtypeStruct((m, d), F32), jax.ShapeDtypeStruct((m * D_TILES, LANES), F32),
                   jax.ShapeDtypeStruct((m, LANES), F32)],
        compiler_params=_params(("arbitrary", "arbitrary")),
        name="outproj",
    )(yc, yr, w_out_bf16, w_out_bf16, x, mod, mod, mod, g2, wr_hi, wr_lo, b_router)


def _route_kernel(lg_ref, eid_o, wt_o):
    lg = lg_ref[...]
    lane = lax.broadcasted_iota(I32, lg.shape, 1)
    neg = -jnp.inf
    big = jnp.int32(1 << 20)

    is_g = lane < N_GROUPS
    gl = jnp.where(is_g, lg, neg)
    gmax = jnp.max(gl, axis=1, keepdims=True)
    gidx = jnp.min(jnp.where(gl == gmax, lane, big), axis=1, keepdims=True)
    gsum = jnp.sum(jnp.where(is_g, jnp.exp(gl - gmax), 0.0), axis=1, keepdims=True)
    gprob = 1.0 / gsum

    e_lane = lane - N_GROUPS
    in_grp = (e_lane >= 0) & (e_lane < N_EXPERTS) & ((e_lane >> 3) == gidx)
    el = jnp.where(in_grp, lg, neg)
    emax = jnp.max(el, axis=1, keepdims=True)
    ex = jnp.where(in_grp, jnp.exp(el - emax), 0.0)
    prob = ex / jnp.sum(ex, axis=1, keepdims=True)
    prob = jnp.where(in_grp, prob, -1.0)
    p1 = jnp.max(prob, axis=1, keepdims=True)
    i1 = jnp.min(jnp.where(prob == p1, lane, big), axis=1, keepdims=True)
    prob2 = jnp.where(lane == i1, -1.0, prob)
    p2 = jnp.max(prob2, axis=1, keepdims=True)
    i2 = jnp.min(jnp.where(prob2 == p2, lane, big), axis=1, keepdims=True)
    psum = p1 + p2
    w1 = p1 / psum * gprob
    w2 = p2 / psum * gprob
    eid_o[...] = jnp.where(lane == 0, i1 - N_GROUPS, jnp.where(lane == 1, i2 - N_GROUPS, 0))
    wt_o[...] = jnp.where(lane == 0, w1, jnp.where(lane == 1, w2, 0.0))


def _route(logits):
    m = logits.shape[0]
    tm = 512
    row = pl.BlockSpec((tm, LANES), lambda i: (i, 0))
    return pl.pallas_call(
        _route_kernel,
        grid=(m // tm,),
        in_specs=[row],
        out_specs=[row, row],
        out_shape=[jax.ShapeDtypeStruct((m, LANES), I32), jax.ShapeDtypeStruct((m, LANES), F32)],
        compiler_params=_params(("arbitrary",)),
        name="route",
    )(logits)


def _row_copy(src_hbm, src_row, dst_ref, dst_row, sem):
    src = pl.multiple_of(src_row * D_TILES, D_TILES)
    dst = pl.multiple_of(dst_row * GATHER_PITCH, SUBLANES)
    return pltpu.make_async_copy(src_hbm.at[pl.ds(src, D_TILES)], dst_ref.at[pl.ds(dst, D_TILES)], sem)


def _start_row_gather(index_of, src_hbm, dst_ref, sem, n):
    def body(r, c):
        _row_copy(src_hbm, index_of(r), dst_ref, r, sem).start()
        return c
    lax.fori_loop(0, n, body, 0, unroll=8)


def _wait_row_gather(src_hbm, dst_ref, sem, n):
    def body(r, c):
        _row_copy(src_hbm, 0, dst_ref, r, sem).wait()
        return c
    lax.fori_loop(0, n, body, 0, unroll=8)


def _experts_kernel(te_ref, nu_ref, idx_ref, idxn_ref, h_hbm, wg_ref, wu_ref, wd_ref, o_ref,
                    xbuf, wgb, wub, wdb, sems):
    i = pl.program_id(0)
    n_used = nu_ref[0]
    slot = i & 1

    @pl.when(i == 0)
    def _():
        _start_row_gather(lambda r: idx_ref[0, r], h_hbm, xbuf.at[0], sems.at[0], EXPERT_TILE)

    @pl.when(i + 1 < n_used)
    def _():
        _start_row_gather(lambda r: idxn_ref[0, r], h_hbm, xbuf.at[1 - slot], sems.at[1 - slot], EXPERT_TILE)

    @pl.when(i < n_used)
    def _():
        new_expert = jnp.logical_or(i == 0, te_ref[i] != te_ref[jnp.maximum(i - 1, 0)])

        @pl.when(new_expert)
        def _():
            wgb[...] = wg_ref[...].astype(BF16)
            wub[...] = wu_ref[...].astype(BF16)
            wdb[...] = wd_ref[...].astype(BF16)

        _wait_row_gather(h_hbm, xbuf.at[slot], sems.at[slot], EXPERT_TILE)
        x = _load_row_tiled(xbuf.at[slot], EXPERT_TILE, D_TILES, GATHER_PITCH).astype(BF16)
        gate = jnp.dot(x, wgb[...], preferred_element_type=F32)
        up = jnp.dot(x, wub[...], preferred_element_type=F32)
        act = (gate * _sigmoid(gate) * up).astype(BF16)
        _store_row_tiled(o_ref, jnp.dot(act, wdb[...], preferred_element_type=F32), D_TILES)

    @pl.when(i >= n_used)
    def _():
        o_ref[...] = jnp.zeros_like(o_ref)


def _experts(tile_expert, n_used, src_rows, h2_tiled, w_gate, w_up, w_down):
    n_tiles = tile_expert.shape[0]
    d = w_gate.shape[1]
    idx3 = src_rows.reshape(n_tiles, 1, EXPERT_TILE)
    smem_idx = lambda f: pl.BlockSpec((None, 1, EXPERT_TILE), f, memory_space=pltpu.SMEM)
    wspec = lambda shape: pl.BlockSpec((None,) + shape, lambda i, te, nu: (te[i], 0, 0))
    grid_spec = pltpu.PrefetchScalarGridSpec(
        num_scalar_prefetch=2,
        grid=(n_tiles,),
        in_specs=[smem_idx(lambda i, te, nu: (i, 0, 0)),
                  smem_idx(lambda i, te, nu: (jnp.minimum(i + 1, n_tiles - 1), 0, 0)),
                  pl.BlockSpec(memory_space=pl.ANY),
                  wspec((d, D_EXPERT)), wspec((d, D_EXPERT)), wspec((D_EXPERT, d))],
        out_specs=pl.BlockSpec((EXPERT_TILE * D_TILES, LANES), lambda i, te, nu: (i, 0)),
        scratch_shapes=[pltpu.VMEM((2, EXPERT_TILE * GATHER_PITCH, LANES), F32),
                        pltpu.VMEM((d, D_EXPERT), BF16), pltpu.VMEM((d, D_EXPERT), BF16),
                        pltpu.VMEM((D_EXPERT, d), BF16),
                        pltpu.SemaphoreType.DMA((2,))],
    )
    return pl.pallas_call(
        _experts_kernel,
        grid_spec=grid_spec,
        out_shape=jax.ShapeDtypeStruct((n_tiles * EXPERT_TILE * D_TILES, LANES), F32),
        compiler_params=_params(("arbitrary",)),
        name="experts",
    )(tile_expert, n_used, idx3, idx3, h2_tiled, w_gate, w_up, w_down)


def _combine_kernel(idx_ref, idxn_ref, wt_ref, x1_ref, ga2_ref, gf_ref, ys_hbm, o_ref, ybuf, sems):
    i = pl.program_id(0)
    n_steps = pl.num_programs(0)
    slot = i & 1
    nb, tb, d = o_ref.shape
    rows = nb * tb

    def fetch(idx, s):
        for j in range(2):
            _start_row_gather(lambda r: idx[0, 2 * r + j], ys_hbm, ybuf.at[s, j], sems.at[s], rows)

    @pl.when(i == 0)
    def _():
        fetch(idx_ref, 0)

    @pl.when(i + 1 < n_steps)
    def _():
        fetch(idxn_ref, 1 - slot)

    for j in range(2):
        _wait_row_gather(ys_hbm, ybuf.at[slot, j], sems.at[slot], rows)

    wt = wt_ref[...]
    moe = (wt[:, 0:1] * _load_row_tiled(ybuf.at[slot, 0], rows, D_TILES, GATHER_PITCH)
           + wt[:, 1:2] * _load_row_tiled(ybuf.at[slot, 1], rows, D_TILES, GATHER_PITCH))
    x2 = x1_ref[...].reshape(nb, tb, d) + ga2_ref[...] * moe.reshape(nb, tb, d)
    ms = jnp.mean(x2 * x2, axis=-1, keepdims=True)
    o_ref[...] = x2 * lax.rsqrt(ms + RMS_EPS) * gf_ref[...]


def _combine(dest, wts, x1, mod, final_g, ys, b, t, nb, tb):
    d = x1.shape[1]
    nt = t // tb
    rows = nb * tb
    n_steps = (b // nb) * nt
    idx3 = dest.reshape(n_steps, 1, 2 * rows)
    smem_idx = lambda f: pl.BlockSpec((None, 1, 2 * rows), f, memory_space=pltpu.SMEM)
    return pl.pallas_call(
        _combine_kernel,
        grid=(n_steps,),
        in_specs=[smem_idx(lambda i: (i, 0, 0)),
                  smem_idx(lambda i: (jnp.minimum(i + 1, n_steps - 1), 0, 0)),
                  pl.BlockSpec((rows, LANES), lambda i: (i, 0)),
                  pl.BlockSpec((rows, d), lambda i: (i, 0)),
                  pl.BlockSpec((nb, 1, d), lambda i: (i // nt, 0, 5)),
                  pl.BlockSpec((1, d), lambda i: (0, 0)),
                  pl.BlockSpec(memory_space=pl.ANY)],
        out_specs=pl.BlockSpec((nb, tb, d), lambda i: (i // nt, i % nt, 0)),
        out_shape=jax.ShapeDtypeStruct((b, t, d), F32),
        scratch_shapes=[pltpu.VMEM((2, 2, rows * GATHER_PITCH, LANES), F32), pltpu.SemaphoreType.DMA((2,))],
        compiler_params=_params(("arbitrary",)),
        name="combine",
    )(idx3, idx3, wts, x1, mod, final_g, ys)


def _dispatch_plan(eid):
    n_pairs = eid.shape[0] * 2
    n_tiles = n_pairs // EXPERT_TILE + N_EXPERTS
    flat = eid.reshape(-1)
    onehot = (flat[:, None] == jnp.arange(N_EXPERTS, dtype=I32)[None, :]).astype(I32)
    csum = jnp.cumsum(onehot, axis=0)
    rank = jnp.sum((csum - onehot) * onehot, axis=1)
    counts = csum[-1]
    tiles_per = (counts + EXPERT_TILE - 1) // EXPERT_TILE
    tile_end = jnp.cumsum(tiles_per)
    row_start = (tile_end - tiles_per) * EXPERT_TILE
    dest = jnp.sum(onehot * row_start[None, :], axis=1) + rank
    n_used = tile_end[-1]
    tile_ids = jnp.arange(n_tiles, dtype=I32)
    tile_expert = jnp.sum((tile_ids[:, None] >= tile_end[None, :]).astype(I32), axis=1)
    last_expert = jnp.max(jnp.where(counts > 0, jnp.arange(N_EXPERTS, dtype=I32), 0))
    tile_expert = jnp.where(tile_ids < n_used, tile_expert, last_expert).astype(I32)
    token = jnp.arange(n_pairs, dtype=I32) // 2
    src_rows = jnp.zeros((n_tiles * EXPERT_TILE,), I32).at[dest].set(token)
    return dest.astype(I32), src_rows, tile_expert, n_used.reshape(1).astype(I32)


def _stream_mix(x, mod, conv_state, shift_state, wkv_state, lp, nb, tb, tt):
    b, t, d = x.shape
    g = b // WKV_BATCHES
    h = _norm1(x, mod, lp["norm1_g"], nb, tb)
    proj = _inproj(h, lp["w_in"])
    outs = _mixprep(proj.reshape(b, t, D_IN_PROJ), conv_state, shift_state, lp, nb, tb)
    operands = [z.reshape(b, t * R_TILES, LANES) for z in outs[:N_WKV_OPERANDS]]
    yc, gate, bonus, new_conv, new_shift = outs[N_WKV_OPERANDS:]
    y_tiled, s_chain = _wkv(operands, _state_to_chain(wkv_state, g), tt)
    new_wkv = _state_from_chain(s_chain, g)
    yr = _post(y_tiled.reshape(b * t * R_TILES, LANES), bonus, gate, lp["ln_x_w"], lp["ln_x_b"])
    x1, h2_tiled, logits = _outproj(yc, yr, lp["w_out_bf16"], x, mod, lp["norm2_g"], lp["wr_hi"], lp["wr_lo"],
                                    lp["b_router"], nb, tb)
    eid, wts = _route(logits)
    return x1, h2_tiled, eid, wts, new_conv, new_shift.reshape(b, D_RWKV_PROJ), new_wkv


def kernel(x_prompt, x_sample, c_prompt, c_sample, state_conv, state_shift, state_wkv, w_ada, b_ada, norm1_g, w_in, conv_w, mix_mu, w0, w_decay_up, a0, w_aaa_up, w_gate_up, k_k, k_a, r_k, ln_x_w, ln_x_b, w_out, norm2_g, w_router_group, b_router_group, w_router_expert, b_router_expert, w_gate, w_up, w_down, final_g):
    depth = w_ada.shape[0]
    assert depth == 1
    bp, tp, d = x_prompt.shape
    bs, ts, _ = x_sample.shape
    l = 0

    row = lambda z: z.reshape(1, -1)
    zeros_lora = jnp.zeros((LORA_DECAY, D_RWKV), F32)
    n_route = N_GROUPS + N_EXPERTS
    w_router = jnp.pad(jnp.concatenate([w_router_group[l], w_router_expert[l]], axis=1),
                       ((0, 0), (0, LANES - n_route)))
    b_router = jnp.concatenate([b_router_group[l], b_router_expert[l]])
    wr_hi = w_router.astype(BF16)
    lp = dict(
        norm1_g=row(norm1_g[l]), w_in=w_in[l], conv_w=conv_w[l], mix_mu=row(mix_mu[l]), w0=row(w0[l]),
        wd_pad=jnp.concatenate([w_decay_up[l], zeros_lora], axis=0), a0=row(a0[l]),
        wa_pad=jnp.concatenate([zeros_lora, w_aaa_up[l]], axis=0), w_gate_up=w_gate_up[l],
        k_k=row(k_k[l]), k_a=row(k_a[l]), r_k=row(r_k[l]), ln_x_w=row(ln_x_w[l]), ln_x_b=row(ln_x_b[l]),
        w_out_bf16=_cast_bf16(w_out[l]), norm2_g=row(norm2_g[l]),
        wr_hi=wr_hi, wr_lo=(w_router - wr_hi.astype(F32)).astype(BF16),
        b_router=jnp.pad(b_router, (0, LANES - n_route)).reshape(1, LANES),
    )

    n_c = bp + bs
    c_all = jnp.concatenate([c_prompt, c_sample], axis=0)
    c_all = jnp.pad(c_all, ((0, (-n_c) % SUBLANES), (0, 0)))
    mod = _ada(c_all, w_ada[l], row(b_ada[l]))
    mod_p = mod[:bp].reshape(bp, 1, N_MOD * d)
    mod_s = mod[bp:n_c].reshape(bs, 1, N_MOD * d)

    zc = jnp.zeros((bp, 2, D_CONV), F32)
    zs = jnp.zeros((bp, 1, D_RWKV_PROJ), F32)
    zw = jnp.zeros((bp, N_HEADS, HEAD_DIM, HEAD_DIM), F32)
    x1_p, h2_p, eid_p, wts_p, conv_p, shift_p, wkv_p = _stream_mix(
        x_prompt, mod_p, zc, zs, zw, lp, nb=1, tb=ROW_TILE, tt=64)
    x1_s, h2_s, eid_s, wts_s, conv_s, shift_s, wkv_s = _stream_mix(
        x_sample, mod_s, state_conv[l], state_shift[l].reshape(bs, 1, D_RWKV_PROJ), state_wkv[l], lp,
        nb=ROW_TILE // ts, tb=ts, tt=ts)

    n_p = bp * tp
    h2 = jnp.concatenate([h2_p, h2_s], axis=0)
    eid = jnp.concatenate([eid_p[:, :2], eid_s[:, :2]], axis=0)
    dest, src_rows, tile_expert, n_used = _dispatch_plan(eid)
    ys = _experts(tile_expert, n_used, src_rows, h2, w_gate[l], w_up[l], w_down[l])
    y_prompt = _combine(dest[:2 * n_p], wts_p, x1_p, mod_p, row(final_g), ys, bp, tp, nb=1, tb=ROW_TILE)
    y_sample = _combine(dest[2 * n_p:], wts_s, x1_s, mod_s, row(final_g), ys, bs, ts, nb=ROW_TILE // ts, tb=ts)

    return (y_prompt, y_sample, conv_p[None], shift_p[None], wkv_p[None],
            conv_s[None], shift_s[None], wkv_s[None])
```

```python
import jax
import jax.numpy as jnp
from jax import lax
from jax.experimental import pallas as pl
from jax.experimental.pallas import tpu as pltpu

F32 = jnp.float32
BF16 = jnp.bfloat16
I32 = jnp.int32

D_MODEL = 2048
D_CONV = 1024
D_RWKV = 1024
HEAD_DIM = 64
N_HEADS = 16
LORA_DECAY = 64
LORA_AAA = 64
LORA_GATE = 128
D_RWKV_PROJ = 3 * D_RWKV + LORA_DECAY + LORA_AAA + LORA_GATE
D_IN_PROJ = 3 * D_CONV + D_RWKV_PROJ
N_GROUPS = 4
EXPERTS_PER_GROUP = 8
N_EXPERTS = 32
D_EXPERT = 512
N_MOD = 6
RMS_EPS = 1e-6
GN_EPS = HEAD_DIM * 1e-5

SUBLANES = 8
LANES = 128
VMEM_LIMIT_BYTES = 56 * 1024 * 1024

ROW_TILE = 256
EXPERT_TILE = 256
SEG = 256
D_TILES = D_MODEL // LANES
R_TILES = D_RWKV // LANES
GATHER_PITCH = D_TILES + SUBLANES
GATHER_DMA_PRIORITY = 1

WKV_BATCHES = 4
V_LO = LANES // (WKV_BATCHES * R_TILES)
V_HI = HEAD_DIM // V_LO
N_WKV_OPERANDS = 6


def _params(semantics):
    return pltpu.CompilerParams(dimension_semantics=semantics, vmem_limit_bytes=VMEM_LIMIT_BYTES)


def _sigmoid(x):
    return 1.0 / (1.0 + jnp.exp(-x))


def _store_row_tiled(ref, x, pitch):
    rows = x.shape[0]
    for s in range(x.shape[1] // LANES):
        ref[pl.ds(s, rows, stride=pitch), :] = x[:, s * LANES:(s + 1) * LANES]


def _load_row_tiled(ref, rows, n, pitch):
    return jnp.concatenate([ref[pl.ds(s, rows, stride=pitch), :] for s in range(n)], axis=-1)


def _head_ones():
    r = lax.broadcasted_iota(I32, (SEG, SEG), 0) // HEAD_DIM
    c = lax.broadcasted_iota(I32, (SEG, SEG), 1) // HEAD_DIM
    return jnp.where(r == c, 1.0, 0.0).astype(BF16)


def _split_bf16(x):
    hi = x.astype(BF16)
    return hi, (x - hi.astype(F32)).astype(BF16)


def _head_sum(x, ones):
    outs = []
    for s in range(x.shape[-1] // SEG):
        hi, lo = _split_bf16(x[:, s * SEG:(s + 1) * SEG])
        outs.append(jnp.dot(hi, ones, preferred_element_type=F32) + jnp.dot(lo, ones, preferred_element_type=F32))
    return jnp.concatenate(outs, axis=-1)


def _ada_kernel(c_ref, w_ref, b_ref, o_ref):
    c = c_ref[...]
    s = c * _sigmoid(c)
    o_ref[...] = jnp.dot(s.astype(BF16), w_ref[...].astype(BF16), preferred_element_type=F32) + b_ref[...]


def _ada(c_all, w_ada, b_ada):
    mb = c_all.shape[0]
    n = w_ada.shape[1]
    tn = 1024
    return pl.pallas_call(
        _ada_kernel,
        grid=(n // tn,),
        in_specs=[pl.BlockSpec((mb, D_MODEL), lambda j: (0, 0)),
                  pl.BlockSpec((D_MODEL, tn), lambda j: (0, j)),
                  pl.BlockSpec((1, tn), lambda j: (0, j))],
        out_specs=pl.BlockSpec((mb, tn), lambda j: (0, j)),
        out_shape=jax.ShapeDtypeStruct((mb, n), F32),
        compiler_params=_params(("arbitrary",)),
        name="ada",
    )(c_all, w_ada, b_ada)


def _norm1_kernel(x_ref, g_ref, sh_ref, sc_ref, o_ref):
    x = x_ref[...]
    nb, tb, d = x.shape
    ms = jnp.mean(x * x, axis=-1, keepdims=True)
    xn = x * lax.rsqrt(ms + RMS_EPS) * g_ref[...]
    h = xn * (1.0 + sc_ref[...]) + sh_ref[...]
    o_ref[...] = h.reshape(nb * tb, d).astype(BF16)


def _norm1(x, mod, g, nb, tb):
    b, t, d = x.shape
    nt = t // tb
    return pl.pallas_call(
        _norm1_kernel,
        grid=(b // nb, nt),
        in_specs=[pl.BlockSpec((nb, tb, d), lambda i, j: (i, j, 0)),
                  pl.BlockSpec((1, d), lambda i, j: (0, 0)),
                  pl.BlockSpec((nb, 1, d), lambda i, j: (i, 0, 0)),
                  pl.BlockSpec((nb, 1, d), lambda i, j: (i, 0, 1))],
        out_specs=pl.BlockSpec((nb * tb, d), lambda i, j: (i * nt + j, 0)),
        out_shape=jax.ShapeDtypeStruct((b * t, d), BF16),
        compiler_params=_params(("arbitrary", "arbitrary")),
        name="norm1",
    )(x, g, mod, mod)


def _inproj_kernel(h_ref, w_ref, o_ref, wb_ref):
    @pl.when(pl.program_id(1) == 0)
    def _():
        wb_ref[...] = w_ref[...].astype(BF16)

    o_ref[...] = jnp.dot(h_ref[...], wb_ref[...], preferred_element_type=F32)


def _inproj(h, w_in):
    m, k = h.shape
    n = w_in.shape[1]
    tm = min(m, 1024)
    tn = 1280
    return pl.pallas_call(
        _inproj_kernel,
        grid=(n // tn, m // tm),
        in_specs=[pl.BlockSpec((tm, k), lambda j, i: (i, 0)),
                  pl.BlockSpec((k, tn), lambda j, i: (0, j))],
        out_specs=pl.BlockSpec((tm, tn), lambda j, i: (i, j)),
        out_shape=jax.ShapeDtypeStruct((m, n), F32),
        scratch_shapes=[pltpu.VMEM((k, tn), BF16)],
        compiler_params=_params(("arbitrary", "arbitrary")),
        name="inproj",
    )(h, w_in)


def _mixprep_kernel(proj_ref, cst_ref, sst_ref, convw_ref, mu_ref, w0_ref, wd_ref, a0_ref, wa_ref, wg_ref,
                    kk_ref, ka_ref, rk_ref,
                    a_o, w_o, b_o, k_o, r_o, v_o, yc_o, g_o, bonus_o, nc_o, ns_o):
    @pl.when(pl.program_id(1) == 0)
    def _():
        nc_o[...] = cst_ref[...]
        ns_o[...] = sst_ref[...]

    nb, tb, _ = proj_ref.shape
    rows = nb * tb
    p = proj_ref[...].reshape(rows, D_IN_PROJ)
    tidx = lax.broadcasted_iota(I32, (rows, 1), 0) & (tb - 1)

    def per_row(c):
        return jnp.broadcast_to(c, (nb, tb, c.shape[-1])).reshape(rows, c.shape[-1])

    gate_b = p[:, 0:D_CONV]
    gate_c = p[:, D_CONV:2 * D_CONV]
    h_conv = p[:, 2 * D_CONV:3 * D_CONV]
    pr = p[:, 3 * D_CONV:]

    u = gate_c * h_conv
    carry = nc_o[...]
    c2 = per_row(carry[:, 0:1, :])
    c1 = per_row(carry[:, 1:2, :])
    u1 = jnp.where(tidx == 0, c1, pltpu.roll(u, 1, 0))
    u2 = jnp.where(tidx == 0, c2, jnp.where(tidx == 1, c1, pltpu.roll(u, 2, 0)))
    cw = convw_ref[...]
    conv_y = u * cw[2:3, :] + u2 * cw[0:1, :] + u1 * cw[1:2, :]
    yc_o[...] = (gate_b * conv_y).astype(BF16)
    nc_o[...] = u.reshape(nb, tb, D_CONV)[:, tb - 2:tb, :]

    p_prev = jnp.where(tidx == 0, per_row(ns_o[...]), pltpu.roll(pr, 1, 0))
    ns_o[...] = pr.reshape(nb, tb, D_RWKV_PROJ)[:, tb - 1:tb, :]
    pm = pr + (p_prev - pr) * mu_ref[...]
    r = pm[:, 0:D_RWKV]
    k = pm[:, D_RWKV:2 * D_RWKV]
    v = pm[:, 2 * D_RWKV:3 * D_RWKV]
    x_lora = pm[:, 3 * D_RWKV:3 * D_RWKV + LORA_DECAY + LORA_AAA]
    xg = pm[:, 3 * D_RWKV + LORA_DECAY + LORA_AAA:]

    d_pre = w0_ref[...] + jnp.dot(jnp.tanh(x_lora).astype(BF16), wd_ref[...].astype(BF16),
                                  preferred_element_type=F32)
    a_pre = a0_ref[...] + jnp.dot(x_lora.astype(BF16), wa_ref[...].astype(BF16), preferred_element_type=F32)
    g = jnp.dot(_sigmoid(xg).astype(BF16), wg_ref[...].astype(BF16), preferred_element_type=F32)
    z = -d_pre
    softplus = jnp.maximum(z, 0.0) + jnp.log(1.0 + jnp.exp(-jnp.abs(z)))
    w_log = -softplus - 0.5
    decay = jnp.exp(-jnp.exp(w_log))
    a_h = _sigmoid(a_pre)

    ones = _head_ones()
    kk = k * kk_ref[...]
    kk = kk * lax.rsqrt(jnp.maximum(_head_sum(kk * kk, ones), 1e-24))
    k_h = k * (1.0 + (a_h - 1.0) * ka_ref[...])
    bonus = _head_sum(r * k_h * rk_ref[...], ones) * v

    _store_row_tiled(a_o, -kk, R_TILES)
    _store_row_tiled(w_o, decay, R_TILES)
    _store_row_tiled(b_o, kk * a_h, R_TILES)
    _store_row_tiled(k_o, k_h, R_TILES)
    _store_row_tiled(r_o, r, R_TILES)
    _store_row_tiled(v_o, v, R_TILES)
    g_o[...] = g
    bonus_o[...] = bonus


def _mixprep(proj3, conv_state, shift_state, lp, nb, tb):
    b, t, _ = proj3.shape
    nt = t // tb
    rows = nb * tb
    m = b * t

    def row_spec(width):
        return pl.BlockSpec((rows, width), lambda i, j: (i * nt + j, 0))

    def full2(a):
        return pl.BlockSpec(a.shape, lambda i, j: (0, 0))

    consts = (lp["conv_w"], lp["mix_mu"], lp["w0"], lp["wd_pad"], lp["a0"], lp["wa_pad"], lp["w_gate_up"],
              lp["k_k"], lp["k_a"], lp["r_k"])
    tiled_spec = pl.BlockSpec((rows * R_TILES, LANES), lambda i, j: (i * nt + j, 0))
    tiled_out = jax.ShapeDtypeStruct((m * R_TILES, LANES), F32)
    f32_out = jax.ShapeDtypeStruct((m, D_RWKV), F32)
    return pl.pallas_call(
        _mixprep_kernel,
        grid=(b // nb, nt),
        in_specs=[pl.BlockSpec((nb, tb, D_IN_PROJ), lambda i, j: (i, j, 0)),
                  pl.BlockSpec((nb, 2, D_CONV), lambda i, j: (i, 0, 0)),
                  pl.BlockSpec((nb, 1, D_RWKV_PROJ), lambda i, j: (i, 0, 0))] + [full2(c) for c in consts],
        out_specs=[tiled_spec] * N_WKV_OPERANDS + [row_spec(D_CONV), row_spec(D_RWKV), row_spec(D_RWKV),
                                                   pl.BlockSpec((nb, 2, D_CONV), lambda i, j: (i, 0, 0)),
                                                   pl.BlockSpec((nb, 1, D_RWKV_PROJ), lambda i, j: (i, 0, 0))],
        out_shape=[tiled_out] * N_WKV_OPERANDS + [jax.ShapeDtypeStruct((m, D_CONV), BF16), f32_out, f32_out,
                                                  jax.ShapeDtypeStruct((b, 2, D_CONV), F32),
                                                  jax.ShapeDtypeStruct((b, 1, D_RWKV_PROJ), F32)],
        compiler_params=_params(("arbitrary", "arbitrary")),
        name="mixprep",
    )(proj3, conv_state, shift_state, *consts)


def _wkv_kernel(a_ref, w_ref, b_ref, k_ref, r_ref, v_ref, s0_ref, y_ref, s_ref, tiles, yq):
    @pl.when(pl.program_id(1) == 0)
    def _():
        s_ref[...] = s0_ref[...]

    tt = a_ref.shape[1] // R_TILES
    srcs = (a_ref, w_ref, b_ref, k_ref, r_ref, v_ref)
    n_chain = WKV_BATCHES * R_TILES
    sub8 = lax.broadcasted_iota(I32, (SUBLANES, LANES), 0)
    grp8 = lax.broadcasted_iota(I32, (SUBLANES, LANES), 1) // n_chain
    diag8 = (sub8 % V_LO) == grp8
    sub4 = lax.broadcasted_iota(I32, (V_LO, LANES), 0)
    grp4 = lax.broadcasted_iota(I32, (V_LO, LANES), 1) // n_chain
    diag4 = sub4 == grp4

    def produce(t, slot):
        row0 = pl.multiple_of(t * R_TILES, R_TILES)
        for ai, ref in enumerate(srcs):
            m = jnp.concatenate([ref[bb, pl.ds(row0, R_TILES), :] for bb in range(WKV_BATCHES)], axis=0)
            xt = jnp.concatenate([m] * V_LO, axis=0).T
            if ai < N_WKV_OPERANDS - 1:
                tiles[slot, ai] = xt
            else:
                for q in range(LANES // SUBLANES):
                    z = jnp.where(diag8, xt[q * SUBLANES:(q + 1) * SUBLANES], 0.0)
                    z = z + pltpu.roll(z, 1, 0)
                    z = z + pltpu.roll(z, 2, 0)
                    tiles[slot, ai, q * SUBLANES:(q + 1) * SUBLANES, :] = z

    def consume(slot):
        for par in range(2):
            lo = par * HEAD_DIM
            a = tiles[slot, 0, lo:lo + HEAD_DIM, :]
            w = tiles[slot, 1, lo:lo + HEAD_DIM, :]
            b = tiles[slot, 2, lo:lo + HEAD_DIM, :]
            k = tiles[slot, 3, lo:lo + HEAD_DIM, :]
            r = tiles[slot, 4, lo:lo + HEAD_DIM, :]
            for vh in range(V_HI):
                s = s_ref[0, par, vh]
                vrow = tiles[slot, 5, lo + vh * V_LO + V_LO - 1:lo + (vh + 1) * V_LO, :]
                sa = jnp.sum(s * a, axis=0, keepdims=True)
                s = s * w + sa * b + vrow * k
                s_ref[0, par, vh] = s
                yrow = jnp.sum(s * r, axis=0, keepdims=True)
                yq[slot, lo + vh * V_LO:lo + (vh + 1) * V_LO, :] = jnp.where(
                    diag4, jnp.broadcast_to(yrow, (V_LO, LANES)), 0.0)

    def finalize(t, slot):
        qt = yq[slot].T
        res = qt[0:n_chain]
        for j in range(1, V_LO):
            res = res + qt[j * n_chain:(j + 1) * n_chain]
        row0 = pl.multiple_of(t * R_TILES, R_TILES)
        for bb in range(WKV_BATCHES):
            y_ref[bb, pl.ds(row0, R_TILES), :] = res[bb * R_TILES:(bb + 1) * R_TILES]

    produce(0, 0)
    produce(1, 1)
    consume(0)

    def body(i, c):
        t1 = 2 * i + 1
        produce(t1 + 1, 0)
        consume(1)
        finalize(t1 - 1, 0)
        produce(jnp.minimum(t1 + 2, tt - 1), 1)
        consume(0)
        finalize(t1, 1)
        return c

    lax.fori_loop(0, tt // 2 - 1, body, 0)
    consume(1)
    finalize(tt - 2, 0)
    finalize(tt - 1, 1)


def _wkv(operands, s0, tt):
    bsz, rows, _ = operands[0].shape
    t = rows // R_TILES
    g = bsz // WKV_BATCHES
    assert tt % 2 == 0 and tt >= 4 and t % tt == 0
    blk = pl.BlockSpec((WKV_BATCHES, tt * R_TILES, LANES), lambda i, j: (i, j, 0))
    state = pl.BlockSpec((1, 2, V_HI, HEAD_DIM, LANES), lambda i, j: (i, 0, 0, 0, 0))
    return pl.pallas_call(
        _wkv_kernel,
        grid=(g, t // tt),
        in_specs=[blk] * N_WKV_OPERANDS + [state],
        out_specs=[blk, state],
        out_shape=[jax.ShapeDtypeStruct((bsz, rows, LANES), F32),
                   jax.ShapeDtypeStruct((g, 2, V_HI, HEAD_DIM, LANES), F32)],
        scratch_shapes=[pltpu.VMEM((2, N_WKV_OPERANDS, LANES, LANES), F32), pltpu.VMEM((2, LANES, LANES), F32)],
        compiler_params=_params(("arbitrary", "arbitrary")),
        name="wkv",
    )(*operands, s0)


def _state_to_chain(s, g):
    s = s.reshape(g, WKV_BATCHES, R_TILES, 2, V_HI, V_LO, HEAD_DIM).transpose(0, 3, 4, 6, 5, 1, 2)
    return s.reshape(g, 2, V_HI, HEAD_DIM, LANES)


def _state_from_chain(s, g):
    s = s.reshape(g, 2, V_HI, HEAD_DIM, V_LO, WKV_BATCHES, R_TILES).transpose(0, 5, 6, 1, 2, 4, 3)
    return s.reshape(g * WKV_BATCHES, N_HEADS, HEAD_DIM, HEAD_DIM)


def _post_kernel(y_ref, bonus_ref, g_ref, lnw_ref, lnb_ref, o_ref):
    ones = _head_ones()
    y = _load_row_tiled(y_ref, o_ref.shape[0], R_TILES, R_TILES)
    mu = _head_sum(y, ones) * (1.0 / HEAD_DIM)
    d = y - mu
    var = _head_sum(d * d, ones) * (1.0 / HEAD_DIM)
    yn = d * lax.rsqrt(var + GN_EPS)
    yn = yn * lnw_ref[...] + lnb_ref[...]
    o_ref[...] = ((yn + bonus_ref[...]) * g_ref[...]).astype(BF16)


def _post(y_tiled, bonus, g, ln_w, ln_b):
    m = bonus.shape[0]
    tm = 512
    row = pl.BlockSpec((tm, D_RWKV), lambda i: (i, 0))
    vec = pl.BlockSpec((1, D_RWKV), lambda i: (0, 0))
    return pl.pallas_call(
        _post_kernel,
        grid=(m // tm,),
        in_specs=[pl.BlockSpec((tm * R_TILES, LANES), lambda i: (i, 0)), row, row, vec, vec],
        out_specs=row,
        out_shape=jax.ShapeDtypeStruct((m, D_RWKV), BF16),
        compiler_params=_params(("arbitrary",)),
        name="post",
    )(y_tiled, bonus, g, ln_w, ln_b)


def _cast_kernel(x_ref, o_ref):
    o_ref[...] = x_ref[...].astype(o_ref.dtype)


def _cast_bf16(w):
    r, c = w.shape
    tr = 512
    return pl.pallas_call(
        _cast_kernel,
        grid=(r // tr,),
        in_specs=[pl.BlockSpec((tr, c), lambda i: (i, 0))],
        out_specs=pl.BlockSpec((tr, c), lambda i: (i, 0)),
        out_shape=jax.ShapeDtypeStruct((r, c), BF16),
        compiler_params=_params(("arbitrary",)),
        name="cast_bf16",
    )(w)


def _route(lg):
    lane = lax.broadcasted_iota(I32, lg.shape, 1)
    neg = -jnp.inf
    big = jnp.int32(1 << 20)

    is_g = lane < N_GROUPS
    gl = jnp.where(is_g, lg, neg)
    gmax = jnp.max(gl, axis=1, keepdims=True)
    gidx = jnp.min(jnp.where(gl == gmax, lane, big), axis=1, keepdims=True)
    gsum = jnp.sum(jnp.where(is_g, jnp.exp(gl - gmax), 0.0), axis=1, keepdims=True)
    gprob = 1.0 / gsum

    e_lane = lane - N_GROUPS
    in_grp = (e_lane >= 0) & (e_lane < N_EXPERTS) & ((e_lane >> 3) == gidx)
    el = jnp.where(in_grp, lg, neg)
    emax = jnp.max(el, axis=1, keepdims=True)
    ex = jnp.where(in_grp, jnp.exp(el - emax), 0.0)
    prob = ex / jnp.sum(ex, axis=1, keepdims=True)
    prob = jnp.where(in_grp, prob, -1.0)
    p1 = jnp.max(prob, axis=1, keepdims=True)
    i1 = jnp.min(jnp.where(prob == p1, lane, big), axis=1, keepdims=True)
    prob2 = jnp.where(lane == i1, -1.0, prob)
    p2 = jnp.max(prob2, axis=1, keepdims=True)
    i2 = jnp.min(jnp.where(prob2 == p2, lane, big), axis=1, keepdims=True)
    psum = p1 + p2
    w1 = p1 / psum * gprob
    w2 = p2 / psum * gprob
    eid = jnp.where(lane == 0, i1 - N_GROUPS, jnp.where(lane == 1, i2 - N_GROUPS, 0))
    wts = jnp.where(lane == 0, w1, jnp.where(lane == 1, w2, 0.0))
    return eid, wts


def _outproj_kernel(yc_ref, yr_ref, wt_ref, wb_ref, x_ref, ga1_ref, sh2_ref, sc2_ref, g2_ref, wrh_ref, wrl_ref,
                    br_ref, x1_o, h2_o, eid_o, wts_o):
    nb, tb, d = x_ref.shape
    rows = nb * tb
    mix = (jnp.dot(yc_ref[...], wt_ref[...], preferred_element_type=F32)
           + jnp.dot(yr_ref[...], wb_ref[...], preferred_element_type=F32))
    x1 = x_ref[...] + ga1_ref[...] * mix.reshape(nb, tb, d)
    ms = jnp.mean(x1 * x1, axis=-1, keepdims=True)
    h2 = x1 * lax.rsqrt(ms + RMS_EPS) * g2_ref[...]
    h2 = (h2 * (1.0 + sc2_ref[...]) + sh2_ref[...]).reshape(rows, d)
    x1_o[...] = x1.reshape(rows, d)
    _store_row_tiled(h2_o, h2, D_TILES)
    h_hi, h_lo = _split_bf16(h2)
    w_hi = wrh_ref[...]
    logits = (jnp.dot(h_hi, w_hi, preferred_element_type=F32)
              + jnp.dot(h_lo, w_hi, preferred_element_type=F32)
              + jnp.dot(h_hi, wrl_ref[...], preferred_element_type=F32)) + br_ref[...]
    eid_o[...], wts_o[...] = _route(logits)


def _outproj(yc, yr, w_out_bf16, x, mod, g2, wr_hi, wr_lo, b_router, nb, tb):
    b, t, d = x.shape
    nt = t // tb
    rows = nb * tb
    m = b * t
    half = D_CONV
    row = lambda width: pl.BlockSpec((rows, width), lambda i, j: (i * nt + j, 0))
    modspec = lambda col: pl.BlockSpec((nb, 1, d), lambda i, j: (i, 0, col))
    full = lambda a: pl.BlockSpec(a.shape, lambda i, j: (0, 0))
    return pl.pallas_call(
        _outproj_kernel,
        grid=(b // nb, nt),
        in_specs=[row(half), row(half),
                  pl.BlockSpec((half, d), lambda i, j: (0, 0)),
                  pl.BlockSpec((half, d), lambda i, j: (1, 0)),
                  pl.BlockSpec((nb, tb, d), lambda i, j: (i, j, 0)),
                  modspec(2), modspec(3), modspec(4),
                  full(g2), full(wr_hi), full(wr_lo), full(b_router)],
        out_specs=[row(d), pl.BlockSpec((rows * D_TILES, LANES), lambda i, j: (i * nt + j, 0)),
                   row(LANES), row(LANES)],
        out_shape=[jax.ShapeDtypeStruct((m, d), F32), jax.ShapeDtypeStruct((m * D_TILES, LANES), F32),
                   jax.ShapeDtypeStruct((m, LANES), I32), jax.ShapeDtypeStruct((m, LANES), F32)],
        compiler_params=_params(("arbitrary", "arbitrary")),
        name="outproj",
    )(yc, yr, w_out_bf16, w_out_bf16, x, mod, mod, mod, g2, wr_hi, wr_lo, b_router)


def _row_copy(src_hbm, src_row, dst_ref, dst_row, sem):
    src = pl.multiple_of(src_row * D_TILES, D_TILES)
    dst = pl.multiple_of(dst_row * GATHER_PITCH, SUBLANES)
    return pltpu.make_async_copy(src_hbm.at[pl.ds(src, D_TILES)], dst_ref.at[pl.ds(dst, D_TILES)], sem)


def _start_row_gather(index_of, src_hbm, dst_ref, sem, n):
    def body(r, c):
        _row_copy(src_hbm, index_of(r), dst_ref, r, sem).start(priority=GATHER_DMA_PRIORITY)
        return c
    lax.fori_loop(0, n, body, 0, unroll=8)


def _wait_row_gather(src_hbm, dst_ref, sem, n):
    def body(r, c):
        _row_copy(src_hbm, 0, dst_ref, r, sem).wait()
        return c
    lax.fori_loop(0, n, body, 0, unroll=8)


def _experts_kernel(te_ref, nu_ref, idx_ref, idxn_ref, h_hbm, wg_ref, wu_ref, wd_ref, o_ref,
                    xbuf, wgb, wub, wdb, sems):
    i = pl.program_id(0)
    n_used = nu_ref[0]
    slot = i & 1

    @pl.when(i == 0)
    def _():
        _start_row_gather(lambda r: idx_ref[0, r], h_hbm, xbuf.at[0], sems.at[0], EXPERT_TILE)

    @pl.when(i + 1 < n_used)
    def _():
        _start_row_gather(lambda r: idxn_ref[0, r], h_hbm, xbuf.at[1 - slot], sems.at[1 - slot], EXPERT_TILE)

    @pl.when(i < n_used)
    def _():
        new_expert = jnp.logical_or(i == 0, te_ref[i] != te_ref[jnp.maximum(i - 1, 0)])

        @pl.when(new_expert)
        def _():
            wgb[...] = wg_ref[...].astype(BF16)
            wub[...] = wu_ref[...].astype(BF16)
            wdb[...] = wd_ref[...].astype(BF16)

        _wait_row_gather(h_hbm, xbuf.at[slot], sems.at[slot], EXPERT_TILE)
        x = _load_row_tiled(xbuf.at[slot], EXPERT_TILE, D_TILES, GATHER_PITCH).astype(BF16)
        gate = jnp.dot(x, wgb[...], preferred_element_type=F32)
        up = jnp.dot(x, wub[...], preferred_element_type=F32)
        act = (gate * _sigmoid(gate) * up).astype(BF16)
        _store_row_tiled(o_ref, jnp.dot(act, wdb[...], preferred_element_type=F32), D_TILES)

    @pl.when(i >= n_used)
    def _():
        o_ref[...] = jnp.zeros_like(o_ref)


def _experts(tile_expert, n_used, src_rows, h2_tiled, w_gate, w_up, w_down):
    n_tiles = tile_expert.shape[0]
    d = w_gate.shape[1]
    idx3 = src_rows.reshape(n_tiles, 1, EXPERT_TILE)
    smem_idx = lambda f: pl.BlockSpec((None, 1, EXPERT_TILE), f, memory_space=pltpu.SMEM)
    wspec = lambda shape: pl.BlockSpec((None,) + shape, lambda i, te, nu: (te[i], 0, 0))
    grid_spec = pltpu.PrefetchScalarGridSpec(
        num_scalar_prefetch=2,
        grid=(n_tiles,),
        in_specs=[smem_idx(lambda i, te, nu: (i, 0, 0)),
                  smem_idx(lambda i, te, nu: (jnp.minimum(i + 1, n_tiles - 1), 0, 0)),
                  pl.BlockSpec(memory_space=pl.ANY),
                  wspec((d, D_EXPERT)), wspec((d, D_EXPERT)), wspec((D_EXPERT, d))],
        out_specs=pl.BlockSpec((EXPERT_TILE * D_TILES, LANES), lambda i, te, nu: (i, 0)),
        scratch_shapes=[pltpu.VMEM((2, EXPERT_TILE * GATHER_PITCH, LANES), F32),
                        pltpu.VMEM((d, D_EXPERT), BF16), pltpu.VMEM((d, D_EXPERT), BF16),
                        pltpu.VMEM((D_EXPERT, d), BF16),
                        pltpu.SemaphoreType.DMA((2,))],
    )
    return pl.pallas_call(
        _experts_kernel,
        grid_spec=grid_spec,
        out_shape=jax.ShapeDtypeStruct((n_tiles * EXPERT_TILE * D_TILES, LANES), F32),
        compiler_params=_params(("arbitrary",)),
        name="experts",
    )(tile_expert, n_used, idx3, idx3, h2_tiled, w_gate, w_up, w_down)


def _combine_kernel(idx_ref, idxn_ref, wt_ref, x1_ref, ga2_ref, gf_ref, ys_hbm, o_ref, ybuf, sems):
    i = pl.program_id(0)
    n_steps = pl.num_programs(0)
    slot = i & 1
    nb, tb, d = o_ref.shape
    rows = nb * tb

    def fetch(idx, s):
        for j in range(2):
            _start_row_gather(lambda r: idx[0, 2 * r + j], ys_hbm, ybuf.at[s, j], sems.at[s], rows)

    @pl.when(i == 0)
    def _():
        fetch(idx_ref, 0)

    @pl.when(i + 1 < n_steps)
    def _():
        fetch(idxn_ref, 1 - slot)

    for j in range(2):
        _wait_row_gather(ys_hbm, ybuf.at[slot, j], sems.at[slot], rows)

    wt = wt_ref[...]
    moe = (wt[:, 0:1] * _load_row_tiled(ybuf.at[slot, 0], rows, D_TILES, GATHER_PITCH)
           + wt[:, 1:2] * _load_row_tiled(ybuf.at[slot, 1], rows, D_TILES, GATHER_PITCH))
    x2 = x1_ref[...].reshape(nb, tb, d) + ga2_ref[...] * moe.reshape(nb, tb, d)
    ms = jnp.mean(x2 * x2, axis=-1, keepdims=True)
    o_ref[...] = x2 * lax.rsqrt(ms + RMS_EPS) * gf_ref[...]


def _combine(dest, wts, x1, mod, final_g, ys, b, t, nb, tb):
    d = x1.shape[1]
    nt = t // tb
    rows = nb * tb
    n_steps = (b // nb) * nt
    idx3 = dest.reshape(n_steps, 1, 2 * rows)
    smem_idx = lambda f: pl.BlockSpec((None, 1, 2 * rows), f, memory_space=pltpu.SMEM)
    return pl.pallas_call(
        _combine_kernel,
        grid=(n_steps,),
        in_specs=[smem_idx(lambda i: (i, 0, 0)),
                  smem_idx(lambda i: (jnp.minimum(i + 1, n_steps - 1), 0, 0)),
                  pl.BlockSpec((rows, LANES), lambda i: (i, 0)),
                  pl.BlockSpec((rows, d), lambda i: (i, 0)),
                  pl.BlockSpec((nb, 1, d), lambda i: (i // nt, 0, 5)),
                  pl.BlockSpec((1, d), lambda i: (0, 0)),
                  pl.BlockSpec(memory_space=pl.ANY)],
        out_specs=pl.BlockSpec((nb, tb, d), lambda i: (i // nt, i % nt, 0)),
        out_shape=jax.ShapeDtypeStruct((b, t, d), F32),
        scratch_shapes=[pltpu.VMEM((2, 2, rows * GATHER_PITCH, LANES), F32), pltpu.SemaphoreType.DMA((2,))],
        compiler_params=_params(("arbitrary",)),
        name="combine",
    )(idx3, idx3, wts, x1, mod, final_g, ys)


def _dispatch_plan(eid):
    n_pairs = eid.shape[0] * 2
    n_tiles = n_pairs // EXPERT_TILE + N_EXPERTS
    flat = eid.reshape(-1)
    onehot = (flat[:, None] == jnp.arange(N_EXPERTS, dtype=I32)[None, :]).astype(I32)
    csum = jnp.cumsum(onehot, axis=0)
    rank = jnp.sum((csum - onehot) * onehot, axis=1)
    counts = csum[-1]
    tiles_per = (counts + EXPERT_TILE - 1) // EXPERT_TILE
    tile_end = jnp.cumsum(tiles_per)
    row_start = (tile_end - tiles_per) * EXPERT_TILE
    dest = jnp.sum(onehot * row_start[None, :], axis=1) + rank
    n_used = tile_end[-1]
    tile_ids = jnp.arange(n_tiles, dtype=I32)
    tile_expert = jnp.sum((tile_ids[:, None] >= tile_end[None, :]).astype(I32), axis=1)
    last_expert = jnp.max(jnp.where(counts > 0, jnp.arange(N_EXPERTS, dtype=I32), 0))
    tile_expert = jnp.where(tile_ids < n_used, tile_expert, last_expert).astype(I32)
    token = jnp.arange(n_pairs, dtype=I32) // 2
    src_rows = jnp.zeros((n_tiles * EXPERT_TILE,), I32).at[dest].set(token)
    return dest.astype(I32), src_rows, tile_expert, n_used.reshape(1).astype(I32)


def _stream_mix(x, mod, conv_state, shift_state, wkv_state, lp, nb, tb, tt):
    b, t, d = x.shape
    g = b // WKV_BATCHES
    h = _norm1(x, mod, lp["norm1_g"], nb, tb)
    proj = _inproj(h, lp["w_in"])
    outs = _mixprep(proj.reshape(b, t, D_IN_PROJ), conv_state, shift_state, lp, nb, tb)
    operands = [z.reshape(b, t * R_TILES, LANES) for z in outs[:N_WKV_OPERANDS]]
    yc, gate, bonus, new_conv, new_shift = outs[N_WKV_OPERANDS:]
    y_tiled, s_chain = _wkv(operands, _state_to_chain(wkv_state, g), tt)
    new_wkv = _state_from_chain(s_chain, g)
    yr = _post(y_tiled.reshape(b * t * R_TILES, LANES), bonus, gate, lp["ln_x_w"], lp["ln_x_b"])
    x1, h2_tiled, eid, wts = _outproj(yc, yr, lp["w_out_bf16"], x, mod, lp["norm2_g"], lp["wr_hi"], lp["wr_lo"],
                                      lp["b_router"], nb, tb)
    return x1, h2_tiled, eid, wts, new_conv, new_shift.reshape(b, D_RWKV_PROJ), new_wkv


def kernel(x_prompt, x_sample, c_prompt, c_sample, state_conv, state_shift, state_wkv, w_ada, b_ada, norm1_g, w_in, conv_w, mix_mu, w0, w_decay_up, a0, w_aaa_up, w_gate_up, k_k, k_a, r_k, ln_x_w, ln_x_b, w_out, norm2_g, w_router_group, b_router_group, w_router_expert, b_router_expert, w_gate, w_up, w_down, final_g):
    depth = w_ada.shape[0]
    assert depth == 1
    bp, tp, d = x_prompt.shape
    bs, ts, _ = x_sample.shape
    l = 0

    row = lambda z: z.reshape(1, -1)
    zeros_lora = jnp.zeros((LORA_DECAY, D_RWKV), F32)
    n_route = N_GROUPS + N_EXPERTS
    w_router = jnp.pad(jnp.concatenate([w_router_group[l], w_router_expert[l]], axis=1),
                       ((0, 0), (0, LANES - n_route)))
    b_router = jnp.concatenate([b_router_group[l], b_router_expert[l]])
    wr_hi = w_router.astype(BF16)
    lp = dict(
        norm1_g=row(norm1_g[l]), w_in=w_in[l], conv_w=conv_w[l], mix_mu=row(mix_mu[l]), w0=row(w0[l]),
        wd_pad=jnp.concatenate([w_decay_up[l], zeros_lora], axis=0), a0=row(a0[l]),
        wa_pad=jnp.concatenate([zeros_lora, w_aaa_up[l]], axis=0), w_gate_up=w_gate_up[l],
        k_k=row(k_k[l]), k_a=row(k_a[l]), r_k=row(r_k[l]), ln_x_w=row(ln_x_w[l]), ln_x_b=row(ln_x_b[l]),
        w_out_bf16=_cast_bf16(w_out[l]), norm2_g=row(norm2_g[l]),
        wr_hi=wr_hi, wr_lo=(w_router - wr_hi.astype(F32)).astype(BF16),
        b_router=jnp.pad(b_router, (0, LANES - n_route)).reshape(1, LANES),
    )

    n_c = bp + bs
    c_all = jnp.concatenate([c_prompt, c_sample], axis=0)
    c_all = jnp.pad(c_all, ((0, (-n_c) % SUBLANES), (0, 0)))
    mod = _ada(c_all, w_ada[l], row(b_ada[l]))
    mod_p = mod[:bp].reshape(bp, 1, N_MOD * d)
    mod_s = mod[bp:n_c].reshape(bs, 1, N_MOD * d)

    zc = jnp.zeros((bp, 2, D_CONV), F32)
    zs = jnp.zeros((bp, 1, D_RWKV_PROJ), F32)
    zw = jnp.zeros((bp, N_HEADS, HEAD_DIM, HEAD_DIM), F32)
    n_p = bp * tp
    x1_p, h2_p, eid_p, wts_p, conv_p, shift_p, wkv_p = _stream_mix(
        x_prompt, mod_p, zc, zs, zw, lp, nb=1, tb=ROW_TILE, tt=64)
    x1_s, h2_s, eid_s, wts_s, conv_s, shift_s, wkv_s = _stream_mix(
        x_sample, mod_s, state_conv[l], state_shift[l].reshape(bs, 1, D_RWKV_PROJ), state_wkv[l], lp,
        nb=ROW_TILE // ts, tb=ts, tt=ts)

    h2 = jnp.concatenate([h2_p, h2_s], axis=0)
    eid = jnp.concatenate([eid_p[:, :2], eid_s[:, :2]], axis=0)
    dest, src_rows, tile_expert, n_used = _dispatch_plan(eid)
    ys = _experts(tile_expert, n_used, src_rows, h2, w_gate[l], w_up[l], w_down[l])
    y_prompt = _combine(dest[:2 * n_p], wts_p, x1_p, mod_p, row(final_g), ys, bp, tp, nb=1, tb=ROW_TILE)
    y_sample = _combine(dest[2 * n_p:], wts_s, x1_s, mod_s, row(final_g), ys, bs, ts, nb=ROW_TILE // ts, tb=ts)

    return (y_prompt, y_sample, conv_p[None], shift_p[None], wkv_p[None],
            conv_s[None], shift_s[None], wkv_s[None])
```

```python
import jax
import jax.numpy as jnp
from jax import lax
from jax.experimental import pallas as pl
from jax.experimental.pallas import tpu as pltpu

F32 = jnp.float32
BF16 = jnp.bfloat16
I32 = jnp.int32

D_MODEL = 2048
D_CONV = 1024
D_RWKV = 1024
HEAD_DIM = 64
N_HEADS = 16
LORA_DECAY = 64
LORA_AAA = 64
LORA_GATE = 128
D_RWKV_PROJ = 3 * D_RWKV + LORA_DECAY + LORA_AAA + LORA_GATE
D_IN_PROJ = 3 * D_CONV + D_RWKV_PROJ
N_GROUPS = 4
EXPERTS_PER_GROUP = 8
N_EXPERTS = 32
D_EXPERT = 512
N_MOD = 6
RMS_EPS = 1e-6
GN_EPS = HEAD_DIM * 1e-5

SUBLANES = 8
LANES = 128
VMEM_LIMIT_BYTES = 56 * 1024 * 1024

ROW_TILE = 256
EXPERT_TILE = 256
SEG = 256
D_TILES = D_MODEL // LANES
R_TILES = D_RWKV // LANES
GATHER_PITCH = D_TILES + SUBLANES
GATHER_DMA_PRIORITY = 0
WEIGHT_DMA_PRIORITY = 1

WKV_BATCHES = 4
V_LO = LANES // (WKV_BATCHES * R_TILES)
V_HI = HEAD_DIM // V_LO
N_WKV_OPERANDS = 6


def _params(semantics):
    return pltpu.CompilerParams(dimension_semantics=semantics, vmem_limit_bytes=VMEM_LIMIT_BYTES)


def _sigmoid(x):
    return 1.0 / (1.0 + jnp.exp(-x))


def _store_row_tiled(ref, x, pitch):
    rows = x.shape[0]
    for s in range(x.shape[1] // LANES):
        ref[pl.ds(s, rows, stride=pitch), :] = x[:, s * LANES:(s + 1) * LANES]


def _load_row_tiled(ref, rows, n, pitch):
    return jnp.concatenate([ref[pl.ds(s, rows, stride=pitch), :] for s in range(n)], axis=-1)


def _head_ones():
    r = lax.broadcasted_iota(I32, (SEG, SEG), 0) // HEAD_DIM
    c = lax.broadcasted_iota(I32, (SEG, SEG), 1) // HEAD_DIM
    return jnp.where(r == c, 1.0, 0.0).astype(BF16)


def _split_bf16(x):
    hi = x.astype(BF16)
    return hi, (x - hi.astype(F32)).astype(BF16)


def _head_sum(x, ones):
    outs = []
    for s in range(x.shape[-1] // SEG):
        hi, lo = _split_bf16(x[:, s * SEG:(s + 1) * SEG])
        outs.append(jnp.dot(hi, ones, preferred_element_type=F32) + jnp.dot(lo, ones, preferred_element_type=F32))
    return jnp.concatenate(outs, axis=-1)


def _ada_kernel(c_ref, w_ref, b_ref, o_ref):
    c = c_ref[...]
    s = c * _sigmoid(c)
    o_ref[...] = jnp.dot(s.astype(BF16), w_ref[...].astype(BF16), preferred_element_type=F32) + b_ref[...]


def _ada(c_all, w_ada, b_ada):
    mb = c_all.shape[0]
    n = w_ada.shape[1]
    tn = 1024
    return pl.pallas_call(
        _ada_kernel,
        grid=(n // tn,),
        in_specs=[pl.BlockSpec((mb, D_MODEL), lambda j: (0, 0)),
                  pl.BlockSpec((D_MODEL, tn), lambda j: (0, j)),
                  pl.BlockSpec((1, tn), lambda j: (0, j))],
        out_specs=pl.BlockSpec((mb, tn), lambda j: (0, j)),
        out_shape=jax.ShapeDtypeStruct((mb, n), F32),
        compiler_params=_params(("arbitrary",)),
        name="ada",
    )(c_all, w_ada, b_ada)


def _norm1_kernel(x_ref, g_ref, sh_ref, sc_ref, o_ref):
    x = x_ref[...]
    nb, tb, d = x.shape
    ms = jnp.mean(x * x, axis=-1, keepdims=True)
    xn = x * lax.rsqrt(ms + RMS_EPS) * g_ref[...]
    h = xn * (1.0 + sc_ref[...]) + sh_ref[...]
    o_ref[...] = h.reshape(nb * tb, d).astype(BF16)


def _norm1(x, mod, g, nb, tb):
    b, t, d = x.shape
    nt = t // tb
    return pl.pallas_call(
        _norm1_kernel,
        grid=(b // nb, nt),
        in_specs=[pl.BlockSpec((nb, tb, d), lambda i, j: (i, j, 0)),
                  pl.BlockSpec((1, d), lambda i, j: (0, 0)),
                  pl.BlockSpec((nb, 1, d), lambda i, j: (i, 0, 0)),
                  pl.BlockSpec((nb, 1, d), lambda i, j: (i, 0, 1))],
        out_specs=pl.BlockSpec((nb * tb, d), lambda i, j: (i * nt + j, 0)),
        out_shape=jax.ShapeDtypeStruct((b * t, d), BF16),
        compiler_params=_params(("arbitrary", "arbitrary")),
        name="norm1",
    )(x, g, mod, mod)


def _inproj_kernel(h_ref, w_ref, o_ref, wb_ref):
    @pl.when(pl.program_id(1) == 0)
    def _():
        wb_ref[...] = w_ref[...].astype(BF16)

    o_ref[...] = jnp.dot(h_ref[...], wb_ref[...], preferred_element_type=F32)


def _inproj(h, w_in):
    m, k = h.shape
    n = w_in.shape[1]
    tm = min(m, 1024)
    tn = 1280
    return pl.pallas_call(
        _inproj_kernel,
        grid=(n // tn, m // tm),
        in_specs=[pl.BlockSpec((tm, k), lambda j, i: (i, 0)),
                  pl.BlockSpec((k, tn), lambda j, i: (0, j))],
        out_specs=pl.BlockSpec((tm, tn), lambda j, i: (i, j)),
        out_shape=jax.ShapeDtypeStruct((m, n), F32),
        scratch_shapes=[pltpu.VMEM((k, tn), BF16)],
        compiler_params=_params(("arbitrary", "arbitrary")),
        name="inproj",
    )(h, w_in)


def _mixprep_kernel(proj_ref, cst_ref, sst_ref, convw_ref, mu_ref, w0_ref, wd_ref, a0_ref, wa_ref, wg_ref,
                    kk_ref, ka_ref, rk_ref,
                    a_o, w_o, b_o, k_o, r_o, v_o, yc_o, g_o, bonus_o, nc_o, ns_o):
    @pl.when(pl.program_id(1) == 0)
    def _():
        nc_o[...] = cst_ref[...]
        ns_o[...] = sst_ref[...]

    nb, tb, _ = proj_ref.shape
    rows = nb * tb
    p = proj_ref[...].reshape(rows, D_IN_PROJ)
    tidx = lax.broadcasted_iota(I32, (rows, 1), 0) & (tb - 1)

    def per_row(c):
        return jnp.broadcast_to(c, (nb, tb, c.shape[-1])).reshape(rows, c.shape[-1])

    gate_b = p[:, 0:D_CONV]
    gate_c = p[:, D_CONV:2 * D_CONV]
    h_conv = p[:, 2 * D_CONV:3 * D_CONV]
    pr = p[:, 3 * D_CONV:]

    u = gate_c * h_conv
    carry = nc_o[...]
    c2 = per_row(carry[:, 0:1, :])
    c1 = per_row(carry[:, 1:2, :])
    u1 = jnp.where(tidx == 0, c1, pltpu.roll(u, 1, 0))
    u2 = jnp.where(tidx == 0, c2, jnp.where(tidx == 1, c1, pltpu.roll(u, 2, 0)))
    cw = convw_ref[...]
    conv_y = u * cw[2:3, :] + u2 * cw[0:1, :] + u1 * cw[1:2, :]
    yc_o[...] = (gate_b * conv_y).astype(BF16)
    nc_o[...] = u.reshape(nb, tb, D_CONV)[:, tb - 2:tb, :]

    p_prev = jnp.where(tidx == 0, per_row(ns_o[...]), pltpu.roll(pr, 1, 0))
    ns_o[...] = pr.reshape(nb, tb, D_RWKV_PROJ)[:, tb - 1:tb, :]
    pm = pr + (p_prev - pr) * mu_ref[...]
    r = pm[:, 0:D_RWKV]
    k = pm[:, D_RWKV:2 * D_RWKV]
    v = pm[:, 2 * D_RWKV:3 * D_RWKV]
    x_lora = pm[:, 3 * D_RWKV:3 * D_RWKV + LORA_DECAY + LORA_AAA]
    xg = pm[:, 3 * D_RWKV + LORA_DECAY + LORA_AAA:]

    d_pre = w0_ref[...] + jnp.dot(jnp.tanh(x_lora).astype(BF16), wd_ref[...].astype(BF16),
                                  preferred_element_type=F32)
    a_pre = a0_ref[...] + jnp.dot(x_lora.astype(BF16), wa_ref[...].astype(BF16), preferred_element_type=F32)
    g = jnp.dot(_sigmoid(xg).astype(BF16), wg_ref[...].astype(BF16), preferred_element_type=F32)
    z = -d_pre
    softplus = jnp.maximum(z, 0.0) + jnp.log(1.0 + jnp.exp(-jnp.abs(z)))
    w_log = -softplus - 0.5
    decay = jnp.exp(-jnp.exp(w_log))
    a_h = _sigmoid(a_pre)

    ones = _head_ones()
    kk = k * kk_ref[...]
    kk = kk * lax.rsqrt(jnp.maximum(_head_sum(kk * kk, ones), 1e-24))
    k_h = k * (1.0 + (a_h - 1.0) * ka_ref[...])
    bonus = _head_sum(r * k_h * rk_ref[...], ones) * v

    _store_row_tiled(a_o, -kk, R_TILES)
    _store_row_tiled(w_o, decay, R_TILES)
    _store_row_tiled(b_o, kk * a_h, R_TILES)
    _store_row_tiled(k_o, k_h, R_TILES)
    _store_row_tiled(r_o, r, R_TILES)
    _store_row_tiled(v_o, v, R_TILES)
    g_o[...] = g
    bonus_o[...] = bonus


def _mixprep(proj3, conv_state, shift_state, lp, nb, tb):
    b, t, _ = proj3.shape
    nt = t // tb
    rows = nb * tb
    m = b * t

    def row_spec(width):
        return pl.BlockSpec((rows, width), lambda i, j: (i * nt + j, 0))

    def full2(a):
        return pl.BlockSpec(a.shape, lambda i, j: (0, 0))

    consts = (lp["conv_w"], lp["mix_mu"], lp["w0"], lp["wd_pad"], lp["a0"], lp["wa_pad"], lp["w_gate_up"],
              lp["k_k"], lp["k_a"], lp["r_k"])
    tiled_spec = pl.BlockSpec((rows * R_TILES, LANES), lambda i, j: (i * nt + j, 0))
    tiled_out = jax.ShapeDtypeStruct((m * R_TILES, LANES), F32)
    f32_out = jax.ShapeDtypeStruct((m, D_RWKV), F32)
    return pl.pallas_call(
        _mixprep_kernel,
        grid=(b // nb, nt),
        in_specs=[pl.BlockSpec((nb, tb, D_IN_PROJ), lambda i, j: (i, j, 0)),
                  pl.BlockSpec((nb, 2, D_CONV), lambda i, j: (i, 0, 0)),
                  pl.BlockSpec((nb, 1, D_RWKV_PROJ), lambda i, j: (i, 0, 0))] + [full2(c) for c in consts],
        out_specs=[tiled_spec] * N_WKV_OPERANDS + [row_spec(D_CONV), row_spec(D_RWKV), row_spec(D_RWKV),
                                                   pl.BlockSpec((nb, 2, D_CONV), lambda i, j: (i, 0, 0)),
                                                   pl.BlockSpec((nb, 1, D_RWKV_PROJ), lambda i, j: (i, 0, 0))],
        out_shape=[tiled_out] * N_WKV_OPERANDS + [jax.ShapeDtypeStruct((m, D_CONV), BF16), f32_out, f32_out,
                                                  jax.ShapeDtypeStruct((b, 2, D_CONV), F32),
                                                  jax.ShapeDtypeStruct((b, 1, D_RWKV_PROJ), F32)],
        compiler_params=_params(("arbitrary", "arbitrary")),
        name="mixprep",
    )(proj3, conv_state, shift_state, *consts)


def _wkv_kernel(a_ref, w_ref, b_ref, k_ref, r_ref, v_ref, s0_ref, y_ref, s_ref, tiles, yq):
    @pl.when(pl.program_id(1) == 0)
    def _():
        s_ref[...] = s0_ref[...]

    tt = a_ref.shape[1] // R_TILES
    srcs = (a_ref, w_ref, b_ref, k_ref, r_ref, v_ref)
    n_chain = WKV_BATCHES * R_TILES
    sub8 = lax.broadcasted_iota(I32, (SUBLANES, LANES), 0)
    grp8 = lax.broadcasted_iota(I32, (SUBLANES, LANES), 1) // n_chain
    diag8 = (sub8 % V_LO) == grp8
    sub4 = lax.broadcasted_iota(I32, (V_LO, LANES), 0)
    grp4 = lax.broadcasted_iota(I32, (V_LO, LANES), 1) // n_chain
    diag4 = sub4 == grp4

    def produce(t, slot):
        row0 = pl.multiple_of(t * R_TILES, R_TILES)
        for ai, ref in enumerate(srcs):
            m = jnp.concatenate([ref[bb, pl.ds(row0, R_TILES), :] for bb in range(WKV_BATCHES)], axis=0)
            xt = jnp.concatenate([m] * V_LO, axis=0).T
            if ai < N_WKV_OPERANDS - 1:
                tiles[slot, ai] = xt
            else:
                for q in range(LANES // SUBLANES):
                    z = jnp.where(diag8, xt[q * SUBLANES:(q + 1) * SUBLANES], 0.0)
                    z = z + pltpu.roll(z, 1, 0)
                    z = z + pltpu.roll(z, 2, 0)
                    tiles[slot, ai, q * SUBLANES:(q + 1) * SUBLANES, :] = z

    def consume(slot):
        for par in range(2):
            lo = par * HEAD_DIM
            a = tiles[slot, 0, lo:lo + HEAD_DIM, :]
            w = tiles[slot, 1, lo:lo + HEAD_DIM, :]
            b = tiles[slot, 2, lo:lo + HEAD_DIM, :]
            k = tiles[slot, 3, lo:lo + HEAD_DIM, :]
            r = tiles[slot, 4, lo:lo + HEAD_DIM, :]
            for vh in range(V_HI):
                s = s_ref[0, par, vh]
                vrow = tiles[slot, 5, lo + vh * V_LO + V_LO - 1:lo + (vh + 1) * V_LO, :]
                sa = jnp.sum(s * a, axis=0, keepdims=True)
                s = s * w + sa * b + vrow * k
                s_ref[0, par, vh] = s
                yrow = jnp.sum(s * r, axis=0, keepdims=True)
                yq[slot, lo + vh * V_LO:lo + (vh + 1) * V_LO, :] = jnp.where(
                    diag4, jnp.broadcast_to(yrow, (V_LO, LANES)), 0.0)

    def finalize(t, slot):
        qt = yq[slot].T
        res = qt[0:n_chain]
        for j in range(1, V_LO):
            res = res + qt[j * n_chain:(j + 1) * n_chain]
        row0 = pl.multiple_of(t * R_TILES, R_TILES)
        for bb in range(WKV_BATCHES):
            y_ref[bb, pl.ds(row0, R_TILES), :] = res[bb * R_TILES:(bb + 1) * R_TILES]

    produce(0, 0)
    produce(1, 1)
    consume(0)

    def body(i, c):
        t1 = 2 * i + 1
        produce(t1 + 1, 0)
        consume(1)
        finalize(t1 - 1, 0)
        produce(jnp.minimum(t1 + 2, tt - 1), 1)
        consume(0)
        finalize(t1, 1)
        return c

    lax.fori_loop(0, tt // 2 - 1, body, 0)
    consume(1)
    finalize(tt - 2, 0)
    finalize(tt - 1, 1)


def _wkv(operands, s0, tt):
    bsz, rows, _ = operands[0].shape
    t = rows // R_TILES
    g = bsz // WKV_BATCHES
    assert tt % 2 == 0 and tt >= 4 and t % tt == 0
    blk = pl.BlockSpec((WKV_BATCHES, tt * R_TILES, LANES), lambda i, j: (i, j, 0))
    state = pl.BlockSpec((1, 2, V_HI, HEAD_DIM, LANES), lambda i, j: (i, 0, 0, 0, 0))
    return pl.pallas_call(
        _wkv_kernel,
        grid=(g, t // tt),
        in_specs=[blk] * N_WKV_OPERANDS + [state],
        out_specs=[blk, state],
        out_shape=[jax.ShapeDtypeStruct((bsz, rows, LANES), F32),
                   jax.ShapeDtypeStruct((g, 2, V_HI, HEAD_DIM, LANES), F32)],
        scratch_shapes=[pltpu.VMEM((2, N_WKV_OPERANDS, LANES, LANES), F32), pltpu.VMEM((2, LANES, LANES), F32)],
        compiler_params=_params(("arbitrary", "arbitrary")),
        name="wkv",
    )(*operands, s0)


def _state_to_chain(s, g):
    s = s.reshape(g, WKV_BATCHES, R_TILES, 2, V_HI, V_LO, HEAD_DIM).transpose(0, 3, 4, 6, 5, 1, 2)
    return s.reshape(g, 2, V_HI, HEAD_DIM, LANES)


def _state_from_chain(s, g):
    s = s.reshape(g, 2, V_HI, HEAD_DIM, V_LO, WKV_BATCHES, R_TILES).transpose(0, 5, 6, 1, 2, 4, 3)
    return s.reshape(g * WKV_BATCHES, N_HEADS, HEAD_DIM, HEAD_DIM)


def _post_kernel(y_ref, bonus_ref, g_ref, lnw_ref, lnb_ref, o_ref):
    ones = _head_ones()
    y = _load_row_tiled(y_ref, o_ref.shape[0], R_TILES, R_TILES)
    mu = _head_sum(y, ones) * (1.0 / HEAD_DIM)
    d = y - mu
    var = _head_sum(d * d, ones) * (1.0 / HEAD_DIM)
    yn = d * lax.rsqrt(var + GN_EPS)
    yn = yn * lnw_ref[...] + lnb_ref[...]
    o_ref[...] = ((yn + bonus_ref[...]) * g_ref[...]).astype(BF16)


def _post(y_tiled, bonus, g, ln_w, ln_b):
    m = bonus.shape[0]
    tm = 512
    row = pl.BlockSpec((tm, D_RWKV), lambda i: (i, 0))
    vec = pl.BlockSpec((1, D_RWKV), lambda i: (0, 0))
    return pl.pallas_call(
        _post_kernel,
        grid=(m // tm,),
        in_specs=[pl.BlockSpec((tm * R_TILES, LANES), lambda i: (i, 0)), row, row, vec, vec],
        out_specs=row,
        out_shape=jax.ShapeDtypeStruct((m, D_RWKV), BF16),
        compiler_params=_params(("arbitrary",)),
        name="post",
    )(y_tiled, bonus, g, ln_w, ln_b)


def _cast_kernel(x_ref, o_ref):
    o_ref[...] = x_ref[...].astype(o_ref.dtype)


def _cast_bf16(w):
    r, c = w.shape
    tr = 512
    return pl.pallas_call(
        _cast_kernel,
        grid=(r // tr,),
        in_specs=[pl.BlockSpec((tr, c), lambda i: (i, 0))],
        out_specs=pl.BlockSpec((tr, c), lambda i: (i, 0)),
        out_shape=jax.ShapeDtypeStruct((r, c), BF16),
        compiler_params=_params(("arbitrary",)),
        name="cast_bf16",
    )(w)


def _route(lg):
    lane = lax.broadcasted_iota(I32, lg.shape, 1)
    neg = -jnp.inf
    big = jnp.int32(1 << 20)

    is_g = lane < N_GROUPS
    gl = jnp.where(is_g, lg, neg)
    gmax = jnp.max(gl, axis=1, keepdims=True)
    gidx = jnp.min(jnp.where(gl == gmax, lane, big), axis=1, keepdims=True)
    gsum = jnp.sum(jnp.where(is_g, jnp.exp(gl - gmax), 0.0), axis=1, keepdims=True)
    gprob = 1.0 / gsum

    e_lane = lane - N_GROUPS
    in_grp = (e_lane >= 0) & (e_lane < N_EXPERTS) & ((e_lane >> 3) == gidx)
    el = jnp.where(in_grp, lg, neg)
    emax = jnp.max(el, axis=1, keepdims=True)
    ex = jnp.where(in_grp, jnp.exp(el - emax), 0.0)
    prob = ex / jnp.sum(ex, axis=1, keepdims=True)
    prob = jnp.where(in_grp, prob, -1.0)
    p1 = jnp.max(prob, axis=1, keepdims=True)
    i1 = jnp.min(jnp.where(prob == p1, lane, big), axis=1, keepdims=True)
    prob2 = jnp.where(lane == i1, -1.0, prob)
    p2 = jnp.max(prob2, axis=1, keepdims=True)
    i2 = jnp.min(jnp.where(prob2 == p2, lane, big), axis=1, keepdims=True)
    psum = p1 + p2
    w1 = p1 / psum * gprob
    w2 = p2 / psum * gprob
    eid = jnp.where(lane == 0, i1 - N_GROUPS, jnp.where(lane == 1, i2 - N_GROUPS, 0))
    wts = jnp.where(lane == 0, w1, jnp.where(lane == 1, w2, 0.0))
    return eid, wts


def _outproj_kernel(yc_ref, yr_ref, wt_ref, wb_ref, x_ref, ga1_ref, sh2_ref, sc2_ref, g2_ref, wrh_ref, wrl_ref,
                    br_ref, x1_o, h2_o, eid_o, wts_o):
    nb, tb, d = x_ref.shape
    rows = nb * tb
    mix = (jnp.dot(yc_ref[...], wt_ref[...], preferred_element_type=F32)
           + jnp.dot(yr_ref[...], wb_ref[...], preferred_element_type=F32))
    x1 = x_ref[...] + ga1_ref[...] * mix.reshape(nb, tb, d)
    ms = jnp.mean(x1 * x1, axis=-1, keepdims=True)
    h2 = x1 * lax.rsqrt(ms + RMS_EPS) * g2_ref[...]
    h2 = (h2 * (1.0 + sc2_ref[...]) + sh2_ref[...]).reshape(rows, d)
    x1_o[...] = x1.reshape(rows, d)
    _store_row_tiled(h2_o, h2, D_TILES)
    h_hi, h_lo = _split_bf16(h2)
    w_hi = wrh_ref[...]
    logits = (jnp.dot(h_hi, w_hi, preferred_element_type=F32)
              + jnp.dot(h_lo, w_hi, preferred_element_type=F32)
              + jnp.dot(h_hi, wrl_ref[...], preferred_element_type=F32)) + br_ref[...]
    eid_o[...], wts_o[...] = _route(logits)


def _outproj(yc, yr, w_out_bf16, x, mod, g2, wr_hi, wr_lo, b_router, nb, tb):
    b, t, d = x.shape
    nt = t // tb
    rows = nb * tb
    m = b * t
    half = D_CONV
    row = lambda width: pl.BlockSpec((rows, width), lambda i, j: (i * nt + j, 0))
    modspec = lambda col: pl.BlockSpec((nb, 1, d), lambda i, j: (i, 0, col))
    full = lambda a: pl.BlockSpec(a.shape, lambda i, j: (0, 0))
    return pl.pallas_call(
        _outproj_kernel,
        grid=(b // nb, nt),
        in_specs=[row(half), row(half),
                  pl.BlockSpec((half, d), lambda i, j: (0, 0)),
                  pl.BlockSpec((half, d), lambda i, j: (1, 0)),
                  pl.BlockSpec((nb, tb, d), lambda i, j: (i, j, 0)),
                  modspec(2), modspec(3), modspec(4),
                  full(g2), full(wr_hi), full(wr_lo), full(b_router)],
        out_specs=[row(d), pl.BlockSpec((rows * D_TILES, LANES), lambda i, j: (i * nt + j, 0)),
                   row(LANES), row(LANES)],
        out_shape=[jax.ShapeDtypeStruct((m, d), F32), jax.ShapeDtypeStruct((m * D_TILES, LANES), F32),
                   jax.ShapeDtypeStruct((m, LANES), I32), jax.ShapeDtypeStruct((m, LANES), F32)],
        compiler_params=_params(("arbitrary", "arbitrary")),
        name="outproj",
    )(yc, yr, w_out_bf16, w_out_bf16, x, mod, mod, mod, g2, wr_hi, wr_lo, b_router)


def _row_copy(src_hbm, src_row, dst_ref, dst_row, sem):
    src = pl.multiple_of(src_row * D_TILES, D_TILES)
    dst = pl.multiple_of(dst_row * GATHER_PITCH, SUBLANES)
    return pltpu.make_async_copy(src_hbm.at[pl.ds(src, D_TILES)], dst_ref.at[pl.ds(dst, D_TILES)], sem)


def _start_row_gather(index_of, src_hbm, dst_ref, sem, n):
    def body(r, c):
        _row_copy(src_hbm, index_of(r), dst_ref, r, sem).start(priority=GATHER_DMA_PRIORITY)
        return c
    lax.fori_loop(0, n, body, 0, unroll=8)


def _wait_row_gather(src_hbm, dst_ref, sem, n):
    def body(r, c):
        _row_copy(src_hbm, 0, dst_ref, r, sem).wait()
        return c
    lax.fori_loop(0, n, body, 0, unroll=8)


def _experts_kernel(te_ref, nu_ref, nxt_ref, wsl_ref, idx_ref, idxn_ref, h_hbm, wg_hbm, wu_hbm, wd_hbm, o_ref,
                    xbuf, wgf, wuf, wdf, wgb, wub, wdb, sems, wsems):
    i = pl.program_id(0)
    n_used = nu_ref[0]
    slot = i & 1

    def weight_copies(e, ws):
        return (pltpu.make_async_copy(wg_hbm.at[e], wgf.at[ws], wsems.at[ws]),
                pltpu.make_async_copy(wu_hbm.at[e], wuf.at[ws], wsems.at[ws]),
                pltpu.make_async_copy(wd_hbm.at[e], wdf.at[ws], wsems.at[ws]))

    @pl.when(i == 0)
    def _():
        for c in weight_copies(te_ref[0], wsl_ref[0]):
            c.start(priority=WEIGHT_DMA_PRIORITY)
        _start_row_gather(lambda r: idx_ref[0, r], h_hbm, xbuf.at[0], sems.at[0], EXPERT_TILE)

    @pl.when(i + 1 < n_used)
    def _():
        _start_row_gather(lambda r: idxn_ref[0, r], h_hbm, xbuf.at[1 - slot], sems.at[1 - slot], EXPERT_TILE)

    @pl.when(i < n_used)
    def _():
        new_expert = jnp.logical_or(i == 0, te_ref[i] != te_ref[jnp.maximum(i - 1, 0)])

        @pl.when(new_expert)
        def _():
            ws = wsl_ref[i]
            for c in weight_copies(te_ref[i], ws):
                c.wait()
            wgb[...] = wgf[ws].astype(BF16)
            wub[...] = wuf[ws].astype(BF16)
            wdb[...] = wdf[ws].astype(BF16)

            @pl.when(nxt_ref[i] >= 0)
            def _():
                for c in weight_copies(nxt_ref[i], 1 - ws):
                    c.start(priority=WEIGHT_DMA_PRIORITY)

        _wait_row_gather(h_hbm, xbuf.at[slot], sems.at[slot], EXPERT_TILE)
        x = _load_row_tiled(xbuf.at[slot], EXPERT_TILE, D_TILES, GATHER_PITCH).astype(BF16)
        gate = jnp.dot(x, wgb[...], preferred_element_type=F32)
        up = jnp.dot(x, wub[...], preferred_element_type=F32)
        act = (gate * _sigmoid(gate) * up).astype(BF16)
        _store_row_tiled(o_ref, jnp.dot(act, wdb[...], preferred_element_type=F32), D_TILES)

    @pl.when(i >= n_used)
    def _():
        o_ref[...] = jnp.zeros_like(o_ref)


def _experts(plan, h2_tiled, w_gate, w_up, w_down):
    tile_expert, n_used, next_expert, weight_slot, src_rows = plan
    n_tiles = tile_expert.shape[0]
    d = w_gate.shape[1]
    idx3 = src_rows.reshape(n_tiles, 1, EXPERT_TILE)
    smem_idx = lambda f: pl.BlockSpec((None, 1, EXPERT_TILE), f, memory_space=pltpu.SMEM)
    hbm = pl.BlockSpec(memory_space=pl.ANY)
    grid_spec = pltpu.PrefetchScalarGridSpec(
        num_scalar_prefetch=4,
        grid=(n_tiles,),
        in_specs=[smem_idx(lambda i, *_: (i, 0, 0)),
                  smem_idx(lambda i, *_: (jnp.minimum(i + 1, n_tiles - 1), 0, 0)),
                  hbm, hbm, hbm, hbm],
        out_specs=pl.BlockSpec((EXPERT_TILE * D_TILES, LANES), lambda i, *_: (i, 0)),
        scratch_shapes=[pltpu.VMEM((2, EXPERT_TILE * GATHER_PITCH, LANES), F32),
                        pltpu.VMEM((2, d, D_EXPERT), F32), pltpu.VMEM((2, d, D_EXPERT), F32),
                        pltpu.VMEM((2, D_EXPERT, d), F32),
                        pltpu.VMEM((d, D_EXPERT), BF16), pltpu.VMEM((d, D_EXPERT), BF16),
                        pltpu.VMEM((D_EXPERT, d), BF16),
                        pltpu.SemaphoreType.DMA((2,)), pltpu.SemaphoreType.DMA((2,))],
    )
    return pl.pallas_call(
        _experts_kernel,
        grid_spec=grid_spec,
        out_shape=jax.ShapeDtypeStruct((n_tiles * EXPERT_TILE * D_TILES, LANES), F32),
        compiler_params=_params(("arbitrary",)),
        name="experts",
    )(tile_expert, n_used, next_expert, weight_slot, idx3, idx3, h2_tiled, w_gate, w_up, w_down)


def _combine_kernel(idx_ref, idxn_ref, wt_ref, x1_ref, ga2_ref, gf_ref, ys_hbm, o_ref, ybuf, sems):
    i = pl.program_id(0)
    n_steps = pl.num_programs(0)
    slot = i & 1
    nb, tb, d = o_ref.shape
    rows = nb * tb

    def fetch(idx, s):
        for j in range(2):
            _start_row_gather(lambda r: idx[0, 2 * r + j], ys_hbm, ybuf.at[s, j], sems.at[s], rows)

    @pl.when(i == 0)
    def _():
        fetch(idx_ref, 0)

    @pl.when(i + 1 < n_steps)
    def _():
        fetch(idxn_ref, 1 - slot)

    for j in range(2):
        _wait_row_gather(ys_hbm, ybuf.at[slot, j], sems.at[slot], rows)

    wt = wt_ref[...]
    moe = (wt[:, 0:1] * _load_row_tiled(ybuf.at[slot, 0], rows, D_TILES, GATHER_PITCH)
           + wt[:, 1:2] * _load_row_tiled(ybuf.at[slot, 1], rows, D_TILES, GATHER_PITCH))
    x2 = x1_ref[...].reshape(nb, tb, d) + ga2_ref[...] * moe.reshape(nb, tb, d)
    ms = jnp.mean(x2 * x2, axis=-1, keepdims=True)
    o_ref[...] = x2 * lax.rsqrt(ms + RMS_EPS) * gf_ref[...]


def _combine(dest, wts, x1, mod, final_g, ys, b, t, nb, tb):
    d = x1.shape[1]
    nt = t // tb
    rows = nb * tb
    n_steps = (b // nb) * nt
    idx3 = dest.reshape(n_steps, 1, 2 * rows)
    smem_idx = lambda f: pl.BlockSpec((None, 1, 2 * rows), f, memory_space=pltpu.SMEM)
    return pl.pallas_call(
        _combine_kernel,
        grid=(n_steps,),
        in_specs=[smem_idx(lambda i: (i, 0, 0)),
                  smem_idx(lambda i: (jnp.minimum(i + 1, n_steps - 1), 0, 0)),
                  pl.BlockSpec((rows, LANES), lambda i: (i, 0)),
                  pl.BlockSpec((rows, d), lambda i: (i, 0)),
                  pl.BlockSpec((nb, 1, d), lambda i: (i // nt, 0, 5)),
                  pl.BlockSpec((1, d), lambda i: (0, 0)),
                  pl.BlockSpec(memory_space=pl.ANY)],
        out_specs=pl.BlockSpec((nb, tb, d), lambda i: (i // nt, i % nt, 0)),
        out_shape=jax.ShapeDtypeStruct((b, t, d), F32),
        scratch_shapes=[pltpu.VMEM((2, 2, rows * GATHER_PITCH, LANES), F32), pltpu.SemaphoreType.DMA((2,))],
        compiler_params=_params(("arbitrary",)),
        name="combine",
    )(idx3, idx3, wts, x1, mod, final_g, ys)


def _dispatch_plan(eid):
    n_pairs = eid.shape[0] * 2
    n_tiles = n_pairs // EXPERT_TILE + N_EXPERTS
    flat = eid.reshape(-1)
    onehot = (flat[:, None] == jnp.arange(N_EXPERTS, dtype=I32)[None, :]).astype(I32)
    csum = jnp.cumsum(onehot, axis=0)
    rank = jnp.sum((csum - onehot) * onehot, axis=1)
    counts = csum[-1]
    tiles_per = (counts + EXPERT_TILE - 1) // EXPERT_TILE
    tile_end = jnp.cumsum(tiles_per)
    row_start = (tile_end - tiles_per) * EXPERT_TILE
    dest = jnp.sum(onehot * row_start[None, :], axis=1) + rank
    n_used = tile_end[-1]
    tile_ids = jnp.arange(n_tiles, dtype=I32)
    tile_expert = jnp.sum((tile_ids[:, None] >= tile_end[None, :]).astype(I32), axis=1)
    last_expert = jnp.max(jnp.where(counts > 0, jnp.arange(N_EXPERTS, dtype=I32), 0))
    tile_expert = jnp.where(tile_ids < n_used, tile_expert, last_expert).astype(I32)
    token = jnp.arange(n_pairs, dtype=I32) // 2
    src_rows = jnp.zeros((n_tiles * EXPERT_TILE,), I32).at[dest].set(token)
    experts = jnp.arange(N_EXPERTS, dtype=I32)
    used = counts > 0
    later_used = used[None, :] & (experts[None, :] > experts[:, None])
    next_used = jnp.min(jnp.where(later_used, experts[None, :], N_EXPERTS), axis=1)
    next_used = jnp.where(next_used < N_EXPERTS, next_used, -1).astype(I32)
    slot_of = ((jnp.cumsum(used.astype(I32)) - 1) & 1).astype(I32)
    plan = (tile_expert, n_used.reshape(1).astype(I32), next_used[tile_expert], slot_of[tile_expert], src_rows)
    return dest.astype(I32), plan


def _stream_mix(x, mod, conv_state, shift_state, wkv_state, lp, nb, tb, tt):
    b, t, d = x.shape
    g = b // WKV_BATCHES
    h = _norm1(x, mod, lp["norm1_g"], nb, tb)
    proj = _inproj(h, lp["w_in"])
    outs = _mixprep(proj.reshape(b, t, D_IN_PROJ), conv_state, shift_state, lp, nb, tb)
    operands = [z.reshape(b, t * R_TILES, LANES) for z in outs[:N_WKV_OPERANDS]]
    yc, gate, bonus, new_conv, new_shift = outs[N_WKV_OPERANDS:]
    y_tiled, s_chain = _wkv(operands, _state_to_chain(wkv_state, g), tt)
    new_wkv = _state_from_chain(s_chain, g)
    yr = _post(y_tiled.reshape(b * t * R_TILES, LANES), bonus, gate, lp["ln_x_w"], lp["ln_x_b"])
    x1, h2_tiled, eid, wts = _outproj(yc, yr, lp["w_out_bf16"], x, mod, lp["norm2_g"], lp["wr_hi"], lp["wr_lo"],
                                      lp["b_router"], nb, tb)
    return x1, h2_tiled, eid, wts, new_conv, new_shift.reshape(b, D_RWKV_PROJ), new_wkv


def kernel(x_prompt, x_sample, c_prompt, c_sample, state_conv, state_shift, state_wkv, w_ada, b_ada, norm1_g, w_in, conv_w, mix_mu, w0, w_decay_up, a0, w_aaa_up, w_gate_up, k_k, k_a, r_k, ln_x_w, ln_x_b, w_out, norm2_g, w_router_group, b_router_group, w_router_expert, b_router_expert, w_gate, w_up, w_down, final_g):
    depth = w_ada.shape[0]
    assert depth == 1
    bp, tp, d = x_prompt.shape
    bs, ts, _ = x_sample.shape
    l = 0

    row = lambda z: z.reshape(1, -1)
    zeros_lora = jnp.zeros((LORA_DECAY, D_RWKV), F32)
    n_route = N_GROUPS + N_EXPERTS
    w_router = jnp.pad(jnp.concatenate([w_router_group[l], w_router_expert[l]], axis=1),
                       ((0, 0), (0, LANES - n_route)))
    b_router = jnp.concatenate([b_router_group[l], b_router_expert[l]])
    wr_hi = w_router.astype(BF16)
    lp = dict(
        norm1_g=row(norm1_g[l]), w_in=w_in[l], conv_w=conv_w[l], mix_mu=row(mix_mu[l]), w0=row(w0[l]),
        wd_pad=jnp.concatenate([w_decay_up[l], zeros_lora], axis=0), a0=row(a0[l]),
        wa_pad=jnp.concatenate([zeros_lora, w_aaa_up[l]], axis=0), w_gate_up=w_gate_up[l],
        k_k=row(k_k[l]), k_a=row(k_a[l]), r_k=row(r_k[l]), ln_x_w=row(ln_x_w[l]), ln_x_b=row(ln_x_b[l]),
        w_out_bf16=_cast_bf16(w_out[l]), norm2_g=row(norm2_g[l]),
        wr_hi=wr_hi, wr_lo=(w_router - wr_hi.astype(F32)).astype(BF16),
        b_router=jnp.pad(b_router, (0, LANES - n_route)).reshape(1, LANES),
    )

    n_c = bp + bs
    c_all = jnp.concatenate([c_prompt, c_sample], axis=0)
    c_all = jnp.pad(c_all, ((0, (-n_c) % SUBLANES), (0, 0)))
    mod = _ada(c_all, w_ada[l], row(b_ada[l]))
    mod_p = mod[:bp].reshape(bp, 1, N_MOD * d)
    mod_s = mod[bp:n_c].reshape(bs, 1, N_MOD * d)

    zc = jnp.zeros((bp, 2, D_CONV), F32)
    zs = jnp.zeros((bp, 1, D_RWKV_PROJ), F32)
    zw = jnp.zeros((bp, N_HEADS, HEAD_DIM, HEAD_DIM), F32)
    n_p = bp * tp
    x1_p, h2_p, eid_p, wts_p, conv_p, shift_p, wkv_p = _stream_mix(
        x_prompt, mod_p, zc, zs, zw, lp, nb=1, tb=ROW_TILE, tt=64)
    x1_s, h2_s, eid_s, wts_s, conv_s, shift_s, wkv_s = _stream_mix(
        x_sample, mod_s, state_conv[l], state_shift[l].reshape(bs, 1, D_RWKV_PROJ), state_wkv[l], lp,
        nb=ROW_TILE // ts, tb=ts, tt=ts)

    h2 = jnp.concatenate([h2_p, h2_s], axis=0)
    eid = jnp.concatenate([eid_p[:, :2], eid_s[:, :2]], axis=0)
    dest, plan = _dispatch_plan(eid)
    ys = _experts(plan, h2, w_gate[l], w_up[l], w_down[l])
    y_prompt = _combine(dest[:2 * n_p], wts_p, x1_p, mod_p, row(final_g), ys, bp, tp, nb=1, tb=ROW_TILE)
    y_sample = _combine(dest[2 * n_p:], wts_s, x1_s, mod_s, row(final_g), ys, bs, ts, nb=ROW_TILE // ts, tb=ts)

    return (y_prompt, y_sample, conv_p[None], shift_p[None], wkv_p[None],
            conv_s[None], shift_s[None], wkv_s[None])
```

```python
import jax
import jax.numpy as jnp
from jax import lax
from jax.experimental import pallas as pl
from jax.experimental.pallas import tpu as pltpu

F32 = jnp.float32
BF16 = jnp.bfloat16
I32 = jnp.int32

D_MODEL = 2048
D_CONV = 1024
D_RWKV = 1024
HEAD_DIM = 64
N_HEADS = 16
LORA_DECAY = 64
LORA_AAA = 64
LORA_GATE = 128
D_RWKV_PROJ = 3 * D_RWKV + LORA_DECAY + LORA_AAA + LORA_GATE
D_IN_PROJ = 3 * D_CONV + D_RWKV_PROJ
N_GROUPS = 4
EXPERTS_PER_GROUP = 8
N_EXPERTS = 32
D_EXPERT = 512
N_MOD = 6
RMS_EPS = 1e-6
GN_EPS = HEAD_DIM * 1e-5

SUBLANES = 8
LANES = 128
VMEM_LIMIT_BYTES = 56 * 1024 * 1024

ROW_TILE = 256
EXPERT_TILE = 256
SEG = 256
D_TILES = D_MODEL // LANES
R_TILES = D_RWKV // LANES
GATHER_PITCH = D_TILES + SUBLANES
GATHER_DMA_PRIORITY = 0
WEIGHT_DMA_PRIORITY = 1

WKV_BATCHES = 4
V_LO = LANES // (WKV_BATCHES * R_TILES)
V_HI = HEAD_DIM // V_LO
N_WKV_OPERANDS = 6


def _params(semantics):
    return pltpu.CompilerParams(dimension_semantics=semantics, vmem_limit_bytes=VMEM_LIMIT_BYTES)


def _sigmoid(x):
    return 1.0 / (1.0 + jnp.exp(-x))


def _store_row_tiled(ref, x, pitch):
    rows = x.shape[0]
    for s in range(x.shape[1] // LANES):
        ref[pl.ds(s, rows, stride=pitch), :] = x[:, s * LANES:(s + 1) * LANES]


def _load_row_tiled(ref, rows, n, pitch):
    return jnp.concatenate([ref[pl.ds(s, rows, stride=pitch), :] for s in range(n)], axis=-1)


def _head_ones():
    r = lax.broadcasted_iota(I32, (SEG, SEG), 0) // HEAD_DIM
    c = lax.broadcasted_iota(I32, (SEG, SEG), 1) // HEAD_DIM
    return jnp.where(r == c, 1.0, 0.0).astype(BF16)


def _split_bf16(x):
    hi = x.astype(BF16)
    return hi, (x - hi.astype(F32)).astype(BF16)


def _head_sum(x, ones):
    outs = []
    for s in range(x.shape[-1] // SEG):
        hi, lo = _split_bf16(x[:, s * SEG:(s + 1) * SEG])
        outs.append(jnp.dot(hi, ones, preferred_element_type=F32) + jnp.dot(lo, ones, preferred_element_type=F32))
    return jnp.concatenate(outs, axis=-1)


def _ada_kernel(c_ref, w_ref, b_ref, o_ref):
    c = c_ref[...]
    s = c * _sigmoid(c)
    o_ref[...] = jnp.dot(s.astype(BF16), w_ref[...].astype(BF16), preferred_element_type=F32) + b_ref[...]


def _ada(c_all, w_ada, b_ada):
    mb = c_all.shape[0]
    n = w_ada.shape[1]
    tn = 1024
    return pl.pallas_call(
        _ada_kernel,
        grid=(n // tn,),
        in_specs=[pl.BlockSpec((mb, D_MODEL), lambda j: (0, 0)),
                  pl.BlockSpec((D_MODEL, tn), lambda j: (0, j)),
                  pl.BlockSpec((1, tn), lambda j: (0, j))],
        out_specs=pl.BlockSpec((mb, tn), lambda j: (0, j)),
        out_shape=jax.ShapeDtypeStruct((mb, n), F32),
        compiler_params=_params(("arbitrary",)),
        name="ada",
    )(c_all, w_ada, b_ada)


def _norm1_kernel(x_ref, g_ref, sh_ref, sc_ref, o_ref):
    x = x_ref[...]
    nb, tb, d = x.shape
    ms = jnp.mean(x * x, axis=-1, keepdims=True)
    xn = x * lax.rsqrt(ms + RMS_EPS) * g_ref[...]
    h = xn * (1.0 + sc_ref[...]) + sh_ref[...]
    o_ref[...] = h.reshape(nb * tb, d).astype(BF16)


def _norm1(x, mod, g, nb, tb):
    b, t, d = x.shape
    nt = t // tb
    return pl.pallas_call(
        _norm1_kernel,
        grid=(b // nb, nt),
        in_specs=[pl.BlockSpec((nb, tb, d), lambda i, j: (i, j, 0)),
                  pl.BlockSpec((1, d), lambda i, j: (0, 0)),
                  pl.BlockSpec((nb, 1, d), lambda i, j: (i, 0, 0)),
                  pl.BlockSpec((nb, 1, d), lambda i, j: (i, 0, 1))],
        out_specs=pl.BlockSpec((nb * tb, d), lambda i, j: (i * nt + j, 0)),
        out_shape=jax.ShapeDtypeStruct((b * t, d), BF16),
        compiler_params=_params(("arbitrary", "arbitrary")),
        name="norm1",
    )(x, g, mod, mod)


def _inproj_kernel(h_ref, w_ref, o_ref, wb_ref):
    @pl.when(pl.program_id(1) == 0)
    def _():
        wb_ref[...] = w_ref[...].astype(BF16)

    o_ref[...] = jnp.dot(h_ref[...], wb_ref[...], preferred_element_type=F32)


def _inproj(h, w_in):
    m, k = h.shape
    n = w_in.shape[1]
    tm = min(m, 1024)
    tn = 1280
    return pl.pallas_call(
        _inproj_kernel,
        grid=(n // tn, m // tm),
        in_specs=[pl.BlockSpec((tm, k), lambda j, i: (i, 0)),
                  pl.BlockSpec((k, tn), lambda j, i: (0, j))],
        out_specs=pl.BlockSpec((tm, tn), lambda j, i: (i, j)),
        out_shape=jax.ShapeDtypeStruct((m, n), F32),
        scratch_shapes=[pltpu.VMEM((k, tn), BF16)],
        compiler_params=_params(("arbitrary", "arbitrary")),
        name="inproj",
    )(h, w_in)


def _mixprep_kernel(proj_ref, cst_ref, sst_ref, convw_ref, mu_ref, w0_ref, wd_ref, a0_ref, wa_ref, wg_ref,
                    kk_ref, ka_ref, rk_ref,
                    a_o, w_o, b_o, k_o, r_o, v_o, yc_o, g_o, bonus_o, nc_o, ns_o):
    @pl.when(pl.program_id(1) == 0)
    def _():
        nc_o[...] = cst_ref[...]
        ns_o[...] = sst_ref[...]

    nb, tb, _ = proj_ref.shape
    rows = nb * tb
    p = proj_ref[...].reshape(rows, D_IN_PROJ)
    tidx = lax.broadcasted_iota(I32, (rows, 1), 0) & (tb - 1)

    def per_row(c):
        return jnp.broadcast_to(c, (nb, tb, c.shape[-1])).reshape(rows, c.shape[-1])

    gate_b = p[:, 0:D_CONV]
    gate_c = p[:, D_CONV:2 * D_CONV]
    h_conv = p[:, 2 * D_CONV:3 * D_CONV]
    pr = p[:, 3 * D_CONV:]

    u = gate_c * h_conv
    carry = nc_o[...]
    c2 = per_row(carry[:, 0:1, :])
    c1 = per_row(carry[:, 1:2, :])
    u1 = jnp.where(tidx == 0, c1, pltpu.roll(u, 1, 0))
    u2 = jnp.where(tidx == 0, c2, jnp.where(tidx == 1, c1, pltpu.roll(u, 2, 0)))
    cw = convw_ref[...]
    conv_y = u * cw[2:3, :] + u2 * cw[0:1, :] + u1 * cw[1:2, :]
    yc_o[...] = (gate_b * conv_y).astype(BF16)
    nc_o[...] = u.reshape(nb, tb, D_CONV)[:, tb - 2:tb, :]

    p_prev = jnp.where(tidx == 0, per_row(ns_o[...]), pltpu.roll(pr, 1, 0))
    ns_o[...] = pr.reshape(nb, tb, D_RWKV_PROJ)[:, tb - 1:tb, :]
    pm = pr + (p_prev - pr) * mu_ref[...]
    r = pm[:, 0:D_RWKV]
    k = pm[:, D_RWKV:2 * D_RWKV]
    v = pm[:, 2 * D_RWKV:3 * D_RWKV]
    x_lora = pm[:, 3 * D_RWKV:3 * D_RWKV + LORA_DECAY + LORA_AAA]
    xg = pm[:, 3 * D_RWKV + LORA_DECAY + LORA_AAA:]

    d_pre = w0_ref[...] + jnp.dot(jnp.tanh(x_lora).astype(BF16), wd_ref[...].astype(BF16),
                                  preferred_element_type=F32)
    a_pre = a0_ref[...] + jnp.dot(x_lora.astype(BF16), wa_ref[...].astype(BF16), preferred_element_type=F32)
    g = jnp.dot(_sigmoid(xg).astype(BF16), wg_ref[...].astype(BF16), preferred_element_type=F32)
    z = -d_pre
    softplus = jnp.maximum(z, 0.0) + jnp.log(1.0 + jnp.exp(-jnp.abs(z)))
    w_log = -softplus - 0.5
    decay = jnp.exp(-jnp.exp(w_log))
    a_h = _sigmoid(a_pre)

    ones = _head_ones()
    kk = k * kk_ref[...]
    kk = kk * lax.rsqrt(jnp.maximum(_head_sum(kk * kk, ones), 1e-24))
    k_h = k * (1.0 + (a_h - 1.0) * ka_ref[...])
    bonus = _head_sum(r * k_h * rk_ref[...], ones) * v

    _store_row_tiled(a_o, -kk, R_TILES)
    _store_row_tiled(w_o, decay, R_TILES)
    _store_row_tiled(b_o, kk * a_h, R_TILES)
    _store_row_tiled(k_o, k_h, R_TILES)
    _store_row_tiled(r_o, r, R_TILES)
    _store_row_tiled(v_o, v, R_TILES)
    g_o[...] = g
    bonus_o[...] = bonus


def _mixprep(proj3, conv_state, shift_state, lp, nb, tb):
    b, t, _ = proj3.shape
    nt = t // tb
    rows = nb * tb
    m = b * t

    def row_spec(width):
        return pl.BlockSpec((rows, width), lambda i, j: (i * nt + j, 0))

    def full2(a):
        return pl.BlockSpec(a.shape, lambda i, j: (0, 0))

    consts = (lp["conv_w"], lp["mix_mu"], lp["w0"], lp["wd_pad"], lp["a0"], lp["wa_pad"], lp["w_gate_up"],
              lp["k_k"], lp["k_a"], lp["r_k"])
    tiled_spec = pl.BlockSpec((rows * R_TILES, LANES), lambda i, j: (i * nt + j, 0))
    tiled_out = jax.ShapeDtypeStruct((m * R_TILES, LANES), F32)
    f32_out = jax.ShapeDtypeStruct((m, D_RWKV), F32)
    return pl.pallas_call(
        _mixprep_kernel,
        grid=(b // nb, nt),
        in_specs=[pl.BlockSpec((nb, tb, D_IN_PROJ), lambda i, j: (i, j, 0)),
                  pl.BlockSpec((nb, 2, D_CONV), lambda i, j: (i, 0, 0)),
                  pl.BlockSpec((nb, 1, D_RWKV_PROJ), lambda i, j: (i, 0, 0))] + [full2(c) for c in consts],
        out_specs=[tiled_spec] * N_WKV_OPERANDS + [row_spec(D_CONV), row_spec(D_RWKV), row_spec(D_RWKV),
                                                   pl.BlockSpec((nb, 2, D_CONV), lambda i, j: (i, 0, 0)),
                                                   pl.BlockSpec((nb, 1, D_RWKV_PROJ), lambda i, j: (i, 0, 0))],
        out_shape=[tiled_out] * N_WKV_OPERANDS + [jax.ShapeDtypeStruct((m, D_CONV), BF16), f32_out, f32_out,
                                                  jax.ShapeDtypeStruct((b, 2, D_CONV), F32),
                                                  jax.ShapeDtypeStruct((b, 1, D_RWKV_PROJ), F32)],
        compiler_params=_params(("arbitrary", "arbitrary")),
        name="mixprep",
    )(proj3, conv_state, shift_state, *consts)


def _wkv_kernel(a_ref, w_ref, b_ref, k_ref, r_ref, v_ref, s0_ref, y_ref, s_ref, tiles, yq):
    @pl.when(pl.program_id(1) == 0)
    def _():
        s_ref[...] = s0_ref[...]

    tt = a_ref.shape[1] // R_TILES
    srcs = (a_ref, w_ref, b_ref, k_ref, r_ref, v_ref)
    n_chain = WKV_BATCHES * R_TILES
    sub8 = lax.broadcasted_iota(I32, (SUBLANES, LANES), 0)
    grp8 = lax.broadcasted_iota(I32, (SUBLANES, LANES), 1) // n_chain
    diag8 = (sub8 % V_LO) == grp8
    sub4 = lax.broadcasted_iota(I32, (V_LO, LANES), 0)
    grp4 = lax.broadcasted_iota(I32, (V_LO, LANES), 1) // n_chain
    diag4 = sub4 == grp4

    def produce(t, slot):
        row0 = pl.multiple_of(t * R_TILES, R_TILES)
        for ai, ref in enumerate(srcs):
            m = jnp.concatenate([ref[bb, pl.ds(row0, R_TILES), :] for bb in range(WKV_BATCHES)], axis=0)
            xt = jnp.concatenate([m] * V_LO, axis=0).T
            if ai < N_WKV_OPERANDS - 1:
                tiles[slot, ai] = xt
            else:
                for q in range(LANES // SUBLANES):
                    z = jnp.where(diag8, xt[q * SUBLANES:(q + 1) * SUBLANES], 0.0)
                    z = z + pltpu.roll(z, 1, 0)
                    z = z + pltpu.roll(z, 2, 0)
                    tiles[slot, ai, q * SUBLANES:(q + 1) * SUBLANES, :] = z

    def consume(slot):
        for par in range(2):
            lo = par * HEAD_DIM
            a = tiles[slot, 0, lo:lo + HEAD_DIM, :]
            w = tiles[slot, 1, lo:lo + HEAD_DIM, :]
            b = tiles[slot, 2, lo:lo + HEAD_DIM, :]
            k = tiles[slot, 3, lo:lo + HEAD_DIM, :]
            r = tiles[slot, 4, lo:lo + HEAD_DIM, :]
            for vh in range(V_HI):
                s = s_ref[0, par, vh]
                vrow = tiles[slot, 5, lo + vh * V_LO + V_LO - 1:lo + (vh + 1) * V_LO, :]
                sa = jnp.sum(s * a, axis=0, keepdims=True)
                s = s * w + sa * b + vrow * k
                s_ref[0, par, vh] = s
                yrow = jnp.sum(s * r, axis=0, keepdims=True)
                yq[slot, lo + vh * V_LO:lo + (vh + 1) * V_LO, :] = jnp.where(
                    diag4, jnp.broadcast_to(yrow, (V_LO, LANES)), 0.0)

    def finalize(t, slot):
        qt = yq[slot].T
        res = qt[0:n_chain]
        for j in range(1, V_LO):
            res = res + qt[j * n_chain:(j + 1) * n_chain]
        row0 = pl.multiple_of(t * R_TILES, R_TILES)
        for bb in range(WKV_BATCHES):
            y_ref[bb, pl.ds(row0, R_TILES), :] = res[bb * R_TILES:(bb + 1) * R_TILES]

    produce(0, 0)
    produce(1, 1)
    consume(0)

    def body(i, c):
        t1 = 2 * i + 1
        produce(t1 + 1, 0)
        consume(1)
        finalize(t1 - 1, 0)
        produce(jnp.minimum(t1 + 2, tt - 1), 1)
        consume(0)
        finalize(t1, 1)
        return c

    lax.fori_loop(0, tt // 2 - 1, body, 0)
    consume(1)
    finalize(tt - 2, 0)
    finalize(tt - 1, 1)


def _wkv(operands, s0, tt):
    bsz, rows, _ = operands[0].shape
    t = rows // R_TILES
    g = bsz // WKV_BATCHES
    assert tt % 2 == 0 and tt >= 4 and t % tt == 0
    blk = pl.BlockSpec((WKV_BATCHES, tt * R_TILES, LANES), lambda i, j: (i, j, 0))
    state = pl.BlockSpec((1, 2, V_HI, HEAD_DIM, LANES), lambda i, j: (i, 0, 0, 0, 0))
    return pl.pallas_call(
        _wkv_kernel,
        grid=(g, t // tt),
        in_specs=[blk] * N_WKV_OPERANDS + [state],
        out_specs=[blk, state],
        out_shape=[jax.ShapeDtypeStruct((bsz, rows, LANES), F32),
                   jax.ShapeDtypeStruct((g, 2, V_HI, HEAD_DIM, LANES), F32)],
        scratch_shapes=[pltpu.VMEM((2, N_WKV_OPERANDS, LANES, LANES), F32), pltpu.VMEM((2, LANES, LANES), F32)],
        compiler_params=_params(("arbitrary", "arbitrary")),
        name="wkv",
    )(*operands, s0)


def _state_to_chain(s, g):
    s = s.reshape(g, WKV_BATCHES, R_TILES, 2, V_HI, V_LO, HEAD_DIM).transpose(0, 3, 4, 6, 5, 1, 2)
    return s.reshape(g, 2, V_HI, HEAD_DIM, LANES)


def _state_from_chain(s, g):
    s = s.reshape(g, 2, V_HI, HEAD_DIM, V_LO, WKV_BATCHES, R_TILES).transpose(0, 5, 6, 1, 2, 4, 3)
    return s.reshape(g * WKV_BATCHES, N_HEADS, HEAD_DIM, HEAD_DIM)


def _post_kernel(y_ref, bonus_ref, g_ref, lnw_ref, lnb_ref, o_ref):
    ones = _head_ones()
    y = _load_row_tiled(y_ref, o_ref.shape[0], R_TILES, R_TILES)
    mu = _head_sum(y, ones) * (1.0 / HEAD_DIM)
    d = y - mu
    var = _head_sum(d * d, ones) * (1.0 / HEAD_DIM)
    yn = d * lax.rsqrt(var + GN_EPS)
    yn = yn * lnw_ref[...] + lnb_ref[...]
    o_ref[...] = ((yn + bonus_ref[...]) * g_ref[...]).astype(BF16)


def _post(y_tiled, bonus, g, ln_w, ln_b):
    m = bonus.shape[0]
    tm = 512
    row = pl.BlockSpec((tm, D_RWKV), lambda i: (i, 0))
    vec = pl.BlockSpec((1, D_RWKV), lambda i: (0, 0))
    return pl.pallas_call(
        _post_kernel,
        grid=(m // tm,),
        in_specs=[pl.BlockSpec((tm * R_TILES, LANES), lambda i: (i, 0)), row, row, vec, vec],
        out_specs=row,
        out_shape=jax.ShapeDtypeStruct((m, D_RWKV), BF16),
        compiler_params=_params(("arbitrary",)),
        name="post",
    )(y_tiled, bonus, g, ln_w, ln_b)


def _cast_kernel(x_ref, o_ref):
    o_ref[...] = x_ref[...].astype(o_ref.dtype)


def _cast_bf16(w):
    r, c = w.shape
    tr = 512
    return pl.pallas_call(
        _cast_kernel,
        grid=(r // tr,),
        in_specs=[pl.BlockSpec((tr, c), lambda i: (i, 0))],
        out_specs=pl.BlockSpec((tr, c), lambda i: (i, 0)),
        out_shape=jax.ShapeDtypeStruct((r, c), BF16),
        compiler_params=_params(("arbitrary",)),
        name="cast_bf16",
    )(w)


def _route(lg):
    lane = lax.broadcasted_iota(I32, lg.shape, 1)
    neg = -jnp.inf
    big = jnp.int32(1 << 20)

    is_g = lane < N_GROUPS
    gl = jnp.where(is_g, lg, neg)
    gmax = jnp.max(gl, axis=1, keepdims=True)
    gidx = jnp.min(jnp.where(gl == gmax, lane, big), axis=1, keepdims=True)
    gsum = jnp.sum(jnp.where(is_g, jnp.exp(gl - gmax), 0.0), axis=1, keepdims=True)
    gprob = 1.0 / gsum

    e_lane = lane - N_GROUPS
    in_grp = (e_lane >= 0) & (e_lane < N_EXPERTS) & ((e_lane >> 3) == gidx)
    el = jnp.where(in_grp, lg, neg)
    emax = jnp.max(el, axis=1, keepdims=True)
    ex = jnp.where(in_grp, jnp.exp(el - emax), 0.0)
    prob = ex / jnp.sum(ex, axis=1, keepdims=True)
    prob = jnp.where(in_grp, prob, -1.0)
    p1 = jnp.max(prob, axis=1, keepdims=True)
    i1 = jnp.min(jnp.where(prob == p1, lane, big), axis=1, keepdims=True)
    prob2 = jnp.where(lane == i1, -1.0, prob)
    p2 = jnp.max(prob2, axis=1, keepdims=True)
    i2 = jnp.min(jnp.where(prob2 == p2, lane, big), axis=1, keepdims=True)
    psum = p1 + p2
    w1 = p1 / psum * gprob
    w2 = p2 / psum * gprob
    eid = jnp.where(lane == 0, i1 - N_GROUPS, jnp.where(lane == 1, i2 - N_GROUPS, 0))
    wts = jnp.where(lane == 0, w1, jnp.where(lane == 1, w2, 0.0))
    return eid, wts


def _outproj_kernel(yc_ref, yr_ref, wt_ref, wb_ref, x_ref, ga1_ref, sh2_ref, sc2_ref, g2_ref, wrh_ref, wrl_ref,
                    br_ref, x1_o, h2_o, eid_o, wts_o):
    nb, tb, d = x_ref.shape
    rows = nb * tb
    mix = (jnp.dot(yc_ref[...], wt_ref[...], preferred_element_type=F32)
           + jnp.dot(yr_ref[...], wb_ref[...], preferred_element_type=F32))
    x1 = x_ref[...] + ga1_ref[...] * mix.reshape(nb, tb, d)
    ms = jnp.mean(x1 * x1, axis=-1, keepdims=True)
    h2 = x1 * lax.rsqrt(ms + RMS_EPS) * g2_ref[...]
    h2 = (h2 * (1.0 + sc2_ref[...]) + sh2_ref[...]).reshape(rows, d)
    x1_o[...] = x1.reshape(rows, d)
    _store_row_tiled(h2_o, h2, D_TILES)
    h_hi, h_lo = _split_bf16(h2)
    w_hi = wrh_ref[...]
    logits = (jnp.dot(h_hi, w_hi, preferred_element_type=F32)
              + jnp.dot(h_lo, w_hi, preferred_element_type=F32)
              + jnp.dot(h_hi, wrl_ref[...], preferred_element_type=F32)) + br_ref[...]
    eid_o[...], wts_o[...] = _route(logits)


def _outproj(yc, yr, w_out_bf16, x, mod, g2, wr_hi, wr_lo, b_router, nb, tb):
    b, t, d = x.shape
    nt = t // tb
    rows = nb * tb
    m = b * t
    half = D_CONV
    row = lambda width: pl.BlockSpec((rows, width), lambda i, j: (i * nt + j, 0))
    modspec = lambda col: pl.BlockSpec((nb, 1, d), lambda i, j: (i, 0, col))
    full = lambda a: pl.BlockSpec(a.shape, lambda i, j: (0, 0))
    return pl.pallas_call(
        _outproj_kernel,
        grid=(b // nb, nt),
        in_specs=[row(half), row(half),
                  pl.BlockSpec((half, d), lambda i, j: (0, 0)),
                  pl.BlockSpec((half, d), lambda i, j: (1, 0)),
                  pl.BlockSpec((nb, tb, d), lambda i, j: (i, j, 0)),
                  modspec(2), modspec(3), modspec(4),
                  full(g2), full(wr_hi), full(wr_lo), full(b_router)],
        out_specs=[row(d), pl.BlockSpec((rows * D_TILES, LANES), lambda i, j: (i * nt + j, 0)),
                   row(LANES), row(LANES)],
        out_shape=[jax.ShapeDtypeStruct((m, d), F32), jax.ShapeDtypeStruct((m * D_TILES, LANES), F32),
                   jax.ShapeDtypeStruct((m, LANES), I32), jax.ShapeDtypeStruct((m, LANES), F32)],
        compiler_params=_params(("arbitrary", "arbitrary")),
        name="outproj",
    )(yc, yr, w_out_bf16, w_out_bf16, x, mod, mod, mod, g2, wr_hi, wr_lo, b_router)


def _row_copy(src_hbm, src_row, dst_ref, dst_row, sem):
    src = pl.multiple_of(src_row * D_TILES, D_TILES)
    dst = pl.multiple_of(dst_row * GATHER_PITCH, SUBLANES)
    return pltpu.make_async_copy(src_hbm.at[pl.ds(src, D_TILES)], dst_ref.at[pl.ds(dst, D_TILES)], sem)


def _start_row_gather(index_of, src_hbm, dst_ref, sem, n):
    def body(r, c):
        _row_copy(src_hbm, index_of(r), dst_ref, r, sem).start(priority=GATHER_DMA_PRIORITY)
        return c
    lax.fori_loop(0, n, body, 0, unroll=8)


def _wait_row_gather(src_hbm, dst_ref, sem, n):
    def body(r, c):
        _row_copy(src_hbm, 0, dst_ref, r, sem).wait()
        return c
    lax.fori_loop(0, n, body, 0, unroll=8)


def _experts_kernel(te_ref, nu_ref, nxt_ref, wsl_ref, idx_ref, idxn_ref, h_hbm, wg_hbm, wu_hbm, wd_hbm, o_ref,
                    xbuf, wgf, wuf, wdf, wgb, wub, wdb, sems, wsems):
    i = pl.program_id(0)
    n_used = nu_ref[0]
    slot = i & 1

    def weight_copies(e, ws):
        return (pltpu.make_async_copy(wg_hbm.at[e], wgf.at[ws], wsems.at[ws]),
                pltpu.make_async_copy(wu_hbm.at[e], wuf.at[ws], wsems.at[ws]),
                pltpu.make_async_copy(wd_hbm.at[e], wdf.at[ws], wsems.at[ws]))

    @pl.when(i == 0)
    def _():
        for c in weight_copies(te_ref[0], wsl_ref[0]):
            c.start(priority=WEIGHT_DMA_PRIORITY)
        _start_row_gather(lambda r: idx_ref[0, r], h_hbm, xbuf.at[0], sems.at[0], EXPERT_TILE)

    @pl.when(i + 1 < n_used)
    def _():
        _start_row_gather(lambda r: idxn_ref[0, r], h_hbm, xbuf.at[1 - slot], sems.at[1 - slot], EXPERT_TILE)

    @pl.when(i < n_used)
    def _():
        new_expert = jnp.logical_or(i == 0, te_ref[i] != te_ref[jnp.maximum(i - 1, 0)])

        @pl.when(new_expert)
        def _():
            ws = wsl_ref[i]
            for c in weight_copies(te_ref[i], ws):
                c.wait()
            wgb[...] = wgf[ws].astype(BF16)
            wub[...] = wuf[ws].astype(BF16)
            wdb[...] = wdf[ws].astype(BF16)

            @pl.when(nxt_ref[i] >= 0)
            def _():
                for c in weight_copies(nxt_ref[i], 1 - ws):
                    c.start(priority=WEIGHT_DMA_PRIORITY)

        _wait_row_gather(h_hbm, xbuf.at[slot], sems.at[slot], EXPERT_TILE)
        x = _load_row_tiled(xbuf.at[slot], EXPERT_TILE, D_TILES, GATHER_PITCH).astype(BF16)
        gate = jnp.dot(x, wgb[...], preferred_element_type=F32)
        up = jnp.dot(x, wub[...], preferred_element_type=F32)
        act = (gate * _sigmoid(gate) * up).astype(BF16)
        _store_row_tiled(o_ref, jnp.dot(act, wdb[...], preferred_element_type=F32), D_TILES)

    @pl.when(i >= n_used)
    def _():
        o_ref[...] = jnp.zeros_like(o_ref)


def _experts(plan, h2_tiled, w_gate, w_up, w_down):
    tile_expert, n_used, next_expert, weight_slot, src_rows = plan
    n_tiles = tile_expert.shape[0]
    d = w_gate.shape[1]
    idx3 = src_rows.reshape(n_tiles, 1, EXPERT_TILE)
    smem_idx = lambda f: pl.BlockSpec((None, 1, EXPERT_TILE), f, memory_space=pltpu.SMEM)
    hbm = pl.BlockSpec(memory_space=pl.ANY)
    grid_spec = pltpu.PrefetchScalarGridSpec(
        num_scalar_prefetch=4,
        grid=(n_tiles,),
        in_specs=[smem_idx(lambda i, *_: (i, 0, 0)),
                  smem_idx(lambda i, *_: (jnp.minimum(i + 1, n_tiles - 1), 0, 0)),
                  hbm, hbm, hbm, hbm],
        out_specs=pl.BlockSpec((EXPERT_TILE * D_TILES, LANES), lambda i, *_: (i, 0)),
        scratch_shapes=[pltpu.VMEM((2, EXPERT_TILE * GATHER_PITCH, LANES), F32),
                        pltpu.VMEM((2, d, D_EXPERT), F32), pltpu.VMEM((2, d, D_EXPERT), F32),
                        pltpu.VMEM((2, D_EXPERT, d), F32),
                        pltpu.VMEM((d, D_EXPERT), BF16), pltpu.VMEM((d, D_EXPERT), BF16),
                        pltpu.VMEM((D_EXPERT, d), BF16),
                        pltpu.SemaphoreType.DMA((2,)), pltpu.SemaphoreType.DMA((2,))],
    )
    return pl.pallas_call(
        _experts_kernel,
        grid_spec=grid_spec,
        out_shape=jax.ShapeDtypeStruct((n_tiles * EXPERT_TILE * D_TILES, LANES), F32),
        compiler_params=_params(("arbitrary",)),
        name="experts",
    )(tile_expert, n_used, next_expert, weight_slot, idx3, idx3, h2_tiled, w_gate, w_up, w_down)


def _combine_kernel(idx_ref, idxn_ref, wt_ref, x1_ref, ga2_ref, gf_ref, ys_hbm, o_ref, ybuf, sems):
    i = pl.program_id(0)
    n_steps = pl.num_programs(0)
    slot = i & 1
    nb, tb, d = o_ref.shape
    rows = nb * tb

    def fetch(idx, s):
        for j in range(2):
            _start_row_gather(lambda r: idx[0, 2 * r + j], ys_hbm, ybuf.at[s, j], sems.at[s], rows)

    @pl.when(i == 0)
    def _():
        fetch(idx_ref, 0)

    @pl.when(i + 1 < n_steps)
    def _():
        fetch(idxn_ref, 1 - slot)

    for j in range(2):
        _wait_row_gather(ys_hbm, ybuf.at[slot, j], sems.at[slot], rows)

    wt = wt_ref[...]
    moe = (wt[:, 0:1] * _load_row_tiled(ybuf.at[slot, 0], rows, D_TILES, GATHER_PITCH)
           + wt[:, 1:2] * _load_row_tiled(ybuf.at[slot, 1], rows, D_TILES, GATHER_PITCH))
    x2 = x1_ref[...].reshape(nb, tb, d) + ga2_ref[...] * moe.reshape(nb, tb, d)
    ms = jnp.mean(x2 * x2, axis=-1, keepdims=True)
    o_ref[...] = x2 * lax.rsqrt(ms + RMS_EPS) * gf_ref[...]


def _combine(dest, wts, x1, mod, final_g, ys, b, t, nb, tb):
    d = x1.shape[1]
    nt = t // tb
    rows = nb * tb
    n_steps = (b // nb) * nt
    idx3 = dest.reshape(n_steps, 1, 2 * rows)
    smem_idx = lambda f: pl.BlockSpec((None, 1, 2 * rows), f, memory_space=pltpu.SMEM)
    return pl.pallas_call(
        _combine_kernel,
        grid=(n_steps,),
        in_specs=[smem_idx(lambda i: (i, 0, 0)),
                  smem_idx(lambda i: (jnp.minimum(i + 1, n_steps - 1), 0, 0)),
                  pl.BlockSpec((rows, LANES), lambda i: (i, 0)),
                  pl.BlockSpec((rows, d), lambda i: (i, 0)),
                  pl.BlockSpec((nb, 1, d), lambda i: (i // nt, 0, 5)),
                  pl.BlockSpec((1, d), lambda i: (0, 0)),
                  pl.BlockSpec(memory_space=pl.ANY)],
        out_specs=pl.BlockSpec((nb, tb, d), lambda i: (i // nt, i % nt, 0)),
        out_shape=jax.ShapeDtypeStruct((b, t, d), F32),
        scratch_shapes=[pltpu.VMEM((2, 2, rows * GATHER_PITCH, LANES), F32), pltpu.SemaphoreType.DMA((2,))],
        compiler_params=_params(("arbitrary",)),
        name="combine",
    )(idx3, idx3, wts, x1, mod, final_g, ys)


def _dispatch_plan(eid):
    n_pairs = eid.shape[0] * 2
    n_tiles = n_pairs // EXPERT_TILE + N_EXPERTS
    flat = eid.reshape(-1)
    onehot = (flat[:, None] == jnp.arange(N_EXPERTS, dtype=I32)[None, :]).astype(I32)
    csum = jnp.cumsum(onehot, axis=0)
    rank = jnp.sum((csum - onehot) * onehot, axis=1)
    counts = csum[-1]
    tiles_per = (counts + EXPERT_TILE - 1) // EXPERT_TILE
    tile_end = jnp.cumsum(tiles_per)
    row_start = (tile_end - tiles_per) * EXPERT_TILE
    dest = jnp.sum(onehot * row_start[None, :], axis=1) + rank
    n_used = tile_end[-1]
    tile_ids = jnp.arange(n_tiles, dtype=I32)
    tile_expert = jnp.sum((tile_ids[:, None] >= tile_end[None, :]).astype(I32), axis=1)
    last_expert = jnp.max(jnp.where(counts > 0, jnp.arange(N_EXPERTS, dtype=I32), 0))
    tile_expert = jnp.where(tile_ids < n_used, tile_expert, last_expert).astype(I32)
    token = jnp.arange(n_pairs, dtype=I32) // 2
    filler = jnp.arange(n_tiles * EXPERT_TILE, dtype=I32) % (n_pairs // 2)
    src_rows = filler.at[dest].set(token)
    experts = jnp.arange(N_EXPERTS, dtype=I32)
    used = counts > 0
    later_used = used[None, :] & (experts[None, :] > experts[:, None])
    next_used = jnp.min(jnp.where(later_used, experts[None, :], N_EXPERTS), axis=1)
    next_used = jnp.where(next_used < N_EXPERTS, next_used, -1).astype(I32)
    slot_of = ((jnp.cumsum(used.astype(I32)) - 1) & 1).astype(I32)
    plan = (tile_expert, n_used.reshape(1).astype(I32), next_used[tile_expert], slot_of[tile_expert], src_rows)
    return dest.astype(I32), plan


def _stream_mix(x, mod, conv_state, shift_state, wkv_state, lp, nb, tb, tt):
    b, t, d = x.shape
    g = b // WKV_BATCHES
    h = _norm1(x, mod, lp["norm1_g"], nb, tb)
    proj = _inproj(h, lp["w_in"])
    outs = _mixprep(proj.reshape(b, t, D_IN_PROJ), conv_state, shift_state, lp, nb, tb)
    operands = [z.reshape(b, t * R_TILES, LANES) for z in outs[:N_WKV_OPERANDS]]
    yc, gate, bonus, new_conv, new_shift = outs[N_WKV_OPERANDS:]
    y_tiled, s_chain = _wkv(operands, _state_to_chain(wkv_state, g), tt)
    new_wkv = _state_from_chain(s_chain, g)
    yr = _post(y_tiled.reshape(b * t * R_TILES, LANES), bonus, gate, lp["ln_x_w"], lp["ln_x_b"])
    x1, h2_tiled, eid, wts = _outproj(yc, yr, lp["w_out_bf16"], x, mod, lp["norm2_g"], lp["wr_hi"], lp["wr_lo"],
                                      lp["b_router"], nb, tb)
    return x1, h2_tiled, eid, wts, new_conv, new_shift.reshape(b, D_RWKV_PROJ), new_wkv


def kernel(x_prompt, x_sample, c_prompt, c_sample, state_conv, state_shift, state_wkv, w_ada, b_ada, norm1_g, w_in, conv_w, mix_mu, w0, w_decay_up, a0, w_aaa_up, w_gate_up, k_k, k_a, r_k, ln_x_w, ln_x_b, w_out, norm2_g, w_router_group, b_router_group, w_router_expert, b_router_expert, w_gate, w_up, w_down, final_g):
    depth = w_ada.shape[0]
    assert depth == 1
    bp, tp, d = x_prompt.shape
    bs, ts, _ = x_sample.shape
    l = 0

    row = lambda z: z.reshape(1, -1)
    zeros_lora = jnp.zeros((LORA_DECAY, D_RWKV), F32)
    n_route = N_GROUPS + N_EXPERTS
    w_router = jnp.pad(jnp.concatenate([w_router_group[l], w_router_expert[l]], axis=1),
                       ((0, 0), (0, LANES - n_route)))
    b_router = jnp.concatenate([b_router_group[l], b_router_expert[l]])
    wr_hi = w_router.astype(BF16)
    lp = dict(
        norm1_g=row(norm1_g[l]), w_in=w_in[l], conv_w=conv_w[l], mix_mu=row(mix_mu[l]), w0=row(w0[l]),
        wd_pad=jnp.concatenate([w_decay_up[l], zeros_lora], axis=0), a0=row(a0[l]),
        wa_pad=jnp.concatenate([zeros_lora, w_aaa_up[l]], axis=0), w_gate_up=w_gate_up[l],
        k_k=row(k_k[l]), k_a=row(k_a[l]), r_k=row(r_k[l]), ln_x_w=row(ln_x_w[l]), ln_x_b=row(ln_x_b[l]),
        w_out_bf16=_cast_bf16(w_out[l]), norm2_g=row(norm2_g[l]),
        wr_hi=wr_hi, wr_lo=(w_router - wr_hi.astype(F32)).astype(BF16),
        b_router=jnp.pad(b_router, (0, LANES - n_route)).reshape(1, LANES),
    )

    n_c = bp + bs
    c_all = jnp.concatenate([c_prompt, c_sample], axis=0)
    c_all = jnp.pad(c_all, ((0, (-n_c) % SUBLANES), (0, 0)))
    mod = _ada(c_all, w_ada[l], row(b_ada[l]))
    mod_p = mod[:bp].reshape(bp, 1, N_MOD * d)
    mod_s = mod[bp:n_c].reshape(bs, 1, N_MOD * d)

    zc = jnp.zeros((bp, 2, D_CONV), F32)
    zs = jnp.zeros((bp, 1, D_RWKV_PROJ), F32)
    zw = jnp.zeros((bp, N_HEADS, HEAD_DIM, HEAD_DIM), F32)
    n_p = bp * tp
    x1_p, h2_p, eid_p, wts_p, conv_p, shift_p, wkv_p = _stream_mix(
        x_prompt, mod_p, zc, zs, zw, lp, nb=1, tb=ROW_TILE, tt=64)
    x1_s, h2_s, eid_s, wts_s, conv_s, shift_s, wkv_s = _stream_mix(
        x_sample, mod_s, state_conv[l], state_shift[l].reshape(bs, 1, D_RWKV_PROJ), state_wkv[l], lp,
        nb=ROW_TILE // ts, tb=ts, tt=ts)

    h2 = jnp.concatenate([h2_p, h2_s], axis=0)
    eid = jnp.concatenate([eid_p[:, :2], eid_s[:, :2]], axis=0)
    dest, plan = _dispatch_plan(eid)
    ys = _experts(plan, h2, w_gate[l], w_up[l], w_down[l])
    y_prompt = _combine(dest[:2 * n_p], wts_p, x1_p, mod_p, row(final_g), ys, bp, tp, nb=1, tb=ROW_TILE)
    y_sample = _combine(dest[2 * n_p:], wts_s, x1_s, mod_s, row(final_g), ys, bs, ts, nb=ROW_TILE // ts, tb=ts)

    return (y_prompt, y_sample, conv_p[None], shift_p[None], wkv_p[None],
            conv_s[None], shift_s[None], wkv_s[None])
```

```python
import functools

import jax
import jax.numpy as jnp
from jax import lax
from jax.experimental import pallas as pl
from jax.experimental.pallas import tpu as pltpu

F32 = jnp.float32
BF16 = jnp.bfloat16
I32 = jnp.int32

D_MODEL = 2048
D_CONV = 1024
D_RWKV = 1024
HEAD_DIM = 64
N_HEADS = 16
LORA_DECAY = 64
LORA_AAA = 64
LORA_GATE = 128
D_RWKV_PROJ = 3 * D_RWKV + LORA_DECAY + LORA_AAA + LORA_GATE
D_IN_PROJ = 3 * D_CONV + D_RWKV_PROJ
N_GROUPS = 4
EXPERTS_PER_GROUP = 8
N_EXPERTS = 32
D_EXPERT = 512
N_MOD = 6
RMS_EPS = 1e-6
GN_EPS = HEAD_DIM * 1e-5

SUBLANES = 8
LANES = 128
VMEM_LIMIT_BYTES = 56 * 1024 * 1024

ROW_TILE = 256
EXPERT_TILE = 256
SEG = 256
D_TILES = D_MODEL // LANES
R_TILES = D_RWKV // LANES
GATHER_PITCH = D_TILES + SUBLANES
GATHER_DMA_PRIORITY = 0
WEIGHT_DMA_PRIORITY = 1

WKV_BATCHES = 4
V_LO = LANES // (WKV_BATCHES * R_TILES)
V_HI = HEAD_DIM // V_LO
N_WKV_OPERANDS = 6


def _params(semantics):
    return pltpu.CompilerParams(dimension_semantics=semantics, vmem_limit_bytes=VMEM_LIMIT_BYTES)


def _sigmoid(x):
    return 1.0 / (1.0 + jnp.exp(-x))


def _store_row_tiled(ref, x, pitch):
    rows = x.shape[0]
    for s in range(x.shape[1] // LANES):
        ref[pl.ds(s, rows, stride=pitch), :] = x[:, s * LANES:(s + 1) * LANES]


def _load_row_tiled(ref, rows, n, pitch):
    return jnp.concatenate([ref[pl.ds(s, rows, stride=pitch), :] for s in range(n)], axis=-1)


def _head_ones():
    r = lax.broadcasted_iota(I32, (SEG, SEG), 0) // HEAD_DIM
    c = lax.broadcasted_iota(I32, (SEG, SEG), 1) // HEAD_DIM
    return jnp.where(r == c, 1.0, 0.0).astype(BF16)


def _split_bf16(x):
    hi = x.astype(BF16)
    return hi, (x - hi.astype(F32)).astype(BF16)


def _head_sum(x, ones):
    outs = []
    for s in range(x.shape[-1] // SEG):
        hi, lo = _split_bf16(x[:, s * SEG:(s + 1) * SEG])
        outs.append(jnp.dot(hi, ones, preferred_element_type=F32) + jnp.dot(lo, ones, preferred_element_type=F32))
    return jnp.concatenate(outs, axis=-1)


def _ada_kernel(c_ref, w_ref, b_ref, o_ref):
    c = c_ref[...]
    s = c * _sigmoid(c)
    o_ref[...] = jnp.dot(s.astype(BF16), w_ref[...].astype(BF16), preferred_element_type=F32) + b_ref[...]


def _ada(c_all, w_ada, b_ada):
    mb = c_all.shape[0]
    n = w_ada.shape[1]
    tn = 1024
    return pl.pallas_call(
        _ada_kernel,
        grid=(n // tn,),
        in_specs=[pl.BlockSpec((mb, D_MODEL), lambda j: (0, 0)),
                  pl.BlockSpec((D_MODEL, tn), lambda j: (0, j)),
                  pl.BlockSpec((1, tn), lambda j: (0, j))],
        out_specs=pl.BlockSpec((mb, tn), lambda j: (0, j)),
        out_shape=jax.ShapeDtypeStruct((mb, n), F32),
        compiler_params=_params(("arbitrary",)),
        name="ada",
    )(c_all, w_ada, b_ada)


def _norm1_kernel(x_ref, g_ref, sh_ref, sc_ref, o_ref):
    x = x_ref[...]
    nb, tb, d = x.shape
    ms = jnp.mean(x * x, axis=-1, keepdims=True)
    xn = x * lax.rsqrt(ms + RMS_EPS) * g_ref[...]
    h = xn * (1.0 + sc_ref[...]) + sh_ref[...]
    o_ref[...] = h.reshape(nb * tb, d).astype(BF16)


def _norm1(x, mod, g, nb, tb):
    b, t, d = x.shape
    nt = t // tb
    return pl.pallas_call(
        _norm1_kernel,
        grid=(b // nb, nt),
        in_specs=[pl.BlockSpec((nb, tb, d), lambda i, j: (i, j, 0)),
                  pl.BlockSpec((1, d), lambda i, j: (0, 0)),
                  pl.BlockSpec((nb, 1, d), lambda i, j: (i, 0, 0)),
                  pl.BlockSpec((nb, 1, d), lambda i, j: (i, 0, 1))],
        out_specs=pl.BlockSpec((nb * tb, d), lambda i, j: (i * nt + j, 0)),
        out_shape=jax.ShapeDtypeStruct((b * t, d), BF16),
        compiler_params=_params(("arbitrary", "arbitrary")),
        name="norm1",
    )(x, g, mod, mod)


def _inproj_kernel(h_ref, w_ref, o_ref, wb_ref):
    @pl.when(pl.program_id(1) == 0)
    def _():
        wb_ref[...] = w_ref[...].astype(BF16)

    o_ref[...] = jnp.dot(h_ref[...], wb_ref[...], preferred_element_type=F32)


def _inproj(h, w_in):
    m, k = h.shape
    n = w_in.shape[1]
    tm = min(m, 1024)
    tn = 1280
    return pl.pallas_call(
        _inproj_kernel,
        grid=(n // tn, m // tm),
        in_specs=[pl.BlockSpec((tm, k), lambda j, i: (i, 0)),
                  pl.BlockSpec((k, tn), lambda j, i: (0, j))],
        out_specs=pl.BlockSpec((tm, tn), lambda j, i: (i, j)),
        out_shape=jax.ShapeDtypeStruct((m, n), F32),
        scratch_shapes=[pltpu.VMEM((k, tn), BF16)],
        compiler_params=_params(("arbitrary", "arbitrary")),
        name="inproj",
    )(h, w_in)


def _mixprep_kernel(proj_ref, cst_ref, sst_ref, convw_ref, mu_ref, w0_ref, wd_ref, a0_ref, wa_ref, wg_ref,
                    kk_ref, ka_ref, rk_ref,
                    a_o, w_o, b_o, k_o, r_o, v_o, yc_o, g_o, bonus_o, nc_o, ns_o):
    @pl.when(pl.program_id(1) == 0)
    def _():
        nc_o[...] = cst_ref[...]
        ns_o[...] = sst_ref[...]

    nb, tb, _ = proj_ref.shape
    rows = nb * tb
    p = proj_ref[...].reshape(rows, D_IN_PROJ)
    tidx = lax.broadcasted_iota(I32, (rows, 1), 0) & (tb - 1)

    def per_row(c):
        return jnp.broadcast_to(c, (nb, tb, c.shape[-1])).reshape(rows, c.shape[-1])

    gate_b = p[:, 0:D_CONV]
    gate_c = p[:, D_CONV:2 * D_CONV]
    h_conv = p[:, 2 * D_CONV:3 * D_CONV]
    pr = p[:, 3 * D_CONV:]

    u = gate_c * h_conv
    carry = nc_o[...]
    c2 = per_row(carry[:, 0:1, :])
    c1 = per_row(carry[:, 1:2, :])
    u1 = jnp.where(tidx == 0, c1, pltpu.roll(u, 1, 0))
    u2 = jnp.where(tidx == 0, c2, jnp.where(tidx == 1, c1, pltpu.roll(u, 2, 0)))
    cw = convw_ref[...]
    conv_y = u * cw[2:3, :] + u2 * cw[0:1, :] + u1 * cw[1:2, :]
    yc_o[...] = (gate_b * conv_y).astype(BF16)
    nc_o[...] = u.reshape(nb, tb, D_CONV)[:, tb - 2:tb, :]

    p_prev = jnp.where(tidx == 0, per_row(ns_o[...]), pltpu.roll(pr, 1, 0))
    ns_o[...] = pr.reshape(nb, tb, D_RWKV_PROJ)[:, tb - 1:tb, :]
    pm = pr + (p_prev - pr) * mu_ref[...]
    r = pm[:, 0:D_RWKV]
    k = pm[:, D_RWKV:2 * D_RWKV]
    v = pm[:, 2 * D_RWKV:3 * D_RWKV]
    x_lora = pm[:, 3 * D_RWKV:3 * D_RWKV + LORA_DECAY + LORA_AAA]
    xg = pm[:, 3 * D_RWKV + LORA_DECAY + LORA_AAA:]

    d_pre = w0_ref[...] + jnp.dot(jnp.tanh(x_lora).astype(BF16), wd_ref[...].astype(BF16),
                                  preferred_element_type=F32)
    a_pre = a0_ref[...] + jnp.dot(x_lora.astype(BF16), wa_ref[...].astype(BF16), preferred_element_type=F32)
    g = jnp.dot(_sigmoid(xg).astype(BF16), wg_ref[...].astype(BF16), preferred_element_type=F32)
    z = -d_pre
    softplus = jnp.maximum(z, 0.0) + jnp.log(1.0 + jnp.exp(-jnp.abs(z)))
    w_log = -softplus - 0.5
    decay = jnp.exp(-jnp.exp(w_log))
    a_h = _sigmoid(a_pre)

    ones = _head_ones()
    kk = k * kk_ref[...]
    kk = kk * lax.rsqrt(jnp.maximum(_head_sum(kk * kk, ones), 1e-24))
    k_h = k * (1.0 + (a_h - 1.0) * ka_ref[...])
    bonus = _head_sum(r * k_h * rk_ref[...], ones) * v

    _store_row_tiled(a_o, -kk, R_TILES)
    _store_row_tiled(w_o, decay, R_TILES)
    _store_row_tiled(b_o, kk * a_h, R_TILES)
    _store_row_tiled(k_o, k_h, R_TILES)
    _store_row_tiled(r_o, r, R_TILES)
    _store_row_tiled(v_o, v, R_TILES)
    g_o[...] = g
    bonus_o[...] = bonus


def _mixprep(proj3, conv_state, shift_state, lp, nb, tb):
    b, t, _ = proj3.shape
    nt = t // tb
    rows = nb * tb
    m = b * t

    def row_spec(width):
        return pl.BlockSpec((rows, width), lambda i, j: (i * nt + j, 0))

    def full2(a):
        return pl.BlockSpec(a.shape, lambda i, j: (0, 0))

    consts = (lp["conv_w"], lp["mix_mu"], lp["w0"], lp["wd_pad"], lp["a0"], lp["wa_pad"], lp["w_gate_up"],
              lp["k_k"], lp["k_a"], lp["r_k"])
    tiled_spec = pl.BlockSpec((rows * R_TILES, LANES), lambda i, j: (i * nt + j, 0))
    tiled_out = jax.ShapeDtypeStruct((m * R_TILES, LANES), F32)
    f32_out = jax.ShapeDtypeStruct((m, D_RWKV), F32)
    return pl.pallas_call(
        _mixprep_kernel,
        grid=(b // nb, nt),
        in_specs=[pl.BlockSpec((nb, tb, D_IN_PROJ), lambda i, j: (i, j, 0)),
                  pl.BlockSpec((nb, 2, D_CONV), lambda i, j: (i, 0, 0)),
                  pl.BlockSpec((nb, 1, D_RWKV_PROJ), lambda i, j: (i, 0, 0))] + [full2(c) for c in consts],
        out_specs=[tiled_spec] * N_WKV_OPERANDS + [row_spec(D_CONV), row_spec(D_RWKV), row_spec(D_RWKV),
                                                   pl.BlockSpec((nb, 2, D_CONV), lambda i, j: (i, 0, 0)),
                                                   pl.BlockSpec((nb, 1, D_RWKV_PROJ), lambda i, j: (i, 0, 0))],
        out_shape=[tiled_out] * N_WKV_OPERANDS + [jax.ShapeDtypeStruct((m, D_CONV), BF16), f32_out, f32_out,
                                                  jax.ShapeDtypeStruct((b, 2, D_CONV), F32),
                                                  jax.ShapeDtypeStruct((b, 1, D_RWKV_PROJ), F32)],
        compiler_params=_params(("arbitrary", "arbitrary")),
        name="mixprep",
    )(proj3, conv_state, shift_state, *consts)


def _wkv_kernel(a_ref, w_ref, b_ref, k_ref, r_ref, v_ref, s0_ref, y_ref, s_ref, tiles, yq):
    @pl.when(pl.program_id(1) == 0)
    def _():
        s_ref[...] = s0_ref[...]

    tt = a_ref.shape[1] // R_TILES
    srcs = (a_ref, w_ref, b_ref, k_ref, r_ref, v_ref)
    n_chain = WKV_BATCHES * R_TILES
    sub8 = lax.broadcasted_iota(I32, (SUBLANES, LANES), 0)
    grp8 = lax.broadcasted_iota(I32, (SUBLANES, LANES), 1) // n_chain
    diag8 = (sub8 % V_LO) == grp8
    sub4 = lax.broadcasted_iota(I32, (V_LO, LANES), 0)
    grp4 = lax.broadcasted_iota(I32, (V_LO, LANES), 1) // n_chain
    diag4 = sub4 == grp4

    def produce(t, slot):
        row0 = pl.multiple_of(t * R_TILES, R_TILES)
        for ai, ref in enumerate(srcs):
            m = jnp.concatenate([ref[bb, pl.ds(row0, R_TILES), :] for bb in range(WKV_BATCHES)], axis=0)
            xt = jnp.concatenate([m] * V_LO, axis=0).T
            if ai < N_WKV_OPERANDS - 1:
                tiles[slot, ai] = xt
            else:
                for q in range(LANES // SUBLANES):
                    z = jnp.where(diag8, xt[q * SUBLANES:(q + 1) * SUBLANES], 0.0)
                    z = z + pltpu.roll(z, 1, 0)
                    z = z + pltpu.roll(z, 2, 0)
                    tiles[slot, ai, q * SUBLANES:(q + 1) * SUBLANES, :] = z

    def consume(slot):
        for par in range(2):
            lo = par * HEAD_DIM
            a = tiles[slot, 0, lo:lo + HEAD_DIM, :]
            w = tiles[slot, 1, lo:lo + HEAD_DIM, :]
            b = tiles[slot, 2, lo:lo + HEAD_DIM, :]
            k = tiles[slot, 3, lo:lo + HEAD_DIM, :]
            r = tiles[slot, 4, lo:lo + HEAD_DIM, :]
            for vh in range(V_HI):
                s = s_ref[0, par, vh]
                vrow = tiles[slot, 5, lo + vh * V_LO + V_LO - 1:lo + (vh + 1) * V_LO, :]
                sa = jnp.sum(s * a, axis=0, keepdims=True)
                s = s * w + sa * b + vrow * k
                s_ref[0, par, vh] = s
                yrow = jnp.sum(s * r, axis=0, keepdims=True)
                yq[slot, lo + vh * V_LO:lo + (vh + 1) * V_LO, :] = jnp.where(
                    diag4, jnp.broadcast_to(yrow, (V_LO, LANES)), 0.0)

    def finalize(t, slot):
        qt = yq[slot].T
        res = qt[0:n_chain]
        for j in range(1, V_LO):
            res = res + qt[j * n_chain:(j + 1) * n_chain]
        row0 = pl.multiple_of(t * R_TILES, R_TILES)
        for bb in range(WKV_BATCHES):
            y_ref[bb, pl.ds(row0, R_TILES), :] = res[bb * R_TILES:(bb + 1) * R_TILES]

    produce(0, 0)
    produce(1, 1)
    consume(0)

    def body(i, c):
        t1 = 2 * i + 1
        produce(t1 + 1, 0)
        consume(1)
        finalize(t1 - 1, 0)
        produce(jnp.minimum(t1 + 2, tt - 1), 1)
        consume(0)
        finalize(t1, 1)
        return c

    lax.fori_loop(0, tt // 2 - 1, body, 0)
    consume(1)
    finalize(tt - 2, 0)
    finalize(tt - 1, 1)


def _wkv(operands, s0, tt):
    bsz, rows, _ = operands[0].shape
    t = rows // R_TILES
    g = bsz // WKV_BATCHES
    assert tt % 2 == 0 and tt >= 4 and t % tt == 0
    blk = pl.BlockSpec((WKV_BATCHES, tt * R_TILES, LANES), lambda i, j: (i, j, 0))
    state = pl.BlockSpec((1, 2, V_HI, HEAD_DIM, LANES), lambda i, j: (i, 0, 0, 0, 0))
    return pl.pallas_call(
        _wkv_kernel,
        grid=(g, t // tt),
        in_specs=[blk] * N_WKV_OPERANDS + [state],
        out_specs=[blk, state],
        out_shape=[jax.ShapeDtypeStruct((bsz, rows, LANES), F32),
                   jax.ShapeDtypeStruct((g, 2, V_HI, HEAD_DIM, LANES), F32)],
        scratch_shapes=[pltpu.VMEM((2, N_WKV_OPERANDS, LANES, LANES), F32), pltpu.VMEM((2, LANES, LANES), F32)],
        compiler_params=_params(("arbitrary", "arbitrary")),
        name="wkv",
    )(*operands, s0)


def _state_to_chain(s, g):
    s = s.reshape(g, WKV_BATCHES, R_TILES, 2, V_HI, V_LO, HEAD_DIM).transpose(0, 3, 4, 6, 5, 1, 2)
    return s.reshape(g, 2, V_HI, HEAD_DIM, LANES)


def _state_from_chain(s, g):
    s = s.reshape(g, 2, V_HI, HEAD_DIM, V_LO, WKV_BATCHES, R_TILES).transpose(0, 5, 6, 1, 2, 4, 3)
    return s.reshape(g * WKV_BATCHES, N_HEADS, HEAD_DIM, HEAD_DIM)


def _group_norm_gate(y, bonus, g, ln_w, ln_b):
    ones = _head_ones()
    mu = _head_sum(y, ones) * (1.0 / HEAD_DIM)
    d = y - mu
    var = _head_sum(d * d, ones) * (1.0 / HEAD_DIM)
    yn = d * lax.rsqrt(var + GN_EPS)
    yn = yn * ln_w + ln_b
    return (yn + bonus) * g


def _cast_kernel(x_ref, o_ref):
    o_ref[...] = x_ref[...].astype(o_ref.dtype)


def _cast_bf16(w):
    r, c = w.shape
    tr = 512
    return pl.pallas_call(
        _cast_kernel,
        grid=(r // tr,),
        in_specs=[pl.BlockSpec((tr, c), lambda i: (i, 0))],
        out_specs=pl.BlockSpec((tr, c), lambda i: (i, 0)),
        out_shape=jax.ShapeDtypeStruct((r, c), BF16),
        compiler_params=_params(("arbitrary",)),
        name="cast_bf16",
    )(w)


def _route(lg):
    lane = lax.broadcasted_iota(I32, lg.shape, 1)
    neg = -jnp.inf
    big = jnp.int32(1 << 20)

    is_g = lane < N_GROUPS
    gl = jnp.where(is_g, lg, neg)
    gmax = jnp.max(gl, axis=1, keepdims=True)
    gidx = jnp.min(jnp.where(gl == gmax, lane, big), axis=1, keepdims=True)
    gsum = jnp.sum(jnp.where(is_g, jnp.exp(gl - gmax), 0.0), axis=1, keepdims=True)
    gprob = 1.0 / gsum

    e_lane = lane - N_GROUPS
    in_grp = (e_lane >= 0) & (e_lane < N_EXPERTS) & ((e_lane >> 3) == gidx)
    el = jnp.where(in_grp, lg, neg)
    emax = jnp.max(el, axis=1, keepdims=True)
    ex = jnp.where(in_grp, jnp.exp(el - emax), 0.0)
    prob = ex / jnp.sum(ex, axis=1, keepdims=True)
    prob = jnp.where(in_grp, prob, -1.0)
    p1 = jnp.max(prob, axis=1, keepdims=True)
    i1 = jnp.min(jnp.where(prob == p1, lane, big), axis=1, keepdims=True)
    prob2 = jnp.where(lane == i1, -1.0, prob)
    p2 = jnp.max(prob2, axis=1, keepdims=True)
    i2 = jnp.min(jnp.where(prob2 == p2, lane, big), axis=1, keepdims=True)
    psum = p1 + p2
    w1 = p1 / psum * gprob
    w2 = p2 / psum * gprob
    eid = jnp.where(lane == 0, i1 - N_GROUPS, jnp.where(lane == 1, i2 - N_GROUPS, 0))
    wts = jnp.where(lane == 0, w1, jnp.where(lane == 1, w2, 0.0))
    return eid, wts


def _outproj_kernel(n_main, yc_ref, y_ref, bonus_ref, gate_ref, lnw_ref, lnb_ref, wt_ref, wb_ref, x_ref,
                    ga1_ref, sh2_ref, sc2_ref, g2_ref, wrh_ref, wrl_ref, br_ref, *rest):
    x1_o, h2_o, eid_o, wts_o = rest[-4:]
    i = pl.program_id(0)

    @pl.when(i < n_main)
    def _():
        nb, tb, d = x_ref.shape
        rows = nb * tb
        y = _load_row_tiled(y_ref, rows, R_TILES, R_TILES)
        yr = _group_norm_gate(y, bonus_ref[...], gate_ref[...], lnw_ref[...], lnb_ref[...]).astype(BF16)
        mix = (jnp.dot(yc_ref[...], wt_ref[...], preferred_element_type=F32)
               + jnp.dot(yr, wb_ref[...], preferred_element_type=F32))
        x1 = x_ref[...] + ga1_ref[...] * mix.reshape(nb, tb, d)
        ms = jnp.mean(x1 * x1, axis=-1, keepdims=True)
        h2 = x1 * lax.rsqrt(ms + RMS_EPS) * g2_ref[...]
        h2 = (h2 * (1.0 + sc2_ref[...]) + sh2_ref[...]).reshape(rows, d)
        x1_o[...] = x1.reshape(rows, d)
        _store_row_tiled(h2_o, h2, D_TILES)
        h_hi, h_lo = _split_bf16(h2)
        w_hi = wrh_ref[...]
        logits = (jnp.dot(h_hi, w_hi, preferred_element_type=F32)
                  + jnp.dot(h_lo, w_hi, preferred_element_type=F32)
                  + jnp.dot(h_hi, wrl_ref[...], preferred_element_type=F32)) + br_ref[...]
        eid_o[...], wts_o[...] = _route(logits)

    if len(rest) == 5:
        @pl.when(i >= n_main)
        def _():
            h2_o[...] = rest[0][...]


def _outproj(yc, y_tiled, bonus, gate, ln_w, ln_b, w_out_bf16, x, mod, g2, wr_hi, wr_lo, b_router, nb, tb,
             h2_tail=None):
    b, t, d = x.shape
    nt = t // tb
    rows = nb * tb
    m = b * t
    half = D_CONV
    n_main = (b // nb) * nt
    n_tail = 0 if h2_tail is None else h2_tail.shape[0] // (rows * D_TILES)
    blk = lambda i: jnp.minimum(i, n_main - 1)
    row = lambda width: pl.BlockSpec((rows, width), lambda i: (blk(i), 0))
    modspec = lambda col: pl.BlockSpec((nb, 1, d), lambda i: (blk(i) // nt, 0, col))
    full = lambda a: pl.BlockSpec(a.shape, lambda i: (0, 0))
    in_specs = [row(half), pl.BlockSpec((rows * R_TILES, LANES), lambda i: (blk(i), 0)), row(half), row(half),
                full(ln_w), full(ln_b),
                pl.BlockSpec((half, d), lambda i: (0, 0)),
                pl.BlockSpec((half, d), lambda i: (1, 0)),
                pl.BlockSpec((nb, tb, d), lambda i: (blk(i) // nt, blk(i) % nt, 0)),
                modspec(2), modspec(3), modspec(4),
                full(g2), full(wr_hi), full(wr_lo), full(b_router)]
    args = [yc, y_tiled, bonus, gate, ln_w, ln_b, w_out_bf16, w_out_bf16, x, mod, mod, mod, g2, wr_hi, wr_lo,
            b_router]
    if h2_tail is not None:
        in_specs.append(pl.BlockSpec((rows * D_TILES, LANES), lambda i: (jnp.maximum(i - n_main, 0), 0)))
        args.append(h2_tail)
    return pl.pallas_call(
        functools.partial(_outproj_kernel, n_main),
        grid=(n_main + n_tail,),
        in_specs=in_specs,
        out_specs=[row(d), pl.BlockSpec((rows * D_TILES, LANES), lambda i: (i, 0)), row(LANES), row(LANES)],
        out_shape=[jax.ShapeDtypeStruct((m, d), F32),
                   jax.ShapeDtypeStruct(((n_main + n_tail) * rows * D_TILES, LANES), F32),
                   jax.ShapeDtypeStruct((m, LANES), I32), jax.ShapeDtypeStruct((m, LANES), F32)],
        compiler_params=_params(("arbitrary",)),
        name="outproj",
    )(*args)


def _row_copy(src_hbm, src_row, dst_ref, dst_row, sem):
    src = pl.multiple_of(src_row * D_TILES, D_TILES)
    dst = pl.multiple_of(dst_row * GATHER_PITCH, SUBLANES)
    return pltpu.make_async_copy(src_hbm.at[pl.ds(src, D_TILES)], dst_ref.at[pl.ds(dst, D_TILES)], sem)


def _start_row_gather(index_of, src_hbm, dst_ref, sem, n):
    def body(r, c):
        _row_copy(src_hbm, index_of(r), dst_ref, r, sem).start(priority=GATHER_DMA_PRIORITY)
        return c
    lax.fori_loop(0, n, body, 0, unroll=8)


def _wait_row_gather(src_hbm, dst_ref, sem, n):
    def body(r, c):
        _row_copy(src_hbm, 0, dst_ref, r, sem).wait()
        return c
    lax.fori_loop(0, n, body, 0, unroll=8)


def _experts_kernel(te_ref, nu_ref, nxt_ref, wsl_ref, idx_ref, idxn_ref, h_hbm, wg_hbm, wu_hbm, wd_hbm, o_ref,
                    xbuf, wgf, wuf, wdf, wgb, wub, wdb, sems, wsems):
    i = pl.program_id(0)
    n_used = nu_ref[0]
    slot = i & 1

    def weight_copies(e, ws):
        return (pltpu.make_async_copy(wg_hbm.at[e], wgf.at[ws], wsems.at[ws]),
                pltpu.make_async_copy(wu_hbm.at[e], wuf.at[ws], wsems.at[ws]),
                pltpu.make_async_copy(wd_hbm.at[e], wdf.at[ws], wsems.at[ws]))

    @pl.when(i == 0)
    def _():
        for c in weight_copies(te_ref[0], wsl_ref[0]):
            c.start(priority=WEIGHT_DMA_PRIORITY)
        _start_row_gather(lambda r: idx_ref[0, r], h_hbm, xbuf.at[0], sems.at[0], EXPERT_TILE)

    @pl.when(i + 1 < n_used)
    def _():
        _start_row_gather(lambda r: idxn_ref[0, r], h_hbm, xbuf.at[1 - slot], sems.at[1 - slot], EXPERT_TILE)

    @pl.when(i < n_used)
    def _():
        new_expert = jnp.logical_or(i == 0, te_ref[i] != te_ref[jnp.maximum(i - 1, 0)])

        @pl.when(new_expert)
        def _():
            ws = wsl_ref[i]
            for c in weight_copies(te_ref[i], ws):
                c.wait()
            wgb[...] = wgf[ws].astype(BF16)
            wub[...] = wuf[ws].astype(BF16)
            wdb[...] = wdf[ws].astype(BF16)

            @pl.when(nxt_ref[i] >= 0)
            def _():
                for c in weight_copies(nxt_ref[i], 1 - ws):
                    c.start(priority=WEIGHT_DMA_PRIORITY)

        _wait_row_gather(h_hbm, xbuf.at[slot], sems.at[slot], EXPERT_TILE)
        x = _load_row_tiled(xbuf.at[slot], EXPERT_TILE, D_TILES, GATHER_PITCH).astype(BF16)
        gate = jnp.dot(x, wgb[...], preferred_element_type=F32)
        up = jnp.dot(x, wub[...], preferred_element_type=F32)
        act = (gate * _sigmoid(gate) * up).astype(BF16)
        _store_row_tiled(o_ref, jnp.dot(act, wdb[...], preferred_element_type=F32), D_TILES)

    @pl.when(i >= n_used)
    def _():
        o_ref[...] = jnp.zeros_like(o_ref)


def _experts(plan, h2_tiled, w_gate, w_up, w_down):
    tile_expert, n_used, next_expert, weight_slot, src_rows = plan
    n_tiles = tile_expert.shape[0]
    d = w_gate.shape[1]
    idx3 = src_rows.reshape(n_tiles, 1, EXPERT_TILE)
    smem_idx = lambda f: pl.BlockSpec((None, 1, EXPERT_TILE), f, memory_space=pltpu.SMEM)
    hbm = pl.BlockSpec(memory_space=pl.ANY)
    grid_spec = pltpu.PrefetchScalarGridSpec(
        num_scalar_prefetch=4,
        grid=(n_tiles,),
        in_specs=[smem_idx(lambda i, *_: (i, 0, 0)),
                  smem_idx(lambda i, *_: (jnp.minimum(i + 1, n_tiles - 1), 0, 0)),
                  hbm, hbm, hbm, hbm],
        out_specs=pl.BlockSpec((EXPERT_TILE * D_TILES, LANES), lambda i, *_: (i, 0)),
        scratch_shapes=[pltpu.VMEM((2, EXPERT_TILE * GATHER_PITCH, LANES), F32),
                        pltpu.VMEM((2, d, D_EXPERT), F32), pltpu.VMEM((2, d, D_EXPERT), F32),
                        pltpu.VMEM((2, D_EXPERT, d), F32),
                        pltpu.VMEM((d, D_EXPERT), BF16), pltpu.VMEM((d, D_EXPERT), BF16),
                        pltpu.VMEM((D_EXPERT, d), BF16),
                        pltpu.SemaphoreType.DMA((2,)), pltpu.SemaphoreType.DMA((2,))],
    )
    return pl.pallas_call(
        _experts_kernel,
        grid_spec=grid_spec,
        out_shape=jax.ShapeDtypeStruct((n_tiles * EXPERT_TILE * D_TILES, LANES), F32),
        compiler_params=_params(("arbitrary",)),
        name="experts",
    )(tile_expert, n_used, next_expert, weight_slot, idx3, idx3, h2_tiled, w_gate, w_up, w_down)


def _combine_kernel(idx_ref, idxn_ref, wt_ref, x1_ref, ga2_ref, gf_ref, ys_hbm, o_ref, ybuf, sems):
    i = pl.program_id(0)
    n_steps = pl.num_programs(0)
    slot = i & 1
    nb, tb, d = o_ref.shape
    rows = nb * tb

    def fetch(idx, s):
        for j in range(2):
            _start_row_gather(lambda r: idx[0, 2 * r + j], ys_hbm, ybuf.at[s, j], sems.at[s], rows)

    @pl.when(i == 0)
    def _():
        fetch(idx_ref, 0)

    @pl.when(i + 1 < n_steps)
    def _():
        fetch(idxn_ref, 1 - slot)

    for j in range(2):
        _wait_row_gather(ys_hbm, ybuf.at[slot, j], sems.at[slot], rows)

    wt = wt_ref[...]
    moe = (wt[:, 0:1] * _load_row_tiled(ybuf.at[slot, 0], rows, D_TILES, GATHER_PITCH)
           + wt[:, 1:2] * _load_row_tiled(ybuf.at[slot, 1], rows, D_TILES, GATHER_PITCH))
    x2 = x1_ref[...].reshape(nb, tb, d) + ga2_ref[...] * moe.reshape(nb, tb, d)
    ms = jnp.mean(x2 * x2, axis=-1, keepdims=True)
    o_ref[...] = x2 * lax.rsqrt(ms + RMS_EPS) * gf_ref[...]


def _combine(dest, wts, x1, mod, final_g, ys, b, t, nb, tb):
    d = x1.shape[1]
    nt = t // tb
    rows = nb * tb
    n_steps = (b // nb) * nt
    idx3 = dest.reshape(n_steps, 1, 2 * rows)
    smem_idx = lambda f: pl.BlockSpec((None, 1, 2 * rows), f, memory_space=pltpu.SMEM)
    return pl.pallas_call(
        _combine_kernel,
        grid=(n_steps,),
        in_specs=[smem_idx(lambda i: (i, 0, 0)),
                  smem_idx(lambda i: (jnp.minimum(i + 1, n_steps - 1), 0, 0)),
                  pl.BlockSpec((rows, LANES), lambda i: (i, 0)),
                  pl.BlockSpec((rows, d), lambda i: (i, 0)),
                  pl.BlockSpec((nb, 1, d), lambda i: (i // nt, 0, 5)),
                  pl.BlockSpec((1, d), lambda i: (0, 0)),
                  pl.BlockSpec(memory_space=pl.ANY)],
        out_specs=pl.BlockSpec((nb, tb, d), lambda i: (i // nt, i % nt, 0)),
        out_shape=jax.ShapeDtypeStruct((b, t, d), F32),
        scratch_shapes=[pltpu.VMEM((2, 2, rows * GATHER_PITCH, LANES), F32), pltpu.SemaphoreType.DMA((2,))],
        compiler_params=_params(("arbitrary",)),
        name="combine",
    )(idx3, idx3, wts, x1, mod, final_g, ys)


def _dispatch_plan(eid):
    n_pairs = eid.shape[0] * 2
    n_tiles = n_pairs // EXPERT_TILE + N_EXPERTS
    flat = eid.reshape(-1)
    onehot = (flat[:, None] == jnp.arange(N_EXPERTS, dtype=I32)[None, :]).astype(I32)
    csum = jnp.cumsum(onehot, axis=0)
    rank = jnp.sum((csum - onehot) * onehot, axis=1)
    counts = csum[-1]
    tiles_per = (counts + EXPERT_TILE - 1) // EXPERT_TILE
    tile_end = jnp.cumsum(tiles_per)
    row_start = (tile_end - tiles_per) * EXPERT_TILE
    dest = jnp.sum(onehot * row_start[None, :], axis=1) + rank
    n_used = tile_end[-1]
    tile_ids = jnp.arange(n_tiles, dtype=I32)
    tile_expert = jnp.sum((tile_ids[:, None] >= tile_end[None, :]).astype(I32), axis=1)
    last_expert = jnp.max(jnp.where(counts > 0, jnp.arange(N_EXPERTS, dtype=I32), 0))
    tile_expert = jnp.where(tile_ids < n_used, tile_expert, last_expert).astype(I32)
    token = jnp.arange(n_pairs, dtype=I32) // 2
    filler = jnp.arange(n_tiles * EXPERT_TILE, dtype=I32) % (n_pairs // 2)
    src_rows = filler.at[dest].set(token)
    experts = jnp.arange(N_EXPERTS, dtype=I32)
    used = counts > 0
    later_used = used[None, :] & (experts[None, :] > experts[:, None])
    next_used = jnp.min(jnp.where(later_used, experts[None, :], N_EXPERTS), axis=1)
    next_used = jnp.where(next_used < N_EXPERTS, next_used, -1).astype(I32)
    slot_of = ((jnp.cumsum(used.astype(I32)) - 1) & 1).astype(I32)
    plan = (tile_expert, n_used.reshape(1).astype(I32), next_used[tile_expert], slot_of[tile_expert], src_rows)
    return dest.astype(I32), plan


def _stream_mix(x, mod, conv_state, shift_state, wkv_state, lp, nb, tb, tt, h2_tail=None):
    b, t, d = x.shape
    g = b // WKV_BATCHES
    h = _norm1(x, mod, lp["norm1_g"], nb, tb)
    proj = _inproj(h, lp["w_in"])
    outs = _mixprep(proj.reshape(b, t, D_IN_PROJ), conv_state, shift_state, lp, nb, tb)
    operands = [z.reshape(b, t * R_TILES, LANES) for z in outs[:N_WKV_OPERANDS]]
    yc, gate, bonus, new_conv, new_shift = outs[N_WKV_OPERANDS:]
    y_tiled, s_chain = _wkv(operands, _state_to_chain(wkv_state, g), tt)
    new_wkv = _state_from_chain(s_chain, g)
    x1, h2_tiled, eid, wts = _outproj(yc, y_tiled.reshape(b * t * R_TILES, LANES), bonus, gate, lp["ln_x_w"],
                                      lp["ln_x_b"], lp["w_out_bf16"], x, mod, lp["norm2_g"], lp["wr_hi"],
                                      lp["wr_lo"], lp["b_router"], nb, tb, h2_tail)
    return x1, h2_tiled, eid, wts, new_conv, new_shift.reshape(b, D_RWKV_PROJ), new_wkv


def kernel(x_prompt, x_sample, c_prompt, c_sample, state_conv, state_shift, state_wkv, w_ada, b_ada, norm1_g, w_in, conv_w, mix_mu, w0, w_decay_up, a0, w_aaa_up, w_gate_up, k_k, k_a, r_k, ln_x_w, ln_x_b, w_out, norm2_g, w_router_group, b_router_group, w_router_expert, b_router_expert, w_gate, w_up, w_down, final_g):
    depth = w_ada.shape[0]
    assert depth == 1
    bp, tp, d = x_prompt.shape
    bs, ts, _ = x_sample.shape
    l = 0

    row = lambda z: z.reshape(1, -1)
    zeros_lora = jnp.zeros((LORA_DECAY, D_RWKV), F32)
    n_route = N_GROUPS + N_EXPERTS
    w_router = jnp.pad(jnp.concatenate([w_router_group[l], w_router_expert[l]], axis=1),
                       ((0, 0), (0, LANES - n_route)))
    b_router = jnp.concatenate([b_router_group[l], b_router_expert[l]])
    wr_hi = w_router.astype(BF16)
    lp = dict(
        norm1_g=row(norm1_g[l]), w_in=w_in[l], conv_w=conv_w[l], mix_mu=row(mix_mu[l]), w0=row(w0[l]),
        wd_pad=jnp.concatenate([w_decay_up[l], zeros_lora], axis=0), a0=row(a0[l]),
        wa_pad=jnp.concatenate([zeros_lora, w_aaa_up[l]], axis=0), w_gate_up=w_gate_up[l],
        k_k=row(k_k[l]), k_a=row(k_a[l]), r_k=row(r_k[l]), ln_x_w=row(ln_x_w[l]), ln_x_b=row(ln_x_b[l]),
        w_out_bf16=_cast_bf16(w_out[l]), norm2_g=row(norm2_g[l]),
        wr_hi=wr_hi, wr_lo=(w_router - wr_hi.astype(F32)).astype(BF16),
        b_router=jnp.pad(b_router, (0, LANES - n_route)).reshape(1, LANES),
    )

    n_c = bp + bs
    c_all = jnp.concatenate([c_prompt, c_sample], axis=0)
    c_all = jnp.pad(c_all, ((0, (-n_c) % SUBLANES), (0, 0)))
    mod = _ada(c_all, w_ada[l], row(b_ada[l]))
    mod_p = mod[:bp].reshape(bp, 1, N_MOD * d)
    mod_s = mod[bp:n_c].reshape(bs, 1, N_MOD * d)

    zc = jnp.zeros((bp, 2, D_CONV), F32)
    zs = jnp.zeros((bp, 1, D_RWKV_PROJ), F32)
    zw = jnp.zeros((bp, N_HEADS, HEAD_DIM, HEAD_DIM), F32)
    n_p = bp * tp
    x1_s, h2_s, eid_s, wts_s, conv_s, shift_s, wkv_s = _stream_mix(
        x_sample, mod_s, state_conv[l], state_shift[l].reshape(bs, 1, D_RWKV_PROJ), state_wkv[l], lp,
        nb=ROW_TILE // ts, tb=ts, tt=ts)
    x1_p, h2, eid_p, wts_p, conv_p, shift_p, wkv_p = _stream_mix(
        x_prompt, mod_p, zc, zs, zw, lp, nb=1, tb=ROW_TILE, tt=64, h2_tail=h2_s)

    eid = jnp.concatenate([eid_p[:, :2], eid_s[:, :2]], axis=0)
    dest, plan = _dispatch_plan(eid)
    ys = _experts(plan, h2, w_gate[l], w_up[l], w_down[l])
    y_prompt = _combine(dest[:2 * n_p], wts_p, x1_p, mod_p, row(final_g), ys, bp, tp, nb=1, tb=ROW_TILE)
    y_sample = _combine(dest[2 * n_p:], wts_s, x1_s, mod_s, row(final_g), ys, bs, ts, nb=ROW_TILE // ts, tb=ts)

    return (y_prompt, y_sample, conv_p[None], shift_p[None], wkv_p[None],
            conv_s[None], shift_s[None], wkv_s[None])
```

```python
import functools

import jax
import jax.numpy as jnp
from jax import lax
from jax.experimental import pallas as pl
from jax.experimental.pallas import tpu as pltpu

F32 = jnp.float32
BF16 = jnp.bfloat16
I32 = jnp.int32

D_MODEL = 2048
D_CONV = 1024
D_RWKV = 1024
HEAD_DIM = 64
N_HEADS = 16
LORA_DECAY = 64
LORA_AAA = 64
LORA_GATE = 128
D_RWKV_PROJ = 3 * D_RWKV + LORA_DECAY + LORA_AAA + LORA_GATE
D_IN_PROJ = 3 * D_CONV + D_RWKV_PROJ
N_GROUPS = 4
EXPERTS_PER_GROUP = 8
N_EXPERTS = 32
D_EXPERT = 512
N_MOD = 6
RMS_EPS = 1e-6
GN_EPS = HEAD_DIM * 1e-5

SUBLANES = 8
LANES = 128
VMEM_LIMIT_BYTES = 56 * 1024 * 1024

ROW_TILE = 256
EXPERT_TILE = 256
SEG = 256
D_TILES = D_MODEL // LANES
R_TILES = D_RWKV // LANES
GATHER_PITCH = D_TILES + SUBLANES
GATHER_DMA_PRIORITY = 0
WEIGHT_DMA_PRIORITY = 1

WKV_BATCHES = 4
V_LO = LANES // (WKV_BATCHES * R_TILES)
V_HI = HEAD_DIM // V_LO
N_WKV_OPERANDS = 6


def _params(semantics):
    return pltpu.CompilerParams(dimension_semantics=semantics, vmem_limit_bytes=VMEM_LIMIT_BYTES)


def _sigmoid(x):
    return 1.0 / (1.0 + jnp.exp(-x))


def _store_row_tiled(ref, x, pitch):
    rows = x.shape[0]
    for s in range(x.shape[1] // LANES):
        ref[pl.ds(s, rows, stride=pitch), :] = x[:, s * LANES:(s + 1) * LANES]


def _load_row_tiled(ref, rows, n, pitch):
    return jnp.concatenate([ref[pl.ds(s, rows, stride=pitch), :] for s in range(n)], axis=-1)


def _head_ones():
    r = lax.broadcasted_iota(I32, (SEG, SEG), 0) // HEAD_DIM
    c = lax.broadcasted_iota(I32, (SEG, SEG), 1) // HEAD_DIM
    return jnp.where(r == c, 1.0, 0.0).astype(BF16)


def _split_bf16(x):
    hi = x.astype(BF16)
    return hi, (x - hi.astype(F32)).astype(BF16)


def _head_sum(x, ones):
    outs = []
    for s in range(x.shape[-1] // SEG):
        hi, lo = _split_bf16(x[:, s * SEG:(s + 1) * SEG])
        outs.append(jnp.dot(hi, ones, preferred_element_type=F32) + jnp.dot(lo, ones, preferred_element_type=F32))
    return jnp.concatenate(outs, axis=-1)


def _ada_kernel(c_ref, w_ref, b_ref, o_ref):
    c = c_ref[...]
    s = c * _sigmoid(c)
    o_ref[...] = jnp.dot(s.astype(BF16), w_ref[...].astype(BF16), preferred_element_type=F32) + b_ref[...]


def _ada(c_all, w_ada, b_ada):
    mb = c_all.shape[0]
    n = w_ada.shape[1]
    tn = 1024
    return pl.pallas_call(
        _ada_kernel,
        grid=(n // tn,),
        in_specs=[pl.BlockSpec((mb, D_MODEL), lambda j: (0, 0)),
                  pl.BlockSpec((D_MODEL, tn), lambda j: (0, j)),
                  pl.BlockSpec((1, tn), lambda j: (0, j))],
        out_specs=pl.BlockSpec((mb, tn), lambda j: (0, j)),
        out_shape=jax.ShapeDtypeStruct((mb, n), F32),
        compiler_params=_params(("arbitrary",)),
        name="ada",
    )(c_all, w_ada, b_ada)


def _norm1_kernel(x_ref, g_ref, sh_ref, sc_ref, o_ref):
    x = x_ref[...]
    nb, tb, d = x.shape
    ms = jnp.mean(x * x, axis=-1, keepdims=True)
    xn = x * lax.rsqrt(ms + RMS_EPS) * g_ref[...]
    h = xn * (1.0 + sc_ref[...]) + sh_ref[...]
    o_ref[...] = h.reshape(nb * tb, d).astype(BF16)


def _norm1(x, mod, g, nb, tb):
    b, t, d = x.shape
    nt = t // tb
    return pl.pallas_call(
        _norm1_kernel,
        grid=(b // nb, nt),
        in_specs=[pl.BlockSpec((nb, tb, d), lambda i, j: (i, j, 0)),
                  pl.BlockSpec((1, d), lambda i, j: (0, 0)),
                  pl.BlockSpec((nb, 1, d), lambda i, j: (i, 0, 0)),
                  pl.BlockSpec((nb, 1, d), lambda i, j: (i, 0, 1))],
        out_specs=pl.BlockSpec((nb * tb, d), lambda i, j: (i * nt + j, 0)),
        out_shape=jax.ShapeDtypeStruct((b * t, d), BF16),
        compiler_params=_params(("arbitrary", "arbitrary")),
        name="norm1",
    )(x, g, mod, mod)


def _inproj_kernel(h_ref, w_ref, o_ref, wb_ref):
    @pl.when(pl.program_id(1) == 0)
    def _():
        wb_ref[...] = w_ref[...].astype(BF16)

    o_ref[...] = jnp.dot(h_ref[...], wb_ref[...], preferred_element_type=F32)


def _inproj(h, w_in):
    m, k = h.shape
    n = w_in.shape[1]
    tm = min(m, 1024)
    tn = 1280
    return pl.pallas_call(
        _inproj_kernel,
        grid=(n // tn, m // tm),
        in_specs=[pl.BlockSpec((tm, k), lambda j, i: (i, 0)),
                  pl.BlockSpec((k, tn), lambda j, i: (0, j))],
        out_specs=pl.BlockSpec((tm, tn), lambda j, i: (i, j)),
        out_shape=jax.ShapeDtypeStruct((m, n), F32),
        scratch_shapes=[pltpu.VMEM((k, tn), BF16)],
        compiler_params=_params(("arbitrary", "arbitrary")),
        name="inproj",
    )(h, w_in)


def _mixprep_kernel(proj_ref, cst_ref, sst_ref, convw_ref, mu_ref, w0_ref, wd_ref, a0_ref, wa_ref, wg_ref,
                    kk_ref, ka_ref, rk_ref,
                    a_o, w_o, b_o, k_o, r_o, v_o, yc_o, g_o, bonus_o, nc_o, ns_o):
    @pl.when(pl.program_id(1) == 0)
    def _():
        nc_o[...] = cst_ref[...]
        ns_o[...] = sst_ref[...]

    nb, tb, _ = proj_ref.shape
    rows = nb * tb
    p = proj_ref[...].reshape(rows, D_IN_PROJ)
    tidx = lax.broadcasted_iota(I32, (rows, 1), 0) & (tb - 1)

    def per_row(c):
        return jnp.broadcast_to(c, (nb, tb, c.shape[-1])).reshape(rows, c.shape[-1])

    gate_b = p[:, 0:D_CONV]
    gate_c = p[:, D_CONV:2 * D_CONV]
    h_conv = p[:, 2 * D_CONV:3 * D_CONV]
    pr = p[:, 3 * D_CONV:]

    u = gate_c * h_conv
    carry = nc_o[...]
    c2 = per_row(carry[:, 0:1, :])
    c1 = per_row(carry[:, 1:2, :])
    u1 = jnp.where(tidx == 0, c1, pltpu.roll(u, 1, 0))
    u2 = jnp.where(tidx == 0, c2, jnp.where(tidx == 1, c1, pltpu.roll(u, 2, 0)))
    cw = convw_ref[...]
    conv_y = u * cw[2:3, :] + u2 * cw[0:1, :] + u1 * cw[1:2, :]
    yc_o[...] = (gate_b * conv_y).astype(BF16)
    nc_o[...] = u.reshape(nb, tb, D_CONV)[:, tb - 2:tb, :]

    p_prev = jnp.where(tidx == 0, per_row(ns_o[...]), pltpu.roll(pr, 1, 0))
    ns_o[...] = pr.reshape(nb, tb, D_RWKV_PROJ)[:, tb - 1:tb, :]
    pm = pr + (p_prev - pr) * mu_ref[...]
    r = pm[:, 0:D_RWKV]
    k = pm[:, D_RWKV:2 * D_RWKV]
    v = pm[:, 2 * D_RWKV:3 * D_RWKV]
    x_lora = pm[:, 3 * D_RWKV:3 * D_RWKV + LORA_DECAY + LORA_AAA]
    xg = pm[:, 3 * D_RWKV + LORA_DECAY + LORA_AAA:]

    d_pre = w0_ref[...] + jnp.dot(jnp.tanh(x_lora).astype(BF16), wd_ref[...].astype(BF16),
                                  preferred_element_type=F32)
    a_pre = a0_ref[...] + jnp.dot(x_lora.astype(BF16), wa_ref[...].astype(BF16), preferred_element_type=F32)
    g = jnp.dot(_sigmoid(xg).astype(BF16), wg_ref[...].astype(BF16), preferred_element_type=F32)
    z = -d_pre
    softplus = jnp.maximum(z, 0.0) + jnp.log(1.0 + jnp.exp(-jnp.abs(z)))
    w_log = -softplus - 0.5
    decay = jnp.exp(-jnp.exp(w_log))
    a_h = _sigmoid(a_pre)

    ones = _head_ones()
    kk = k * kk_ref[...]
    kk = kk * lax.rsqrt(jnp.maximum(_head_sum(kk * kk, ones), 1e-24))
    k_h = k * (1.0 + (a_h - 1.0) * ka_ref[...])
    bonus = _head_sum(r * k_h * rk_ref[...], ones) * v

    _store_row_tiled(a_o, -kk, R_TILES)
    _store_row_tiled(w_o, decay, R_TILES)
    _store_row_tiled(b_o, kk * a_h, R_TILES)
    _store_row_tiled(k_o, k_h, R_TILES)
    _store_row_tiled(r_o, r, R_TILES)
    _store_row_tiled(v_o, v, R_TILES)
    g_o[...] = g
    bonus_o[...] = bonus


def _mixprep(proj3, conv_state, shift_state, lp, nb, tb):
    b, t, _ = proj3.shape
    nt = t // tb
    rows = nb * tb
    m = b * t

    def row_spec(width):
        return pl.BlockSpec((rows, width), lambda i, j: (i * nt + j, 0))

    def full2(a):
        return pl.BlockSpec(a.shape, lambda i, j: (0, 0))

    consts = (lp["conv_w"], lp["mix_mu"], lp["w0"], lp["wd_pad"], lp["a0"], lp["wa_pad"], lp["w_gate_up"],
              lp["k_k"], lp["k_a"], lp["r_k"])
    tiled_spec = pl.BlockSpec((rows * R_TILES, LANES), lambda i, j: (i * nt + j, 0))
    tiled_out = jax.ShapeDtypeStruct((m * R_TILES, LANES), F32)
    f32_out = jax.ShapeDtypeStruct((m, D_RWKV), F32)
    return pl.pallas_call(
        _mixprep_kernel,
        grid=(b // nb, nt),
        in_specs=[pl.BlockSpec((nb, tb, D_IN_PROJ), lambda i, j: (i, j, 0)),
                  pl.BlockSpec((nb, 2, D_CONV), lambda i, j: (i, 0, 0)),
                  pl.BlockSpec((nb, 1, D_RWKV_PROJ), lambda i, j: (i, 0, 0))] + [full2(c) for c in consts],
        out_specs=[tiled_spec] * N_WKV_OPERANDS + [row_spec(D_CONV), row_spec(D_RWKV), row_spec(D_RWKV),
                                                   pl.BlockSpec((nb, 2, D_CONV), lambda i, j: (i, 0, 0)),
                                                   pl.BlockSpec((nb, 1, D_RWKV_PROJ), lambda i, j: (i, 0, 0))],
        out_shape=[tiled_out] * N_WKV_OPERANDS + [jax.ShapeDtypeStruct((m, D_CONV), BF16), f32_out, f32_out,
                                                  jax.ShapeDtypeStruct((b, 2, D_CONV), F32),
                                                  jax.ShapeDtypeStruct((b, 1, D_RWKV_PROJ), F32)],
        compiler_params=_params(("arbitrary", "arbitrary")),
        name="mixprep",
    )(proj3, conv_state, shift_state, *consts)


def _wkv_kernel(a_ref, w_ref, b_ref, k_ref, r_ref, v_ref, s0_ref, y_ref, s_ref, tiles, yq):
    @pl.when(pl.program_id(1) == 0)
    def _():
        s_ref[...] = s0_ref[...]

    tt = a_ref.shape[1] // R_TILES
    srcs = (a_ref, w_ref, b_ref, k_ref, r_ref, v_ref)
    n_chain = WKV_BATCHES * R_TILES
    sub8 = lax.broadcasted_iota(I32, (SUBLANES, LANES), 0)
    grp8 = lax.broadcasted_iota(I32, (SUBLANES, LANES), 1) // n_chain
    diag8 = (sub8 % V_LO) == grp8
    sub4 = lax.broadcasted_iota(I32, (V_LO, LANES), 0)
    grp4 = lax.broadcasted_iota(I32, (V_LO, LANES), 1) // n_chain
    diag4 = sub4 == grp4

    def produce(t, slot):
        row0 = pl.multiple_of(t * R_TILES, R_TILES)
        for ai, ref in enumerate(srcs):
            m = jnp.concatenate([ref[bb, pl.ds(row0, R_TILES), :] for bb in range(WKV_BATCHES)], axis=0)
            xt = jnp.concatenate([m] * V_LO, axis=0).T
            if ai < N_WKV_OPERANDS - 1:
                tiles[slot, ai] = xt
            else:
                for q in range(LANES // SUBLANES):
                    z = jnp.where(diag8, xt[q * SUBLANES:(q + 1) * SUBLANES], 0.0)
                    z = z + pltpu.roll(z, 1, 0)
                    z = z + pltpu.roll(z, 2, 0)
                    tiles[slot, ai, q * SUBLANES:(q + 1) * SUBLANES, :] = z

    def consume(slot):
        for par in range(2):
            lo = par * HEAD_DIM
            a = tiles[slot, 0, lo:lo + HEAD_DIM, :]
            w = tiles[slot, 1, lo:lo + HEAD_DIM, :]
            b = tiles[slot, 2, lo:lo + HEAD_DIM, :]
            k = tiles[slot, 3, lo:lo + HEAD_DIM, :]
            r = tiles[slot, 4, lo:lo + HEAD_DIM, :]
            for vh in range(V_HI):
                s = s_ref[0, par, vh]
                vrow = tiles[slot, 5, lo + vh * V_LO + V_LO - 1:lo + (vh + 1) * V_LO, :]
                sa = jnp.sum(s * a, axis=0, keepdims=True)
                s = s * w + sa * b + vrow * k
                s_ref[0, par, vh] = s
                yrow = jnp.sum(s * r, axis=0, keepdims=True)
                yq[slot, lo + vh * V_LO:lo + (vh + 1) * V_LO, :] = jnp.where(
                    diag4, jnp.broadcast_to(yrow, (V_LO, LANES)), 0.0)

    def finalize(t, slot):
        qt = yq[slot].T
        res = qt[0:n_chain]
        for j in range(1, V_LO):
            res = res + qt[j * n_chain:(j + 1) * n_chain]
        row0 = pl.multiple_of(t * R_TILES, R_TILES)
        for bb in range(WKV_BATCHES):
            y_ref[bb, pl.ds(row0, R_TILES), :] = res[bb * R_TILES:(bb + 1) * R_TILES]

    produce(0, 0)
    produce(1, 1)
    consume(0)

    def body(i, c):
        t1 = 2 * i + 1
        produce(t1 + 1, 0)
        consume(1)
        finalize(t1 - 1, 0)
        produce(jnp.minimum(t1 + 2, tt - 1), 1)
        consume(0)
        finalize(t1, 1)
        return c

    lax.fori_loop(0, tt // 2 - 1, body, 0)
    consume(1)
    finalize(tt - 2, 0)
    finalize(tt - 1, 1)


def _wkv(operands, s0, tt):
    bsz, rows, _ = operands[0].shape
    t = rows // R_TILES
    g = bsz // WKV_BATCHES
    assert tt % 2 == 0 and tt >= 4 and t % tt == 0
    blk = pl.BlockSpec((WKV_BATCHES, tt * R_TILES, LANES), lambda i, j: (i, j, 0))
    state = pl.BlockSpec((1, 2, V_HI, HEAD_DIM, LANES), lambda i, j: (i, 0, 0, 0, 0))
    return pl.pallas_call(
        _wkv_kernel,
        grid=(g, t // tt),
        in_specs=[blk] * N_WKV_OPERANDS + [state],
        out_specs=[blk, state],
        out_shape=[jax.ShapeDtypeStruct((bsz, rows, LANES), F32),
                   jax.ShapeDtypeStruct((g, 2, V_HI, HEAD_DIM, LANES), F32)],
        scratch_shapes=[pltpu.VMEM((2, N_WKV_OPERANDS, LANES, LANES), F32), pltpu.VMEM((2, LANES, LANES), F32)],
        compiler_params=_params(("arbitrary", "arbitrary")),
        name="wkv",
    )(*operands, s0)


def _state_to_chain(s, g):
    s = s.reshape(g, WKV_BATCHES, R_TILES, 2, V_HI, V_LO, HEAD_DIM).transpose(0, 3, 4, 6, 5, 1, 2)
    return s.reshape(g, 2, V_HI, HEAD_DIM, LANES)


def _state_from_chain(s, g):
    s = s.reshape(g, 2, V_HI, HEAD_DIM, V_LO, WKV_BATCHES, R_TILES).transpose(0, 5, 6, 1, 2, 4, 3)
    return s.reshape(g * WKV_BATCHES, N_HEADS, HEAD_DIM, HEAD_DIM)


def _group_norm_gate(y, bonus, g, ln_w, ln_b):
    ones = _head_ones()
    mu = _head_sum(y, ones) * (1.0 / HEAD_DIM)
    d = y - mu
    var = _head_sum(d * d, ones) * (1.0 / HEAD_DIM)
    yn = d * lax.rsqrt(var + GN_EPS)
    yn = yn * ln_w + ln_b
    return (yn + bonus) * g


def _cast_kernel(x_ref, o_ref):
    o_ref[...] = x_ref[...].astype(o_ref.dtype)


def _cast_bf16(w):
    r, c = w.shape
    tr = 512
    return pl.pallas_call(
        _cast_kernel,
        grid=(r // tr,),
        in_specs=[pl.BlockSpec((tr, c), lambda i: (i, 0))],
        out_specs=pl.BlockSpec((tr, c), lambda i: (i, 0)),
        out_shape=jax.ShapeDtypeStruct((r, c), BF16),
        compiler_params=_params(("arbitrary",)),
        name="cast_bf16",
    )(w)


def _route(lg):
    lane = lax.broadcasted_iota(I32, lg.shape, 1)
    neg = -jnp.inf
    big = jnp.int32(1 << 20)

    is_g = lane < N_GROUPS
    gl = jnp.where(is_g, lg, neg)
    gmax = jnp.max(gl, axis=1, keepdims=True)
    gidx = jnp.min(jnp.where(gl == gmax, lane, big), axis=1, keepdims=True)
    gsum = jnp.sum(jnp.where(is_g, jnp.exp(gl - gmax), 0.0), axis=1, keepdims=True)
    gprob = 1.0 / gsum

    e_lane = lane - N_GROUPS
    in_grp = (e_lane >= 0) & (e_lane < N_EXPERTS) & ((e_lane >> 3) == gidx)
    el = jnp.where(in_grp, lg, neg)
    emax = jnp.max(el, axis=1, keepdims=True)
    ex = jnp.where(in_grp, jnp.exp(el - emax), 0.0)
    prob = ex / jnp.sum(ex, axis=1, keepdims=True)
    prob = jnp.where(in_grp, prob, -1.0)
    p1 = jnp.max(prob, axis=1, keepdims=True)
    i1 = jnp.min(jnp.where(prob == p1, lane, big), axis=1, keepdims=True)
    prob2 = jnp.where(lane == i1, -1.0, prob)
    p2 = jnp.max(prob2, axis=1, keepdims=True)
    i2 = jnp.min(jnp.where(prob2 == p2, lane, big), axis=1, keepdims=True)
    psum = p1 + p2
    w1 = p1 / psum * gprob
    w2 = p2 / psum * gprob
    eid = jnp.where(lane == 0, i1 - N_GROUPS, jnp.where(lane == 1, i2 - N_GROUPS, 0))
    wts = jnp.where(lane == 0, w1, jnp.where(lane == 1, w2, 0.0))
    return eid, wts


def _outproj_kernel(n_main, yc_ref, y_ref, bonus_ref, gate_ref, lnw_ref, lnb_ref, wt_ref, wb_ref, x_ref,
                    ga1_ref, sh2_ref, sc2_ref, g2_ref, wrh_ref, wrl_ref, br_ref, *rest):
    x1_o, h2_o, eid_o, wts_o = rest[-4:]
    i = pl.program_id(0)

    @pl.when(i < n_main)
    def _():
        nb, tb, d = x_ref.shape
        rows = nb * tb
        y = _load_row_tiled(y_ref, rows, R_TILES, R_TILES)
        yr = _group_norm_gate(y, bonus_ref[...], gate_ref[...], lnw_ref[...], lnb_ref[...]).astype(BF16)
        mix = (jnp.dot(yc_ref[...], wt_ref[...], preferred_element_type=F32)
               + jnp.dot(yr, wb_ref[...], preferred_element_type=F32))
        x1 = x_ref[...] + ga1_ref[...] * mix.reshape(nb, tb, d)
        ms = jnp.mean(x1 * x1, axis=-1, keepdims=True)
        h2 = x1 * lax.rsqrt(ms + RMS_EPS) * g2_ref[...]
        h2 = (h2 * (1.0 + sc2_ref[...]) + sh2_ref[...]).reshape(rows, d)
        x1_o[...] = x1.reshape(rows, d)
        _store_row_tiled(h2_o, h2, D_TILES)
        h_hi, h_lo = _split_bf16(h2)
        w_hi = wrh_ref[...]
        logits = (jnp.dot(h_hi, w_hi, preferred_element_type=F32)
                  + jnp.dot(h_lo, w_hi, preferred_element_type=F32)
                  + jnp.dot(h_hi, wrl_ref[...], preferred_element_type=F32)) + br_ref[...]
        eid_o[...], wts_o[...] = _route(logits)

    if len(rest) == 5:
        @pl.when(i >= n_main)
        def _():
            h2_o[...] = rest[0][...]


def _outproj(yc, y_tiled, bonus, gate, ln_w, ln_b, w_out_bf16, x, mod, g2, wr_hi, wr_lo, b_router, nb, tb,
             h2_tail=None):
    b, t, d = x.shape
    nt = t // tb
    rows = nb * tb
    m = b * t
    half = D_CONV
    n_main = (b // nb) * nt
    n_tail = 0 if h2_tail is None else h2_tail.shape[0] // (rows * D_TILES)
    blk = lambda i: jnp.minimum(i, n_main - 1)
    row = lambda width: pl.BlockSpec((rows, width), lambda i: (blk(i), 0))
    modspec = lambda col: pl.BlockSpec((nb, 1, d), lambda i: (blk(i) // nt, 0, col))
    full = lambda a: pl.BlockSpec(a.shape, lambda i: (0, 0))
    in_specs = [row(half), pl.BlockSpec((rows * R_TILES, LANES), lambda i: (blk(i), 0)), row(half), row(half),
                full(ln_w), full(ln_b),
                pl.BlockSpec((half, d), lambda i: (0, 0)),
                pl.BlockSpec((half, d), lambda i: (1, 0)),
                pl.BlockSpec((nb, tb, d), lambda i: (blk(i) // nt, blk(i) % nt, 0)),
                modspec(2), modspec(3), modspec(4),
                full(g2), full(wr_hi), full(wr_lo), full(b_router)]
    args = [yc, y_tiled, bonus, gate, ln_w, ln_b, w_out_bf16, w_out_bf16, x, mod, mod, mod, g2, wr_hi, wr_lo,
            b_router]
    if h2_tail is not None:
        in_specs.append(pl.BlockSpec((rows * D_TILES, LANES), lambda i: (jnp.maximum(i - n_main, 0), 0)))
        args.append(h2_tail)
    return pl.pallas_call(
        functools.partial(_outproj_kernel, n_main),
        grid=(n_main + n_tail,),
        in_specs=in_specs,
        out_specs=[row(d), pl.BlockSpec((rows * D_TILES, LANES), lambda i: (i, 0)), row(LANES), row(LANES)],
        out_shape=[jax.ShapeDtypeStruct((m, d), F32),
                   jax.ShapeDtypeStruct(((n_main + n_tail) * rows * D_TILES, LANES), F32),
                   jax.ShapeDtypeStruct((m, LANES), I32), jax.ShapeDtypeStruct((m, LANES), F32)],
        compiler_params=_params(("arbitrary",)),
        name="outproj",
    )(*args)


def _row_copy(src_hbm, src_row, dst_ref, dst_row, sem):
    src = src_row * D_TILES
    dst = dst_row * GATHER_PITCH
    if not isinstance(src, int):
        src = pl.multiple_of(src, D_TILES)
    if not isinstance(dst, int):
        dst = pl.multiple_of(dst, SUBLANES)
    return pltpu.make_async_copy(src_hbm.at[pl.ds(src, D_TILES)], dst_ref.at[pl.ds(dst, D_TILES)], sem)


def _start_row_gather(index_of, src_hbm, dst_ref, sem, n):
    for r in range(n):
        _row_copy(src_hbm, index_of(r), dst_ref, r, sem).start(priority=GATHER_DMA_PRIORITY)


def _wait_row_gather(src_hbm, dst_ref, sem, n):
    def body(r, c):
        _row_copy(src_hbm, 0, dst_ref, r, sem).wait()
        return c
    lax.fori_loop(0, n, body, 0, unroll=8)


def _experts_kernel(te_ref, nu_ref, nxt_ref, wsl_ref, idx_ref, idxn_ref, h_hbm, wg_hbm, wu_hbm, wd_hbm, o_ref,
                    xbuf, wgf, wuf, wdf, wgb, wub, wdb, sems, wsems):
    i = pl.program_id(0)
    n_used = nu_ref[0]
    slot = i & 1

    def weight_copies(e, ws):
        return (pltpu.make_async_copy(wg_hbm.at[e], wgf.at[ws], wsems.at[ws]),
                pltpu.make_async_copy(wu_hbm.at[e], wuf.at[ws], wsems.at[ws]),
                pltpu.make_async_copy(wd_hbm.at[e], wdf.at[ws], wsems.at[ws]))

    @pl.when(i == 0)
    def _():
        for c in weight_copies(te_ref[0], wsl_ref[0]):
            c.start(priority=WEIGHT_DMA_PRIORITY)
        _start_row_gather(lambda r: idx_ref[0, r], h_hbm, xbuf.at[0], sems.at[0], EXPERT_TILE)

    @pl.when(i + 1 < n_used)
    def _():
        _start_row_gather(lambda r: idxn_ref[0, r], h_hbm, xbuf.at[1 - slot], sems.at[1 - slot], EXPERT_TILE)

    @pl.when(i < n_used)
    def _():
        new_expert = jnp.logical_or(i == 0, te_ref[i] != te_ref[jnp.maximum(i - 1, 0)])

        @pl.when(new_expert)
        def _():
            ws = wsl_ref[i]
            for c in weight_copies(te_ref[i], ws):
                c.wait()
            wgb[...] = wgf[ws].astype(BF16)
            wub[...] = wuf[ws].astype(BF16)
            wdb[...] = wdf[ws].astype(BF16)

            @pl.when(nxt_ref[i] >= 0)
            def _():
                for c in weight_copies(nxt_ref[i], 1 - ws):
                    c.start(priority=WEIGHT_DMA_PRIORITY)

        _wait_row_gather(h_hbm, xbuf.at[slot], sems.at[slot], EXPERT_TILE)
        x = _load_row_tiled(xbuf.at[slot], EXPERT_TILE, D_TILES, GATHER_PITCH).astype(BF16)
        gate = jnp.dot(x, wgb[...], preferred_element_type=F32)
        up = jnp.dot(x, wub[...], preferred_element_type=F32)
        act = (gate * _sigmoid(gate) * up).astype(BF16)
        _store_row_tiled(o_ref, jnp.dot(act, wdb[...], preferred_element_type=F32), D_TILES)

    @pl.when(i >= n_used)
    def _():
        o_ref[...] = jnp.zeros_like(o_ref)


def _experts(plan, h2_tiled, w_gate, w_up, w_down):
    tile_expert, n_used, next_expert, weight_slot, src_rows = plan
    n_tiles = tile_expert.shape[0]
    d = w_gate.shape[1]
    idx3 = src_rows.reshape(n_tiles, 1, EXPERT_TILE)
    smem_idx = lambda f: pl.BlockSpec((None, 1, EXPERT_TILE), f, memory_space=pltpu.SMEM)
    hbm = pl.BlockSpec(memory_space=pl.ANY)
    grid_spec = pltpu.PrefetchScalarGridSpec(
        num_scalar_prefetch=4,
        grid=(n_tiles,),
        in_specs=[smem_idx(lambda i, *_: (i, 0, 0)),
                  smem_idx(lambda i, *_: (jnp.minimum(i + 1, n_tiles - 1), 0, 0)),
                  hbm, hbm, hbm, hbm],
        out_specs=pl.BlockSpec((EXPERT_TILE * D_TILES, LANES), lambda i, *_: (i, 0)),
        scratch_shapes=[pltpu.VMEM((2, EXPERT_TILE * GATHER_PITCH, LANES), F32),
                        pltpu.VMEM((2, d, D_EXPERT), F32), pltpu.VMEM((2, d, D_EXPERT), F32),
                        pltpu.VMEM((2, D_EXPERT, d), F32),
                        pltpu.VMEM((d, D_EXPERT), BF16), pltpu.VMEM((d, D_EXPERT), BF16),
                        pltpu.VMEM((D_EXPERT, d), BF16),
                        pltpu.SemaphoreType.DMA((2,)), pltpu.SemaphoreType.DMA((2,))],
    )
    return pl.pallas_call(
        _experts_kernel,
        grid_spec=grid_spec,
        out_shape=jax.ShapeDtypeStruct((n_tiles * EXPERT_TILE * D_TILES, LANES), F32),
        compiler_params=_params(("arbitrary",)),
        name="experts",
    )(tile_expert, n_used, next_expert, weight_slot, idx3, idx3, h2_tiled, w_gate, w_up, w_down)


def _combine_kernel(idx_ref, idxn_ref, wt_ref, x1_ref, ga2_ref, gf_ref, ys_hbm, o_ref, ybuf, sems):
    i = pl.program_id(0)
    n_steps = pl.num_programs(0)
    slot = i & 1
    nb, tb, d = o_ref.shape
    rows = nb * tb

    def fetch(idx, s):
        for j in range(2):
            _start_row_gather(lambda r: idx[0, 2 * r + j], ys_hbm, ybuf.at[s, j], sems.at[s], rows)

    @pl.when(i == 0)
    def _():
        fetch(idx_ref, 0)

    @pl.when(i + 1 < n_steps)
    def _():
        fetch(idxn_ref, 1 - slot)

    for j in range(2):
        _wait_row_gather(ys_hbm, ybuf.at[slot, j], sems.at[slot], rows)

    wt = wt_ref[...]
    moe = (wt[:, 0:1] * _load_row_tiled(ybuf.at[slot, 0], rows, D_TILES, GATHER_PITCH)
           + wt[:, 1:2] * _load_row_tiled(ybuf.at[slot, 1], rows, D_TILES, GATHER_PITCH))
    x2 = x1_ref[...].reshape(nb, tb, d) + ga2_ref[...] * moe.reshape(nb, tb, d)
    ms = jnp.mean(x2 * x2, axis=-1, keepdims=True)
    o_ref[...] = x2 * lax.rsqrt(ms + RMS_EPS) * gf_ref[...]


def _combine(dest, wts, x1, mod, final_g, ys, b, t, nb, tb):
    d = x1.shape[1]
    nt = t // tb
    rows = nb * tb
    n_steps = (b // nb) * nt
    idx3 = dest.reshape(n_steps, 1, 2 * rows)
    smem_idx = lambda f: pl.BlockSpec((None, 1, 2 * rows), f, memory_space=pltpu.SMEM)
    return pl.pallas_call(
        _combine_kernel,
        grid=(n_steps,),
        in_specs=[smem_idx(lambda i: (i, 0, 0)),
                  smem_idx(lambda i: (jnp.minimum(i + 1, n_steps - 1), 0, 0)),
                  pl.BlockSpec((rows, LANES), lambda i: (i, 0)),
                  pl.BlockSpec((rows, d), lambda i: (i, 0)),
                  pl.BlockSpec((nb, 1, d), lambda i: (i // nt, 0, 5)),
                  pl.BlockSpec((1, d), lambda i: (0, 0)),
                  pl.BlockSpec(memory_space=pl.ANY)],
        out_specs=pl.BlockSpec((nb, tb, d), lambda i: (i // nt, i % nt, 0)),
        out_shape=jax.ShapeDtypeStruct((b, t, d), F32),
        scratch_shapes=[pltpu.VMEM((2, 2, rows * GATHER_PITCH, LANES), F32), pltpu.SemaphoreType.DMA((2,))],
        compiler_params=_params(("arbitrary",)),
        name="combine",
    )(idx3, idx3, wts, x1, mod, final_g, ys)


def _dispatch_plan(eid):
    n_pairs = eid.shape[0] * 2
    n_tiles = n_pairs // EXPERT_TILE + N_EXPERTS
    flat = eid.reshape(-1)
    onehot = (flat[:, None] == jnp.arange(N_EXPERTS, dtype=I32)[None, :]).astype(I32)
    csum = jnp.cumsum(onehot, axis=0)
    rank = jnp.sum((csum - onehot) * onehot, axis=1)
    counts = csum[-1]
    tiles_per = (counts + EXPERT_TILE - 1) // EXPERT_TILE
    tile_end = jnp.cumsum(tiles_per)
    row_start = (tile_end - tiles_per) * EXPERT_TILE
    dest = jnp.sum(onehot * row_start[None, :], axis=1) + rank
    n_used = tile_end[-1]
    tile_ids = jnp.arange(n_tiles, dtype=I32)
    tile_expert = jnp.sum((tile_ids[:, None] >= tile_end[None, :]).astype(I32), axis=1)
    last_expert = jnp.max(jnp.where(counts > 0, jnp.arange(N_EXPERTS, dtype=I32), 0))
    tile_expert = jnp.where(tile_ids < n_used, tile_expert, last_expert).astype(I32)
    token = jnp.arange(n_pairs, dtype=I32) // 2
    filler = jnp.arange(n_tiles * EXPERT_TILE, dtype=I32) % (n_pairs // 2)
    src_rows = filler.at[dest].set(token)
    experts = jnp.arange(N_EXPERTS, dtype=I32)
    used = counts > 0
    later_used = used[None, :] & (experts[None, :] > experts[:, None])
    next_used = jnp.min(jnp.where(later_used, experts[None, :], N_EXPERTS), axis=1)
    next_used = jnp.where(next_used < N_EXPERTS, next_used, -1).astype(I32)
    slot_of = ((jnp.cumsum(used.astype(I32)) - 1) & 1).astype(I32)
    plan = (tile_expert, n_used.reshape(1).astype(I32), next_used[tile_expert], slot_of[tile_expert], src_rows)
    return dest.astype(I32), plan


def _stream_mix(x, mod, conv_state, shift_state, wkv_state, lp, nb, tb, tt, h2_tail=None):
    b, t, d = x.shape
    g = b // WKV_BATCHES
    h = _norm1(x, mod, lp["norm1_g"], nb, tb)
    proj = _inproj(h, lp["w_in"])
    outs = _mixprep(proj.reshape(b, t, D_IN_PROJ), conv_state, shift_state, lp, nb, tb)
    operands = [z.reshape(b, t * R_TILES, LANES) for z in outs[:N_WKV_OPERANDS]]
    yc, gate, bonus, new_conv, new_shift = outs[N_WKV_OPERANDS:]
    y_tiled, s_chain = _wkv(operands, _state_to_chain(wkv_state, g), tt)
    new_wkv = _state_from_chain(s_chain, g)
    x1, h2_tiled, eid, wts = _outproj(yc, y_tiled.reshape(b * t * R_TILES, LANES), bonus, gate, lp["ln_x_w"],
                                      lp["ln_x_b"], lp["w_out_bf16"], x, mod, lp["norm2_g"], lp["wr_hi"],
                                      lp["wr_lo"], lp["b_router"], nb, tb, h2_tail)
    return x1, h2_tiled, eid, wts, new_conv, new_shift.reshape(b, D_RWKV_PROJ), new_wkv


def kernel(x_prompt, x_sample, c_prompt, c_sample, state_conv, state_shift, state_wkv, w_ada, b_ada, norm1_g, w_in, conv_w, mix_mu, w0, w_decay_up, a0, w_aaa_up, w_gate_up, k_k, k_a, r_k, ln_x_w, ln_x_b, w_out, norm2_g, w_router_group, b_router_group, w_router_expert, b_router_expert, w_gate, w_up, w_down, final_g):
    depth = w_ada.shape[0]
    assert depth == 1
    bp, tp, d = x_prompt.shape
    bs, ts, _ = x_sample.shape
    l = 0

    row = lambda z: z.reshape(1, -1)
    zeros_lora = jnp.zeros((LORA_DECAY, D_RWKV), F32)
    n_route = N_GROUPS + N_EXPERTS
    w_router = jnp.pad(jnp.concatenate([w_router_group[l], w_router_expert[l]], axis=1),
                       ((0, 0), (0, LANES - n_route)))
    b_router = jnp.concatenate([b_router_group[l], b_router_expert[l]])
    wr_hi = w_router.astype(BF16)
    lp = dict(
        norm1_g=row(norm1_g[l]), w_in=w_in[l], conv_w=conv_w[l], mix_mu=row(mix_mu[l]), w0=row(w0[l]),
        wd_pad=jnp.concatenate([w_decay_up[l], zeros_lora], axis=0), a0=row(a0[l]),
        wa_pad=jnp.concatenate([zeros_lora, w_aaa_up[l]], axis=0), w_gate_up=w_gate_up[l],
        k_k=row(k_k[l]), k_a=row(k_a[l]), r_k=row(r_k[l]), ln_x_w=row(ln_x_w[l]), ln_x_b=row(ln_x_b[l]),
        w_out_bf16=_cast_bf16(w_out[l]), norm2_g=row(norm2_g[l]),
        wr_hi=wr_hi, wr_lo=(w_router - wr_hi.astype(F32)).astype(BF16),
        b_router=jnp.pad(b_router, (0, LANES - n_route)).reshape(1, LANES),
    )

    n_c = bp + bs
    c_all = jnp.concatenate([c_prompt, c_sample], axis=0)
    c_all = jnp.pad(c_all, ((0, (-n_c) % SUBLANES), (0, 0)))
    mod = _ada(c_all, w_ada[l], row(b_ada[l]))
    mod_p = mod[:bp].reshape(bp, 1, N_MOD * d)
    mod_s = mod[bp:n_c].reshape(bs, 1, N_MOD * d)

    zc = jnp.zeros((bp, 2, D_CONV), F32)
    zs = jnp.zeros((bp, 1, D_RWKV_PROJ), F32)
    zw = jnp.zeros((bp, N_HEADS, HEAD_DIM, HEAD_DIM), F32)
    n_p = bp * tp
    x1_s, h2_s, eid_s, wts_s, conv_s, shift_s, wkv_s = _stream_mix(
        x_sample, mod_s, state_conv[l], state_shift[l].reshape(bs, 1, D_RWKV_PROJ), state_wkv[l], lp,
        nb=ROW_TILE // ts, tb=ts, tt=ts)
    x1_p, h2, eid_p, wts_p, conv_p, shift_p, wkv_p = _stream_mix(
        x_prompt, mod_p, zc, zs, zw, lp, nb=1, tb=ROW_TILE, tt=128, h2_tail=h2_s)

    eid = jnp.concatenate([eid_p[:, :2], eid_s[:, :2]], axis=0)
    dest, plan = _dispatch_plan(eid)
    ys = _experts(plan, h2, w_gate[l], w_up[l], w_down[l])
    y_prompt = _combine(dest[:2 * n_p], wts_p, x1_p, mod_p, row(final_g), ys, bp, tp, nb=1, tb=ROW_TILE)
    y_sample = _combine(dest[2 * n_p:], wts_s, x1_s, mod_s, row(final_g), ys, bs, ts, nb=ROW_TILE // ts, tb=ts)

    return (y_prompt, y_sample, conv_p[None], shift_p[None], wkv_p[None],
            conv_s[None], shift_s[None], wkv_s[None])
```

```python
import functools

import jax
import jax.numpy as jnp
from jax import lax
from jax.experimental import pallas as pl
from jax.experimental.pallas import tpu as pltpu

F32 = jnp.float32
BF16 = jnp.bfloat16
I32 = jnp.int32

D_MODEL = 2048
D_CONV = 1024
D_RWKV = 1024
HEAD_DIM = 64
N_HEADS = 16
LORA_DECAY = 64
LORA_AAA = 64
LORA_GATE = 128
D_RWKV_PROJ = 3 * D_RWKV + LORA_DECAY + LORA_AAA + LORA_GATE
D_IN_PROJ = 3 * D_CONV + D_RWKV_PROJ
N_GROUPS = 4
EXPERTS_PER_GROUP = 8
N_EXPERTS = 32
D_EXPERT = 512
N_MOD = 6
RMS_EPS = 1e-6
GN_EPS = HEAD_DIM * 1e-5

SUBLANES = 8
LANES = 128
VMEM_LIMIT_BYTES = 56 * 1024 * 1024

ROW_TILE = 256
EXPERT_TILE = 256
SEG = 256
D_TILES = D_MODEL // LANES
R_TILES = D_RWKV // LANES
GATHER_PITCH = D_TILES + SUBLANES
GATHER_DMA_PRIORITY = 0
WEIGHT_DMA_PRIORITY = 1

WKV_BATCHES = 4
V_LO = LANES // (WKV_BATCHES * R_TILES)
V_HI = HEAD_DIM // V_LO
N_WKV_OPERANDS = 6
STATE_PITCH = HEAD_DIM + SUBLANES


def _params(semantics):
    return pltpu.CompilerParams(dimension_semantics=semantics, vmem_limit_bytes=VMEM_LIMIT_BYTES)


def _sigmoid(x):
    return 1.0 / (1.0 + jnp.exp(-x))


def _store_row_tiled(ref, x, pitch):
    rows = x.shape[0]
    for s in range(x.shape[1] // LANES):
        ref[pl.ds(s, rows, stride=pitch), :] = x[:, s * LANES:(s + 1) * LANES]


def _load_row_tiled(ref, rows, n, pitch):
    return jnp.concatenate([ref[pl.ds(s, rows, stride=pitch), :] for s in range(n)], axis=-1)


def _head_ones():
    r = lax.broadcasted_iota(I32, (SEG, SEG), 0) // HEAD_DIM
    c = lax.broadcasted_iota(I32, (SEG, SEG), 1) // HEAD_DIM
    return jnp.where(r == c, 1.0, 0.0).astype(BF16)


def _split_bf16(x):
    hi = x.astype(BF16)
    return hi, (x - hi.astype(F32)).astype(BF16)


def _head_sum(x, ones):
    outs = []
    for s in range(x.shape[-1] // SEG):
        hi, lo = _split_bf16(x[:, s * SEG:(s + 1) * SEG])
        outs.append(jnp.dot(hi, ones, preferred_element_type=F32) + jnp.dot(lo, ones, preferred_element_type=F32))
    return jnp.concatenate(outs, axis=-1)


def _ada_kernel(c_ref, w_ref, b_ref, o_ref):
    c = c_ref[...]
    s = c * _sigmoid(c)
    o_ref[...] = jnp.dot(s.astype(BF16), w_ref[...].astype(BF16), preferred_element_type=F32) + b_ref[...]


def _ada(c_all, w_ada, b_ada):
    mb = c_all.shape[0]
    n = w_ada.shape[1]
    tn = 1024
    return pl.pallas_call(
        _ada_kernel,
        grid=(n // tn,),
        in_specs=[pl.BlockSpec((mb, D_MODEL), lambda j: (0, 0)),
                  pl.BlockSpec((D_MODEL, tn), lambda j: (0, j)),
                  pl.BlockSpec((1, tn), lambda j: (0, j))],
        out_specs=pl.BlockSpec((mb, tn), lambda j: (0, j)),
        out_shape=jax.ShapeDtypeStruct((mb, n), F32),
        compiler_params=_params(("arbitrary",)),
        name="ada",
    )(c_all, w_ada, b_ada)


def _norm1_kernel(x_ref, g_ref, sh_ref, sc_ref, o_ref):
    x = x_ref[...]
    nb, tb, d = x.shape
    ms = jnp.mean(x * x, axis=-1, keepdims=True)
    xn = x * lax.rsqrt(ms + RMS_EPS) * g_ref[...]
    h = xn * (1.0 + sc_ref[...]) + sh_ref[...]
    o_ref[...] = h.reshape(nb * tb, d).astype(BF16)


def _norm1(x, mod, g, nb, tb):
    b, t, d = x.shape
    nt = t // tb
    return pl.pallas_call(
        _norm1_kernel,
        grid=(b // nb, nt),
        in_specs=[pl.BlockSpec((nb, tb, d), lambda i, j: (i, j, 0)),
                  pl.BlockSpec((1, d), lambda i, j: (0, 0)),
                  pl.BlockSpec((nb, 1, d), lambda i, j: (i, 0, 0)),
                  pl.BlockSpec((nb, 1, d), lambda i, j: (i, 0, 1))],
        out_specs=pl.BlockSpec((nb * tb, d), lambda i, j: (i * nt + j, 0)),
        out_shape=jax.ShapeDtypeStruct((b * t, d), BF16),
        compiler_params=_params(("arbitrary", "arbitrary")),
        name="norm1",
    )(x, g, mod, mod)


def _inproj_kernel(h_ref, w_ref, o_ref, wb_ref):
    @pl.when(pl.program_id(1) == 0)
    def _():
        wb_ref[...] = w_ref[...].astype(BF16)

    o_ref[...] = jnp.dot(h_ref[...], wb_ref[...], preferred_element_type=F32)


def _inproj(h, w_in):
    m, k = h.shape
    n = w_in.shape[1]
    tm = min(m, 1024)
    tn = 1280
    return pl.pallas_call(
        _inproj_kernel,
        grid=(n // tn, m // tm),
        in_specs=[pl.BlockSpec((tm, k), lambda j, i: (i, 0)),
                  pl.BlockSpec((k, tn), lambda j, i: (0, j))],
        out_specs=pl.BlockSpec((tm, tn), lambda j, i: (i, j)),
        out_shape=jax.ShapeDtypeStruct((m, n), F32),
        scratch_shapes=[pltpu.VMEM((k, tn), BF16)],
        compiler_params=_params(("arbitrary", "arbitrary")),
        name="inproj",
    )(h, w_in)


def _mixprep_kernel(proj_ref, cst_ref, sst_ref, convw_ref, mu_ref, w0_ref, wd_ref, a0_ref, wa_ref, wg_ref,
                    kk_ref, ka_ref, rk_ref,
                    a_o, w_o, b_o, k_o, r_o, v_o, yc_o, g_o, bonus_o, nc_o, ns_o):
    @pl.when(pl.program_id(1) == 0)
    def _():
        nc_o[...] = cst_ref[...]
        ns_o[...] = sst_ref[...]

    nb, tb, _ = proj_ref.shape
    rows = nb * tb
    p = proj_ref[...].reshape(rows, D_IN_PROJ)
    tidx = lax.broadcasted_iota(I32, (rows, 1), 0) & (tb - 1)

    def per_row(c):
        return jnp.broadcast_to(c, (nb, tb, c.shape[-1])).reshape(rows, c.shape[-1])

    gate_b = p[:, 0:D_CONV]
    gate_c = p[:, D_CONV:2 * D_CONV]
    h_conv = p[:, 2 * D_CONV:3 * D_CONV]
    pr = p[:, 3 * D_CONV:]

    u = gate_c * h_conv
    carry = nc_o[...]
    c2 = per_row(carry[:, 0:1, :])
    c1 = per_row(carry[:, 1:2, :])
    u1 = jnp.where(tidx == 0, c1, pltpu.roll(u, 1, 0))
    u2 = jnp.where(tidx == 0, c2, jnp.where(tidx == 1, c1, pltpu.roll(u, 2, 0)))
    cw = convw_ref[...]
    conv_y = u * cw[2:3, :] + u2 * cw[0:1, :] + u1 * cw[1:2, :]
    yc_o[...] = (gate_b * conv_y).astype(BF16)
    nc_o[...] = u.reshape(nb, tb, D_CONV)[:, tb - 2:tb, :]

    p_prev = jnp.where(tidx == 0, per_row(ns_o[...]), pltpu.roll(pr, 1, 0))
    ns_o[...] = pr.reshape(nb, tb, D_RWKV_PROJ)[:, tb - 1:tb, :]
    pm = pr + (p_prev - pr) * mu_ref[...]
    r = pm[:, 0:D_RWKV]
    k = pm[:, D_RWKV:2 * D_RWKV]
    v = pm[:, 2 * D_RWKV:3 * D_RWKV]
    x_lora = pm[:, 3 * D_RWKV:3 * D_RWKV + LORA_DECAY + LORA_AAA]
    xg = pm[:, 3 * D_RWKV + LORA_DECAY + LORA_AAA:]

    d_pre = w0_ref[...] + jnp.dot(jnp.tanh(x_lora).astype(BF16), wd_ref[...].astype(BF16),
                                  preferred_element_type=F32)
    a_pre = a0_ref[...] + jnp.dot(x_lora.astype(BF16), wa_ref[...].astype(BF16), preferred_element_type=F32)
    g = jnp.dot(_sigmoid(xg).astype(BF16), wg_ref[...].astype(BF16), preferred_element_type=F32)
    z = -d_pre
    softplus = jnp.maximum(z, 0.0) + jnp.log(1.0 + jnp.exp(-jnp.abs(z)))
    w_log = -softplus - 0.5
    decay = jnp.exp(-jnp.exp(w_log))
    a_h = _sigmoid(a_pre)

    ones = _head_ones()
    kk = k * kk_ref[...]
    kk = kk * lax.rsqrt(jnp.maximum(_head_sum(kk * kk, ones), 1e-24))
    k_h = k * (1.0 + (a_h - 1.0) * ka_ref[...])
    bonus = _head_sum(r * k_h * rk_ref[...], ones) * v

    _store_row_tiled(a_o, -kk, R_TILES)
    _store_row_tiled(w_o, decay, R_TILES)
    _store_row_tiled(b_o, kk * a_h, R_TILES)
    _store_row_tiled(k_o, k_h, R_TILES)
    _store_row_tiled(r_o, r, R_TILES)
    _store_row_tiled(v_o, v, R_TILES)
    g_o[...] = g
    bonus_o[...] = bonus


def _mixprep(proj3, conv_state, shift_state, lp, nb, tb):
    b, t, _ = proj3.shape
    nt = t // tb
    rows = nb * tb
    m = b * t

    def row_spec(width):
        return pl.BlockSpec((rows, width), lambda i, j: (i * nt + j, 0))

    def full2(a):
        return pl.BlockSpec(a.shape, lambda i, j: (0, 0))

    consts = (lp["conv_w"], lp["mix_mu"], lp["w0"], lp["wd_pad"], lp["a0"], lp["wa_pad"], lp["w_gate_up"],
              lp["k_k"], lp["k_a"], lp["r_k"])
    tiled_spec = pl.BlockSpec((rows * R_TILES, LANES), lambda i, j: (i * nt + j, 0))
    tiled_out = jax.ShapeDtypeStruct((m * R_TILES, LANES), F32)
    f32_out = jax.ShapeDtypeStruct((m, D_RWKV), F32)
    return pl.pallas_call(
        _mixprep_kernel,
        grid=(b // nb, nt),
        in_specs=[pl.BlockSpec((nb, tb, D_IN_PROJ), lambda i, j: (i, j, 0)),
                  pl.BlockSpec((nb, 2, D_CONV), lambda i, j: (i, 0, 0)),
                  pl.BlockSpec((nb, 1, D_RWKV_PROJ), lambda i, j: (i, 0, 0))] + [full2(c) for c in consts],
        out_specs=[tiled_spec] * N_WKV_OPERANDS + [row_spec(D_CONV), row_spec(D_RWKV), row_spec(D_RWKV),
                                                   pl.BlockSpec((nb, 2, D_CONV), lambda i, j: (i, 0, 0)),
                                                   pl.BlockSpec((nb, 1, D_RWKV_PROJ), lambda i, j: (i, 0, 0))],
        out_shape=[tiled_out] * N_WKV_OPERANDS + [jax.ShapeDtypeStruct((m, D_CONV), BF16), f32_out, f32_out,
                                                  jax.ShapeDtypeStruct((b, 2, D_CONV), F32),
                                                  jax.ShapeDtypeStruct((b, 1, D_RWKV_PROJ), F32)],
        compiler_params=_params(("arbitrary", "arbitrary")),
        name="mixprep",
    )(proj3, conv_state, shift_state, *consts)


def _wkv_kernel(a_ref, w_ref, b_ref, k_ref, r_ref, v_ref, s0_ref, y_ref, sout_ref, s_ref, tiles, yq, cbuf):
    n_pairs = WKV_BATCHES * R_TILES

    @pl.when(pl.program_id(1) == 0)
    def _():
        for bp in range(n_pairs):
            bb, hp = divmod(bp, R_TILES)
            cbuf[pl.ds(bp * STATE_PITCH, HEAD_DIM), :] = jnp.concatenate(
                [s0_ref[bb, 2 * hp], s0_ref[bb, 2 * hp + 1]], axis=1)
        for vh in range(V_HI):
            m = jnp.concatenate([cbuf[pl.ds(vh * V_LO + vl, n_pairs, stride=STATE_PITCH), :]
                                 for vl in range(V_LO)], axis=0)
            mt = m.T
            s_ref[0, vh] = mt[:HEAD_DIM]
            s_ref[1, vh] = mt[HEAD_DIM:]

    tt = a_ref.shape[1] // R_TILES
    srcs = (a_ref, w_ref, b_ref, k_ref, r_ref, v_ref)
    n_chain = WKV_BATCHES * R_TILES
    sub8 = lax.broadcasted_iota(I32, (SUBLANES, LANES), 0)
    grp8 = lax.broadcasted_iota(I32, (SUBLANES, LANES), 1) // n_chain
    diag8 = (sub8 % V_LO) == grp8
    sub4 = lax.broadcasted_iota(I32, (V_LO, LANES), 0)
    grp4 = lax.broadcasted_iota(I32, (V_LO, LANES), 1) // n_chain
    diag4 = sub4 == grp4

    def produce(t, slot):
        row0 = pl.multiple_of(t * R_TILES, R_TILES)
        for ai, ref in enumerate(srcs):
            m = jnp.concatenate([ref[bb, pl.ds(row0, R_TILES), :] for bb in range(WKV_BATCHES)], axis=0)
            xt = jnp.concatenate([m] * V_LO, axis=0).T
            if ai < N_WKV_OPERANDS - 1:
                tiles[slot, ai] = xt
            else:
                for q in range(LANES // SUBLANES):
                    z = jnp.where(diag8, xt[q * SUBLANES:(q + 1) * SUBLANES], 0.0)
                    z = z + pltpu.roll(z, 1, 0)
                    z = z + pltpu.roll(z, 2, 0)
                    tiles[slot, ai, q * SUBLANES:(q + 1) * SUBLANES, :] = z

    def consume(slot):
        for par in range(2):
            lo = par * HEAD_DIM
            a = tiles[slot, 0, lo:lo + HEAD_DIM, :]
            w = tiles[slot, 1, lo:lo + HEAD_DIM, :]
            b = tiles[slot, 2, lo:lo + HEAD_DIM, :]
            k = tiles[slot, 3, lo:lo + HEAD_DIM, :]
            r = tiles[slot, 4, lo:lo + HEAD_DIM, :]
            for vh in range(V_HI):
                s = s_ref[par, vh]
                vrow = tiles[slot, 5, lo + vh * V_LO + V_LO - 1:lo + (vh + 1) * V_LO, :]
                sa = jnp.sum(s * a, axis=0, keepdims=True)
                s = s * w + sa * b + vrow * k
                s_ref[par, vh] = s
                yrow = jnp.sum(s * r, axis=0, keepdims=True)
                yq[slot, lo + vh * V_LO:lo + (vh + 1) * V_LO, :] = jnp.where(
                    diag4, jnp.broadcast_to(yrow, (V_LO, LANES)), 0.0)

    def finalize(t, slot):
        qt = yq[slot].T
        res = qt[0:n_chain]
        for j in range(1, V_LO):
            res = res + qt[j * n_chain:(j + 1) * n_chain]
        row0 = pl.multiple_of(t * R_TILES, R_TILES)
        for bb in range(WKV_BATCHES):
            y_ref[bb, pl.ds(row0, R_TILES), :] = res[bb * R_TILES:(bb + 1) * R_TILES]

    produce(0, 0)
    produce(1, 1)
    consume(0)

    def body(i, c):
        t1 = 2 * i + 1
        produce(t1 + 1, 0)
        consume(1)
        finalize(t1 - 1, 0)
        produce(jnp.minimum(t1 + 2, tt - 1), 1)
        consume(0)
        finalize(t1, 1)
        return c

    lax.fori_loop(0, tt // 2 - 1, body, 0)
    consume(1)
    finalize(tt - 2, 0)
    finalize(tt - 1, 1)

    @pl.when(pl.program_id(1) == pl.num_programs(1) - 1)
    def _():
        for vh in range(V_HI):
            m = jnp.concatenate([s_ref[0, vh], s_ref[1, vh]], axis=0).T
            for vl in range(V_LO):
                cbuf[pl.ds(vh * V_LO + vl, n_pairs, stride=STATE_PITCH), :] = m[vl * n_pairs:(vl + 1) * n_pairs]
        for bp in range(n_pairs):
            bb, hp = divmod(bp, R_TILES)
            c = cbuf[pl.ds(bp * STATE_PITCH, HEAD_DIM), :]
            sout_ref[bb, 2 * hp] = c[:, :HEAD_DIM]
            sout_ref[bb, 2 * hp + 1] = c[:, HEAD_DIM:]


def _wkv(operands, s0, tt):
    bsz, rows, _ = operands[0].shape
    t = rows // R_TILES
    g = bsz // WKV_BATCHES
    assert tt % 2 == 0 and tt >= 4 and t % tt == 0
    blk = pl.BlockSpec((WKV_BATCHES, tt * R_TILES, LANES), lambda i, j: (i, j, 0))
    state = pl.BlockSpec((WKV_BATCHES, N_HEADS, HEAD_DIM, HEAD_DIM), lambda i, j: (i, 0, 0, 0))
    return pl.pallas_call(
        _wkv_kernel,
        grid=(g, t // tt),
        in_specs=[blk] * N_WKV_OPERANDS + [state],
        out_specs=[blk, state],
        out_shape=[jax.ShapeDtypeStruct((bsz, rows, LANES), F32),
                   jax.ShapeDtypeStruct((bsz, N_HEADS, HEAD_DIM, HEAD_DIM), F32)],
        scratch_shapes=[pltpu.VMEM((2, V_HI, HEAD_DIM, LANES), F32),
                        pltpu.VMEM((2, N_WKV_OPERANDS, LANES, LANES), F32), pltpu.VMEM((2, LANES, LANES), F32),
                        pltpu.VMEM((WKV_BATCHES * R_TILES * STATE_PITCH, LANES), F32)],
        compiler_params=_params(("arbitrary", "arbitrary")),
        name="wkv",
    )(*operands, s0)


def _group_norm_gate(y, bonus, g, ln_w, ln_b):
    ones = _head_ones()
    mu = _head_sum(y, ones) * (1.0 / HEAD_DIM)
    d = y - mu
    var = _head_sum(d * d, ones) * (1.0 / HEAD_DIM)
    yn = d * lax.rsqrt(var + GN_EPS)
    yn = yn * ln_w + ln_b
    return (yn + bonus) * g


def _cast_kernel(x_ref, o_ref):
    o_ref[...] = x_ref[...].astype(o_ref.dtype)


def _cast_bf16(w):
    r, c = w.shape
    tr = 512
    return pl.pallas_call(
        _cast_kernel,
        grid=(r // tr,),
        in_specs=[pl.BlockSpec((tr, c), lambda i: (i, 0))],
        out_specs=pl.BlockSpec((tr, c), lambda i: (i, 0)),
        out_shape=jax.ShapeDtypeStruct((r, c), BF16),
        compiler_params=_params(("arbitrary",)),
        name="cast_bf16",
    )(w)


def _route(lg):
    lane = lax.broadcasted_iota(I32, lg.shape, 1)
    neg = -jnp.inf
    big = jnp.int32(1 << 20)

    is_g = lane < N_GROUPS
    gl = jnp.where(is_g, lg, neg)
    gmax = jnp.max(gl, axis=1, keepdims=True)
    gidx = jnp.min(jnp.where(gl == gmax, lane, big), axis=1, keepdims=True)
    gsum = jnp.sum(jnp.where(is_g, jnp.exp(gl - gmax), 0.0), axis=1, keepdims=True)
    gprob = 1.0 / gsum

    e_lane = lane - N_GROUPS
    in_grp = (e_lane >= 0) & (e_lane < N_EXPERTS) & ((e_lane >> 3) == gidx)
    el = jnp.where(in_grp, lg, neg)
    emax = jnp.max(el, axis=1, keepdims=True)
    ex = jnp.where(in_grp, jnp.exp(el - emax), 0.0)
    prob = ex / jnp.sum(ex, axis=1, keepdims=True)
    prob = jnp.where(in_grp, prob, -1.0)
    p1 = jnp.max(prob, axis=1, keepdims=True)
    i1 = jnp.min(jnp.where(prob == p1, lane, big), axis=1, keepdims=True)
    prob2 = jnp.where(lane == i1, -1.0, prob)
    p2 = jnp.max(prob2, axis=1, keepdims=True)
    i2 = jnp.min(jnp.where(prob2 == p2, lane, big), axis=1, keepdims=True)
    psum = p1 + p2
    w1 = p1 / psum * gprob
    w2 = p2 / psum * gprob
    eid = jnp.where(lane == 0, i1 - N_GROUPS, jnp.where(lane == 1, i2 - N_GROUPS, 0))
    wts = jnp.where(lane == 0, w1, jnp.where(lane == 1, w2, 0.0))
    return eid, wts


def _outproj_kernel(n_main, yc_ref, y_ref, bonus_ref, gate_ref, lnw_ref, lnb_ref, wt_ref, wb_ref, x_ref,
                    ga1_ref, sh2_ref, sc2_ref, g2_ref, wrh_ref, wrl_ref, br_ref, *rest):
    x1_o, h2_o, eid_o, wts_o = rest[-4:]
    i = pl.program_id(0)

    @pl.when(i < n_main)
    def _():
        nb, tb, d = x_ref.shape
        rows = nb * tb
        y = _load_row_tiled(y_ref, rows, R_TILES, R_TILES)
        yr = _group_norm_gate(y, bonus_ref[...], gate_ref[...], lnw_ref[...], lnb_ref[...]).astype(BF16)
        mix = (jnp.dot(yc_ref[...], wt_ref[...], preferred_element_type=F32)
               + jnp.dot(yr, wb_ref[...], preferred_element_type=F32))
        x1 = x_ref[...] + ga1_ref[...] * mix.reshape(nb, tb, d)
        ms = jnp.mean(x1 * x1, axis=-1, keepdims=True)
        h2 = x1 * lax.rsqrt(ms + RMS_EPS) * g2_ref[...]
        h2 = (h2 * (1.0 + sc2_ref[...]) + sh2_ref[...]).reshape(rows, d)
        x1_o[...] = x1.reshape(rows, d)
        _store_row_tiled(h2_o, h2, D_TILES)
        h_hi, h_lo = _split_bf16(h2)
        w_hi = wrh_ref[...]
        logits = (jnp.dot(h_hi, w_hi, preferred_element_type=F32)
                  + jnp.dot(h_lo, w_hi, preferred_element_type=F32)
                  + jnp.dot(h_hi, wrl_ref[...], preferred_element_type=F32)) + br_ref[...]
        eid_o[...], wts_o[...] = _route(logits)

    if len(rest) == 5:
        @pl.when(i >= n_main)
        def _():
            h2_o[...] = rest[0][...]


def _outproj(yc, y_tiled, bonus, gate, ln_w, ln_b, w_out_bf16, x, mod, g2, wr_hi, wr_lo, b_router, nb, tb,
             h2_tail=None):
    b, t, d = x.shape
    nt = t // tb
    rows = nb * tb
    m = b * t
    half = D_CONV
    n_main = (b // nb) * nt
    n_tail = 0 if h2_tail is None else h2_tail.shape[0] // (rows * D_TILES)
    blk = lambda i: jnp.minimum(i, n_main - 1)
    row = lambda width: pl.BlockSpec((rows, width), lambda i: (blk(i), 0))
    modspec = lambda col: pl.BlockSpec((nb, 1, d), lambda i: (blk(i) // nt, 0, col))
    full = lambda a: pl.BlockSpec(a.shape, lambda i: (0, 0))
    in_specs = [row(half), pl.BlockSpec((rows * R_TILES, LANES), lambda i: (blk(i), 0)), row(half), row(half),
                full(ln_w), full(ln_b),
                pl.BlockSpec((half, d), lambda i: (0, 0)),
                pl.BlockSpec((half, d), lambda i: (1, 0)),
                pl.BlockSpec((nb, tb, d), lambda i: (blk(i) // nt, blk(i) % nt, 0)),
                modspec(2), modspec(3), modspec(4),
                full(g2), full(wr_hi), full(wr_lo), full(b_router)]
    args = [yc, y_tiled, bonus, gate, ln_w, ln_b, w_out_bf16, w_out_bf16, x, mod, mod, mod, g2, wr_hi, wr_lo,
            b_router]
    if h2_tail is not None:
        in_specs.append(pl.BlockSpec((rows * D_TILES, LANES), lambda i: (jnp.maximum(i - n_main, 0), 0)))
        args.append(h2_tail)
    return pl.pallas_call(
        functools.partial(_outproj_kernel, n_main),
        grid=(n_main + n_tail,),
        in_specs=in_specs,
        out_specs=[row(d), pl.BlockSpec((rows * D_TILES, LANES), lambda i: (i, 0)), row(LANES), row(LANES)],
        out_shape=[jax.ShapeDtypeStruct((m, d), F32),
                   jax.ShapeDtypeStruct(((n_main + n_tail) * rows * D_TILES, LANES), F32),
                   jax.ShapeDtypeStruct((m, LANES), I32), jax.ShapeDtypeStruct((m, LANES), F32)],
        compiler_params=_params(("arbitrary",)),
        name="outproj",
    )(*args)


def _row_copy(src_hbm, src_row, dst_ref, dst_row, sem):
    src = src_row * D_TILES
    dst = dst_row * GATHER_PITCH
    if not isinstance(src, int):
        src = pl.multiple_of(src, D_TILES)
    if not isinstance(dst, int):
        dst = pl.multiple_of(dst, SUBLANES)
    return pltpu.make_async_copy(src_hbm.at[pl.ds(src, D_TILES)], dst_ref.at[pl.ds(dst, D_TILES)], sem)


def _start_row_gather(index_of, src_hbm, dst_ref, sem, n):
    for r in range(n):
        _row_copy(src_hbm, index_of(r), dst_ref, r, sem).start(priority=GATHER_DMA_PRIORITY)


def _wait_row_gather(src_hbm, dst_ref, sem, n):
    def body(r, c):
        _row_copy(src_hbm, 0, dst_ref, r, sem).wait()
        return c
    lax.fori_loop(0, n, body, 0, unroll=8)


def _experts_kernel(te_ref, nu_ref, nxt_ref, wsl_ref, idx_ref, idxn_ref, h_hbm, wg_hbm, wu_hbm, wd_hbm, o_ref,
                    xbuf, wgf, wuf, wdf, wgb, wub, wdb, sems, wsems):
    i = pl.program_id(0)
    n_used = nu_ref[0]
    slot = i & 1

    def weight_copies(e, ws):
        return (pltpu.make_async_copy(wg_hbm.at[e], wgf.at[ws], wsems.at[ws]),
                pltpu.make_async_copy(wu_hbm.at[e], wuf.at[ws], wsems.at[ws]),
                pltpu.make_async_copy(wd_hbm.at[e], wdf.at[ws], wsems.at[ws]))

    @pl.when(i == 0)
    def _():
        for c in weight_copies(te_ref[0], wsl_ref[0]):
            c.start(priority=WEIGHT_DMA_PRIORITY)
        _start_row_gather(lambda r: idx_ref[0, r], h_hbm, xbuf.at[0], sems.at[0], EXPERT_TILE)

    @pl.when(i + 1 < n_used)
    def _():
        _start_row_gather(lambda r: idxn_ref[0, r], h_hbm, xbuf.at[1 - slot], sems.at[1 - slot], EXPERT_TILE)

    @pl.when(i < n_used)
    def _():
        new_expert = jnp.logical_or(i == 0, te_ref[i] != te_ref[jnp.maximum(i - 1, 0)])

        @pl.when(new_expert)
        def _():
            ws = wsl_ref[i]
            for c in weight_copies(te_ref[i], ws):
                c.wait()
            wgb[...] = wgf[ws].astype(BF16)
            wub[...] = wuf[ws].astype(BF16)
            wdb[...] = wdf[ws].astype(BF16)

            @pl.when(nxt_ref[i] >= 0)
            def _():
                for c in weight_copies(nxt_ref[i], 1 - ws):
                    c.start(priority=WEIGHT_DMA_PRIORITY)

        _wait_row_gather(h_hbm, xbuf.at[slot], sems.at[slot], EXPERT_TILE)
        x = _load_row_tiled(xbuf.at[slot], EXPERT_TILE, D_TILES, GATHER_PITCH).astype(BF16)
        gate = jnp.dot(x, wgb[...], preferred_element_type=F32)
        up = jnp.dot(x, wub[...], preferred_element_type=F32)
        act = (gate * _sigmoid(gate) * up).astype(BF16)
        _store_row_tiled(o_ref, jnp.dot(act, wdb[...], preferred_element_type=F32), D_TILES)

    @pl.when(i >= n_used)
    def _():
        o_ref[...] = jnp.zeros_like(o_ref)


def _experts(plan, h2_tiled, w_gate, w_up, w_down):
    tile_expert, n_used, next_expert, weight_slot, src_rows = plan
    n_tiles = tile_expert.shape[0]
    d = w_gate.shape[1]
    idx3 = src_rows.reshape(n_tiles, 1, EXPERT_TILE)
    smem_idx = lambda f: pl.BlockSpec((None, 1, EXPERT_TILE), f, memory_space=pltpu.SMEM)
    hbm = pl.BlockSpec(memory_space=pl.ANY)
    grid_spec = pltpu.PrefetchScalarGridSpec(
        num_scalar_prefetch=4,
        grid=(n_tiles,),
        in_specs=[smem_idx(lambda i, *_: (i, 0, 0)),
                  smem_idx(lambda i, *_: (jnp.minimum(i + 1, n_tiles - 1), 0, 0)),
                  hbm, hbm, hbm, hbm],
        out_specs=pl.BlockSpec((EXPERT_TILE * D_TILES, LANES), lambda i, *_: (i, 0)),
        scratch_shapes=[pltpu.VMEM((2, EXPERT_TILE * GATHER_PITCH, LANES), F32),
                        pltpu.VMEM((2, d, D_EXPERT), F32), pltpu.VMEM((2, d, D_EXPERT), F32),
                        pltpu.VMEM((2, D_EXPERT, d), F32),
                        pltpu.VMEM((d, D_EXPERT), BF16), pltpu.VMEM((d, D_EXPERT), BF16),
                        pltpu.VMEM((D_EXPERT, d), BF16),
                        pltpu.SemaphoreType.DMA((2,)), pltpu.SemaphoreType.DMA((2,))],
    )
    return pl.pallas_call(
        _experts_kernel,
        grid_spec=grid_spec,
        out_shape=jax.ShapeDtypeStruct((n_tiles * EXPERT_TILE * D_TILES, LANES), F32),
        compiler_params=_params(("arbitrary",)),
        name="experts",
    )(tile_expert, n_used, next_expert, weight_slot, idx3, idx3, h2_tiled, w_gate, w_up, w_down)


def _combine_kernel(idx_ref, idxn_ref, wt_ref, x1_ref, ga2_ref, gf_ref, ys_hbm, o_ref, ybuf, sems):
    i = pl.program_id(0)
    n_steps = pl.num_programs(0)
    slot = i & 1
    nb, tb, d = o_ref.shape
    rows = nb * tb

    def fetch(idx, s):
        for j in range(2):
            _start_row_gather(lambda r: idx[0, 2 * r + j], ys_hbm, ybuf.at[s, j], sems.at[s], rows)

    @pl.when(i == 0)
    def _():
        fetch(idx_ref, 0)

    @pl.when(i + 1 < n_steps)
    def _():
        fetch(idxn_ref, 1 - slot)

    for j in range(2):
        _wait_row_gather(ys_hbm, ybuf.at[slot, j], sems.at[slot], rows)

    wt = wt_ref[...]
    moe = (wt[:, 0:1] * _load_row_tiled(ybuf.at[slot, 0], rows, D_TILES, GATHER_PITCH)
           + wt[:, 1:2] * _load_row_tiled(ybuf.at[slot, 1], rows, D_TILES, GATHER_PITCH))
    x2 = x1_ref[...].reshape(nb, tb, d) + ga2_ref[...] * moe.reshape(nb, tb, d)
    ms = jnp.mean(x2 * x2, axis=-1, keepdims=True)
    o_ref[...] = x2 * lax.rsqrt(ms + RMS_EPS) * gf_ref[...]


def _combine(dest, wts, x1, mod, final_g, ys, b, t, nb, tb):
    d = x1.shape[1]
    nt = t // tb
    rows = nb * tb
    n_steps = (b // nb) * nt
    idx3 = dest.reshape(n_steps, 1, 2 * rows)
    smem_idx = lambda f: pl.BlockSpec((None, 1, 2 * rows), f, memory_space=pltpu.SMEM)
    return pl.pallas_call(
        _combine_kernel,
        grid=(n_steps,),
        in_specs=[smem_idx(lambda i: (i, 0, 0)),
                  smem_idx(lambda i: (jnp.minimum(i + 1, n_steps - 1), 0, 0)),
                  pl.BlockSpec((rows, LANES), lambda i: (i, 0)),
                  pl.BlockSpec((rows, d), lambda i: (i, 0)),
                  pl.BlockSpec((nb, 1, d), lambda i: (i // nt, 0, 5)),
                  pl.BlockSpec((1, d), lambda i: (0, 0)),
                  pl.BlockSpec(memory_space=pl.ANY)],
        out_specs=pl.BlockSpec((nb, tb, d), lambda i: (i // nt, i % nt, 0)),
        out_shape=jax.ShapeDtypeStruct((b, t, d), F32),
        scratch_shapes=[pltpu.VMEM((2, 2, rows * GATHER_PITCH, LANES), F32), pltpu.SemaphoreType.DMA((2,))],
        compiler_params=_params(("arbitrary",)),
        name="combine",
    )(idx3, idx3, wts, x1, mod, final_g, ys)


def _dispatch_plan(eid):
    n_pairs = eid.shape[0] * 2
    n_tiles = n_pairs // EXPERT_TILE + N_EXPERTS
    flat = eid.reshape(-1)
    onehot = (flat[:, None] == jnp.arange(N_EXPERTS, dtype=I32)[None, :]).astype(I32)
    csum = jnp.cumsum(onehot, axis=0)
    rank = jnp.sum((csum - onehot) * onehot, axis=1)
    counts = csum[-1]
    tiles_per = (counts + EXPERT_TILE - 1) // EXPERT_TILE
    tile_end = jnp.cumsum(tiles_per)
    row_start = (tile_end - tiles_per) * EXPERT_TILE
    dest = jnp.sum(onehot * row_start[None, :], axis=1) + rank
    n_used = tile_end[-1]
    tile_ids = jnp.arange(n_tiles, dtype=I32)
    tile_expert = jnp.sum((tile_ids[:, None] >= tile_end[None, :]).astype(I32), axis=1)
    last_expert = jnp.max(jnp.where(counts > 0, jnp.arange(N_EXPERTS, dtype=I32), 0))
    tile_expert = jnp.where(tile_ids < n_used, tile_expert, last_expert).astype(I32)
    token = jnp.arange(n_pairs, dtype=I32) // 2
    filler = jnp.arange(n_tiles * EXPERT_TILE, dtype=I32) % (n_pairs // 2)
    src_rows = filler.at[dest].set(token)
    experts = jnp.arange(N_EXPERTS, dtype=I32)
    used = counts > 0
    later_used = used[None, :] & (experts[None, :] > experts[:, None])
    next_used = jnp.min(jnp.where(later_used, experts[None, :], N_EXPERTS), axis=1)
    next_used = jnp.where(next_used < N_EXPERTS, next_used, -1).astype(I32)
    slot_of = ((jnp.cumsum(used.astype(I32)) - 1) & 1).astype(I32)
    plan = (tile_expert, n_used.reshape(1).astype(I32), next_used[tile_expert], slot_of[tile_expert], src_rows)
    return dest.astype(I32), plan


def _stream_mix(x, mod, conv_state, shift_state, wkv_state, lp, nb, tb, tt, h2_tail=None):
    b, t, d = x.shape
    g = b // WKV_BATCHES
    h = _norm1(x, mod, lp["norm1_g"], nb, tb)
    proj = _inproj(h, lp["w_in"])
    outs = _mixprep(proj.reshape(b, t, D_IN_PROJ), conv_state, shift_state, lp, nb, tb)
    operands = [z.reshape(b, t * R_TILES, LANES) for z in outs[:N_WKV_OPERANDS]]
    yc, gate, bonus, new_conv, new_shift = outs[N_WKV_OPERANDS:]
    y_tiled, new_wkv = _wkv(operands, wkv_state, tt)
    x1, h2_tiled, eid, wts = _outproj(yc, y_tiled.reshape(b * t * R_TILES, LANES), bonus, gate, lp["ln_x_w"],
                                      lp["ln_x_b"], lp["w_out_bf16"], x, mod, lp["norm2_g"], lp["wr_hi"],
                                      lp["wr_lo"], lp["b_router"], nb, tb, h2_tail)
    return x1, h2_tiled, eid, wts, new_conv, new_shift.reshape(b, D_RWKV_PROJ), new_wkv


def kernel(x_prompt, x_sample, c_prompt, c_sample, state_conv, state_shift, state_wkv, w_ada, b_ada, norm1_g, w_in, conv_w, mix_mu, w0, w_decay_up, a0, w_aaa_up, w_gate_up, k_k, k_a, r_k, ln_x_w, ln_x_b, w_out, norm2_g, w_router_group, b_router_group, w_router_expert, b_router_expert, w_gate, w_up, w_down, final_g):
    depth = w_ada.shape[0]
    assert depth == 1
    bp, tp, d = x_prompt.shape
    bs, ts, _ = x_sample.shape
    l = 0

    row = lambda z: z.reshape(1, -1)
    zeros_lora = jnp.zeros((LORA_DECAY, D_RWKV), F32)
    n_route = N_GROUPS + N_EXPERTS
    w_router = jnp.pad(jnp.concatenate([w_router_group[l], w_router_expert[l]], axis=1),
                       ((0, 0), (0, LANES - n_route)))
    b_router = jnp.concatenate([b_router_group[l], b_router_expert[l]])
    wr_hi = w_router.astype(BF16)
    lp = dict(
        norm1_g=row(norm1_g[l]), w_in=w_in[l], conv_w=conv_w[l], mix_mu=row(mix_mu[l]), w0=row(w0[l]),
        wd_pad=jnp.concatenate([w_decay_up[l], zeros_lora], axis=0), a0=row(a0[l]),
        wa_pad=jnp.concatenate([zeros_lora, w_aaa_up[l]], axis=0), w_gate_up=w_gate_up[l],
        k_k=row(k_k[l]), k_a=row(k_a[l]), r_k=row(r_k[l]), ln_x_w=row(ln_x_w[l]), ln_x_b=row(ln_x_b[l]),
        w_out_bf16=_cast_bf16(w_out[l]), norm2_g=row(norm2_g[l]),
        wr_hi=wr_hi, wr_lo=(w_router - wr_hi.astype(F32)).astype(BF16),
        b_router=jnp.pad(b_router, (0, LANES - n_route)).reshape(1, LANES),
    )

    n_c = bp + bs
    c_all = jnp.concatenate([c_prompt, c_sample], axis=0)
    c_all = jnp.pad(c_all, ((0, (-n_c) % SUBLANES), (0, 0)))
    mod = _ada(c_all, w_ada[l], row(b_ada[l]))
    mod_p = mod[:bp].reshape(bp, 1, N_MOD * d)
    mod_s = mod[bp:n_c].reshape(bs, 1, N_MOD * d)

    zc = jnp.zeros((bp, 2, D_CONV), F32)
    zs = jnp.zeros((bp, 1, D_RWKV_PROJ), F32)
    zw = jnp.zeros((bp, N_HEADS, HEAD_DIM, HEAD_DIM), F32)
    n_p = bp * tp
    x1_s, h2_s, eid_s, wts_s, conv_s, shift_s, wkv_s = _stream_mix(
        x_sample, mod_s, state_conv[l], state_shift[l].reshape(bs, 1, D_RWKV_PROJ), state_wkv[l], lp,
        nb=ROW_TILE // ts, tb=ts, tt=ts)
    x1_p, h2, eid_p, wts_p, conv_p, shift_p, wkv_p = _stream_mix(
        x_prompt, mod_p, zc, zs, zw, lp, nb=1, tb=ROW_TILE, tt=128, h2_tail=h2_s)

    eid = jnp.concatenate([eid_p[:, :2], eid_s[:, :2]], axis=0)
    dest, plan = _dispatch_plan(eid)
    ys = _experts(plan, h2, w_gate[l], w_up[l], w_down[l])
    y_prompt = _combine(dest[:2 * n_p], wts_p, x1_p, mod_p, row(final_g), ys, bp, tp, nb=1, tb=ROW_TILE)
    y_sample = _combine(dest[2 * n_p:], wts_s, x1_s, mod_s, row(final_g), ys, bs, ts, nb=ROW_TILE // ts, tb=ts)

    return (y_prompt, y_sample, conv_p[None], shift_p[None], wkv_p[None],
            conv_s[None], shift_s[None], wkv_s[None])
```

```python
import functools

import jax
import jax.numpy as jnp
from jax import lax
from jax.experimental import pallas as pl
from jax.experimental.pallas import tpu as pltpu

F32 = jnp.float32
BF16 = jnp.bfloat16
I32 = jnp.int32

D_MODEL = 2048
D_CONV = 1024
D_RWKV = 1024
HEAD_DIM = 64
N_HEADS = 16
LORA_DECAY = 64
LORA_AAA = 64
LORA_GATE = 128
D_RWKV_PROJ = 3 * D_RWKV + LORA_DECAY + LORA_AAA + LORA_GATE
D_IN_PROJ = 3 * D_CONV + D_RWKV_PROJ
N_GROUPS = 4
EXPERTS_PER_GROUP = 8
N_EXPERTS = 32
D_EXPERT = 512
N_MOD = 6
RMS_EPS = 1e-6
GN_EPS = HEAD_DIM * 1e-5

SUBLANES = 8
LANES = 128
VMEM_LIMIT_BYTES = 56 * 1024 * 1024

ROW_TILE = 256
EXPERT_TILE = 256
SEG = 256
D_TILES = D_MODEL // LANES
R_TILES = D_RWKV // LANES
GATHER_PITCH = D_TILES + SUBLANES
GATHER_DMA_PRIORITY = 0
WEIGHT_DMA_PRIORITY = 1

WKV_BATCHES = 4
V_LO = LANES // (WKV_BATCHES * R_TILES)
V_HI = HEAD_DIM // V_LO
N_WKV_OPERANDS = 6
STATE_PITCH = HEAD_DIM + SUBLANES


def _params(semantics):
    return pltpu.CompilerParams(dimension_semantics=semantics, vmem_limit_bytes=VMEM_LIMIT_BYTES)


def _sigmoid(x):
    return 1.0 / (1.0 + jnp.exp(-x))


def _store_row_tiled(ref, x, pitch):
    rows = x.shape[0]
    for s in range(x.shape[1] // LANES):
        ref[pl.ds(s, rows, stride=pitch), :] = x[:, s * LANES:(s + 1) * LANES]


def _load_row_tiled(ref, rows, n, pitch):
    return jnp.concatenate([ref[pl.ds(s, rows, stride=pitch), :] for s in range(n)], axis=-1)


def _head_ones():
    r = lax.broadcasted_iota(I32, (SEG, SEG), 0) // HEAD_DIM
    c = lax.broadcasted_iota(I32, (SEG, SEG), 1) // HEAD_DIM
    return jnp.where(r == c, 1.0, 0.0).astype(BF16)


def _split_bf16(x):
    hi = x.astype(BF16)
    return hi, (x - hi.astype(F32)).astype(BF16)


def _head_sum(x, ones):
    outs = []
    for s in range(x.shape[-1] // SEG):
        hi, lo = _split_bf16(x[:, s * SEG:(s + 1) * SEG])
        outs.append(jnp.dot(hi, ones, preferred_element_type=F32) + jnp.dot(lo, ones, preferred_element_type=F32))
    return jnp.concatenate(outs, axis=-1)


def _ada_kernel(c_ref, w_ref, b_ref, o_ref):
    c = c_ref[...]
    s = c * _sigmoid(c)
    o_ref[...] = jnp.dot(s.astype(BF16), w_ref[...].astype(BF16), preferred_element_type=F32) + b_ref[...]


def _ada(c_all, w_ada, b_ada):
    mb = c_all.shape[0]
    n = w_ada.shape[1]
    tn = 1024
    return pl.pallas_call(
        _ada_kernel,
        grid=(n // tn,),
        in_specs=[pl.BlockSpec((mb, D_MODEL), lambda j: (0, 0)),
                  pl.BlockSpec((D_MODEL, tn), lambda j: (0, j)),
                  pl.BlockSpec((1, tn), lambda j: (0, j))],
        out_specs=pl.BlockSpec((mb, tn), lambda j: (0, j)),
        out_shape=jax.ShapeDtypeStruct((mb, n), F32),
        compiler_params=_params(("arbitrary",)),
        name="ada",
    )(c_all, w_ada, b_ada)


def _norm1_kernel(x_ref, g_ref, sh_ref, sc_ref, o_ref):
    x = x_ref[...]
    nb, tb, d = x.shape
    ms = jnp.mean(x * x, axis=-1, keepdims=True)
    xn = x * lax.rsqrt(ms + RMS_EPS) * g_ref[...]
    h = xn * (1.0 + sc_ref[...]) + sh_ref[...]
    o_ref[...] = h.reshape(nb * tb, d).astype(BF16)


def _norm1(x, mod, g, nb, tb):
    b, t, d = x.shape
    nt = t // tb
    return pl.pallas_call(
        _norm1_kernel,
        grid=(b // nb, nt),
        in_specs=[pl.BlockSpec((nb, tb, d), lambda i, j: (i, j, 0)),
                  pl.BlockSpec((1, d), lambda i, j: (0, 0)),
                  pl.BlockSpec((nb, 1, d), lambda i, j: (i, 0, 0)),
                  pl.BlockSpec((nb, 1, d), lambda i, j: (i, 0, 1))],
        out_specs=pl.BlockSpec((nb * tb, d), lambda i, j: (i * nt + j, 0)),
        out_shape=jax.ShapeDtypeStruct((b * t, d), BF16),
        compiler_params=_params(("arbitrary", "arbitrary")),
        name="norm1",
    )(x, g, mod, mod)


def _inproj_kernel(h_ref, w_ref, o_ref, wb_ref):
    @pl.when(pl.program_id(1) == 0)
    def _():
        wb_ref[...] = w_ref[...].astype(BF16)

    o_ref[...] = jnp.dot(h_ref[...], wb_ref[...], preferred_element_type=F32)


def _inproj(h, w_in):
    m, k = h.shape
    n = w_in.shape[1]
    tm = min(m, 1024)
    tn = 1280
    return pl.pallas_call(
        _inproj_kernel,
        grid=(n // tn, m // tm),
        in_specs=[pl.BlockSpec((tm, k), lambda j, i: (i, 0)),
                  pl.BlockSpec((k, tn), lambda j, i: (0, j))],
        out_specs=pl.BlockSpec((tm, tn), lambda j, i: (i, j)),
        out_shape=jax.ShapeDtypeStruct((m, n), F32),
        scratch_shapes=[pltpu.VMEM((k, tn), BF16)],
        compiler_params=_params(("arbitrary", "arbitrary")),
        name="inproj",
    )(h, w_in)


def _store_time_major(ref, x, stage, nb, tb):
    pitch = tb * R_TILES + SUBLANES
    for bb in range(nb):
        xb = x[bb * tb:(bb + 1) * tb]
        for s in range(R_TILES):
            stage[pl.ds(bb * pitch + s, tb, stride=R_TILES), :] = xb[:, s * LANES:(s + 1) * LANES]
    for t in range(tb):
        for s in range(R_TILES):
            ref[t, s] = stage[pl.ds(t * R_TILES + s, nb, stride=pitch), :]


def _load_time_major(ref, stage, nb, tb):
    pitch = tb * R_TILES + SUBLANES
    for t in range(tb):
        for s in range(R_TILES):
            stage[pl.ds(t * R_TILES + s, nb, stride=pitch), :] = ref[t, s]
    rows = []
    for bb in range(nb):
        rows.append(jnp.concatenate(
            [stage[pl.ds(bb * pitch + s, tb, stride=R_TILES), :] for s in range(R_TILES)], axis=1))
    return jnp.concatenate(rows, axis=0)


def _mixprep_kernel(time_major, proj_ref, cst_ref, sst_ref, convw_ref, mu_ref, w0_ref, wd_ref, a0_ref, wa_ref,
                    wg_ref, kk_ref, ka_ref, rk_ref,
                    a_o, w_o, b_o, k_o, r_o, v_o, yc_o, g_o, bonus_o, nc_o, ns_o, *stage):
    @pl.when(pl.program_id(1) == 0)
    def _():
        nc_o[...] = cst_ref[...]
        ns_o[...] = sst_ref[...]

    nb, tb, _ = proj_ref.shape
    rows = nb * tb
    p = proj_ref[...].reshape(rows, D_IN_PROJ)
    tidx = lax.broadcasted_iota(I32, (rows, 1), 0) & (tb - 1)

    def per_row(c):
        return jnp.broadcast_to(c, (nb, tb, c.shape[-1])).reshape(rows, c.shape[-1])

    gate_b = p[:, 0:D_CONV]
    gate_c = p[:, D_CONV:2 * D_CONV]
    h_conv = p[:, 2 * D_CONV:3 * D_CONV]
    pr = p[:, 3 * D_CONV:]

    u = gate_c * h_conv
    carry = nc_o[...]
    c2 = per_row(carry[:, 0:1, :])
    c1 = per_row(carry[:, 1:2, :])
    u1 = jnp.where(tidx == 0, c1, pltpu.roll(u, 1, 0))
    u2 = jnp.where(tidx == 0, c2, jnp.where(tidx == 1, c1, pltpu.roll(u, 2, 0)))
    cw = convw_ref[...]
    conv_y = u * cw[2:3, :] + u2 * cw[0:1, :] + u1 * cw[1:2, :]
    yc_o[...] = (gate_b * conv_y).astype(BF16)
    nc_o[...] = u.reshape(nb, tb, D_CONV)[:, tb - 2:tb, :]

    p_prev = jnp.where(tidx == 0, per_row(ns_o[...]), pltpu.roll(pr, 1, 0))
    ns_o[...] = pr.reshape(nb, tb, D_RWKV_PROJ)[:, tb - 1:tb, :]
    pm = pr + (p_prev - pr) * mu_ref[...]
    r = pm[:, 0:D_RWKV]
    k = pm[:, D_RWKV:2 * D_RWKV]
    v = pm[:, 2 * D_RWKV:3 * D_RWKV]
    x_lora = pm[:, 3 * D_RWKV:3 * D_RWKV + LORA_DECAY + LORA_AAA]
    xg = pm[:, 3 * D_RWKV + LORA_DECAY + LORA_AAA:]

    d_pre = w0_ref[...] + jnp.dot(jnp.tanh(x_lora).astype(BF16), wd_ref[...].astype(BF16),
                                  preferred_element_type=F32)
    a_pre = a0_ref[...] + jnp.dot(x_lora.astype(BF16), wa_ref[...].astype(BF16), preferred_element_type=F32)
    g = jnp.dot(_sigmoid(xg).astype(BF16), wg_ref[...].astype(BF16), preferred_element_type=F32)
    z = -d_pre
    softplus = jnp.maximum(z, 0.0) + jnp.log(1.0 + jnp.exp(-jnp.abs(z)))
    w_log = -softplus - 0.5
    decay = jnp.exp(-jnp.exp(w_log))
    a_h = _sigmoid(a_pre)

    ones = _head_ones()
    kk = k * kk_ref[...]
    kk = kk * lax.rsqrt(jnp.maximum(_head_sum(kk * kk, ones), 1e-24))
    k_h = k * (1.0 + (a_h - 1.0) * ka_ref[...])
    bonus = _head_sum(r * k_h * rk_ref[...], ones) * v

    operands = ((a_o, -kk), (w_o, decay), (b_o, kk * a_h), (k_o, k_h), (r_o, r), (v_o, v))
    for ref, val in operands:
        if time_major:
            _store_time_major(ref, val, stage[0], nb, tb)
        else:
            _store_row_tiled(ref, val, R_TILES)
    g_o[...] = g
    bonus_o[...] = bonus


def _mixprep(proj3, conv_state, shift_state, lp, nb, tb, time_major):
    b, t, _ = proj3.shape
    nt = t // tb
    rows = nb * tb
    m = b * t
    assert not time_major or nt == 1

    def row_spec(width):
        return pl.BlockSpec((rows, width), lambda i, j: (i * nt + j, 0))

    def full2(a):
        return pl.BlockSpec(a.shape, lambda i, j: (0, 0))

    consts = (lp["conv_w"], lp["mix_mu"], lp["w0"], lp["wd_pad"], lp["a0"], lp["wa_pad"], lp["w_gate_up"],
              lp["k_k"], lp["k_a"], lp["r_k"])
    if time_major:
        tiled_spec = pl.BlockSpec((tb, R_TILES, nb, LANES), lambda i, j: (0, 0, i, 0))
        tiled_out = jax.ShapeDtypeStruct((t, R_TILES, b, LANES), F32)
        scratch = [pltpu.VMEM((nb * (tb * R_TILES + SUBLANES), LANES), F32)]
    else:
        tiled_spec = pl.BlockSpec((rows * R_TILES, LANES), lambda i, j: (i * nt + j, 0))
        tiled_out = jax.ShapeDtypeStruct((m * R_TILES, LANES), F32)
        scratch = []
    f32_out = jax.ShapeDtypeStruct((m, D_RWKV), F32)
    return pl.pallas_call(
        functools.partial(_mixprep_kernel, time_major),
        scratch_shapes=scratch,
        grid=(b // nb, nt),
        in_specs=[pl.BlockSpec((nb, tb, D_IN_PROJ), lambda i, j: (i, j, 0)),
                  pl.BlockSpec((nb, 2, D_CONV), lambda i, j: (i, 0, 0)),
                  pl.BlockSpec((nb, 1, D_RWKV_PROJ), lambda i, j: (i, 0, 0))] + [full2(c) for c in consts],
        out_specs=[tiled_spec] * N_WKV_OPERANDS + [row_spec(D_CONV), row_spec(D_RWKV), row_spec(D_RWKV),
                                                   pl.BlockSpec((nb, 2, D_CONV), lambda i, j: (i, 0, 0)),
                                                   pl.BlockSpec((nb, 1, D_RWKV_PROJ), lambda i, j: (i, 0, 0))],
        out_shape=[tiled_out] * N_WKV_OPERANDS + [jax.ShapeDtypeStruct((m, D_CONV), BF16), f32_out, f32_out,
                                                  jax.ShapeDtypeStruct((b, 2, D_CONV), F32),
                                                  jax.ShapeDtypeStruct((b, 1, D_RWKV_PROJ), F32)],
        compiler_params=_params(("arbitrary", "arbitrary")),
        name="mixprep",
    )(proj3, conv_state, shift_state, *consts)


def _wkv_kernel(a_ref, w_ref, b_ref, k_ref, r_ref, v_ref, s0_ref, y_ref, sout_ref, s_ref, tiles, yq, cbuf):
    n_pairs = WKV_BATCHES * R_TILES

    @pl.when(pl.program_id(1) == 0)
    def _():
        for bp in range(n_pairs):
            bb, hp = divmod(bp, R_TILES)
            cbuf[pl.ds(bp * STATE_PITCH, HEAD_DIM), :] = jnp.concatenate(
                [s0_ref[bb, 2 * hp], s0_ref[bb, 2 * hp + 1]], axis=1)
        for vh in range(V_HI):
            m = jnp.concatenate([cbuf[pl.ds(vh * V_LO + vl, n_pairs, stride=STATE_PITCH), :]
                                 for vl in range(V_LO)], axis=0)
            mt = m.T
            s_ref[0, vh] = mt[:HEAD_DIM]
            s_ref[1, vh] = mt[HEAD_DIM:]

    tt = a_ref.shape[1] // R_TILES
    srcs = (a_ref, w_ref, b_ref, k_ref, r_ref, v_ref)
    n_chain = WKV_BATCHES * R_TILES
    sub8 = lax.broadcasted_iota(I32, (SUBLANES, LANES), 0)
    grp8 = lax.broadcasted_iota(I32, (SUBLANES, LANES), 1) // n_chain
    diag8 = (sub8 % V_LO) == grp8
    sub4 = lax.broadcasted_iota(I32, (V_LO, LANES), 0)
    grp4 = lax.broadcasted_iota(I32, (V_LO, LANES), 1) // n_chain
    diag4 = sub4 == grp4

    def produce(t, slot):
        row0 = pl.multiple_of(t * R_TILES, R_TILES)
        for ai, ref in enumerate(srcs):
            m = jnp.concatenate([ref[bb, pl.ds(row0, R_TILES), :] for bb in range(WKV_BATCHES)], axis=0)
            xt = jnp.concatenate([m] * V_LO, axis=0).T
            if ai < N_WKV_OPERANDS - 1:
                tiles[slot, ai] = xt
            else:
                for q in range(LANES // SUBLANES):
                    z = jnp.where(diag8, xt[q * SUBLANES:(q + 1) * SUBLANES], 0.0)
                    z = z + pltpu.roll(z, 1, 0)
                    z = z + pltpu.roll(z, 2, 0)
                    tiles[slot, ai, q * SUBLANES:(q + 1) * SUBLANES, :] = z

    def consume(slot):
        for par in range(2):
            lo = par * HEAD_DIM
            a = tiles[slot, 0, lo:lo + HEAD_DIM, :]
            w = tiles[slot, 1, lo:lo + HEAD_DIM, :]
            b = tiles[slot, 2, lo:lo + HEAD_DIM, :]
            k = tiles[slot, 3, lo:lo + HEAD_DIM, :]
            r = tiles[slot, 4, lo:lo + HEAD_DIM, :]
            for vh in range(V_HI):
                s = s_ref[par, vh]
                vrow = tiles[slot, 5, lo + vh * V_LO + V_LO - 1:lo + (vh + 1) * V_LO, :]
                sa = jnp.sum(s * a, axis=0, keepdims=True)
                s = s * w + sa * b + vrow * k
                s_ref[par, vh] = s
                yrow = jnp.sum(s * r, axis=0, keepdims=True)
                yq[slot, lo + vh * V_LO:lo + (vh + 1) * V_LO, :] = jnp.where(
                    diag4, jnp.broadcast_to(yrow, (V_LO, LANES)), 0.0)

    def finalize(t, slot):
        qt = yq[slot].T
        res = qt[0:n_chain]
        for j in range(1, V_LO):
            res = res + qt[j * n_chain:(j + 1) * n_chain]
        row0 = pl.multiple_of(t * R_TILES, R_TILES)
        for bb in range(WKV_BATCHES):
            y_ref[bb, pl.ds(row0, R_TILES), :] = res[bb * R_TILES:(bb + 1) * R_TILES]

    produce(0, 0)
    produce(1, 1)
    consume(0)

    def body(i, c):
        t1 = 2 * i + 1
        produce(t1 + 1, 0)
        consume(1)
        finalize(t1 - 1, 0)
        produce(jnp.minimum(t1 + 2, tt - 1), 1)
        consume(0)
        finalize(t1, 1)
        return c

    lax.fori_loop(0, tt // 2 - 1, body, 0)
    consume(1)
    finalize(tt - 2, 0)
    finalize(tt - 1, 1)

    @pl.when(pl.program_id(1) == pl.num_programs(1) - 1)
    def _():
        for vh in range(V_HI):
            m = jnp.concatenate([s_ref[0, vh], s_ref[1, vh]], axis=0).T
            for vl in range(V_LO):
                cbuf[pl.ds(vh * V_LO + vl, n_pairs, stride=STATE_PITCH), :] = m[vl * n_pairs:(vl + 1) * n_pairs]
        for bp in range(n_pairs):
            bb, hp = divmod(bp, R_TILES)
            c = cbuf[pl.ds(bp * STATE_PITCH, HEAD_DIM), :]
            sout_ref[bb, 2 * hp] = c[:, :HEAD_DIM]
            sout_ref[bb, 2 * hp + 1] = c[:, HEAD_DIM:]


def _wkv(operands, s0, tt):
    bsz, rows, _ = operands[0].shape
    t = rows // R_TILES
    g = bsz // WKV_BATCHES
    assert tt % 2 == 0 and tt >= 4 and t % tt == 0
    blk = pl.BlockSpec((WKV_BATCHES, tt * R_TILES, LANES), lambda i, j: (i, j, 0))
    state = pl.BlockSpec((WKV_BATCHES, N_HEADS, HEAD_DIM, HEAD_DIM), lambda i, j: (i, 0, 0, 0))
    return pl.pallas_call(
        _wkv_kernel,
        grid=(g, t // tt),
        in_specs=[blk] * N_WKV_OPERANDS + [state],
        out_specs=[blk, state],
        out_shape=[jax.ShapeDtypeStruct((bsz, rows, LANES), F32),
                   jax.ShapeDtypeStruct((bsz, N_HEADS, HEAD_DIM, HEAD_DIM), F32)],
        scratch_shapes=[pltpu.VMEM((2, V_HI, HEAD_DIM, LANES), F32),
                        pltpu.VMEM((2, N_WKV_OPERANDS, LANES, LANES), F32), pltpu.VMEM((2, LANES, LANES), F32),
                        pltpu.VMEM((WKV_BATCHES * R_TILES * STATE_PITCH, LANES), F32)],
        compiler_params=_params(("arbitrary", "arbitrary")),
        name="wkv",
    )(*operands, s0)


def _wkv_batch_kernel(a_ref, w_ref, b_ref, k_ref, r_ref, v_ref, s0_ref, y_ref, s_ref, yq):
    s_ref[...] = s0_ref[...]
    tt = a_ref.shape[0]

    def step(t, carry):
        a, w, b, k, r, v = (ref[t, 0].T for ref in (a_ref, w_ref, b_ref, k_ref, r_ref, v_ref))
        for par in range(2):
            lo = par * HEAD_DIM
            ap, wp, bp, kp, rp = (z[lo:lo + HEAD_DIM] for z in (a, w, b, k, r))
            for vi in range(HEAD_DIM):
                s = s_ref[par, vi]
                sa = jnp.sum(s * ap, axis=0, keepdims=True)
                s = s * wp + sa * bp + v[lo + vi:lo + vi + 1] * kp
                s_ref[par, vi] = s
                yq[lo + vi:lo + vi + 1, :] = jnp.sum(s * rp, axis=0, keepdims=True)
        y_ref[t, 0] = yq[...].T
        return carry

    lax.fori_loop(0, tt, step, 0)


def _wkv_batch(operands, s0):
    t = operands[0].shape[0]
    assert operands[0].shape[2] == LANES and s0.shape == (N_HEADS, HEAD_DIM, HEAD_DIM, LANES)
    blk = pl.BlockSpec((t, 1, LANES, LANES), lambda i: (0, i, 0, 0))
    state = pl.BlockSpec((2, HEAD_DIM, HEAD_DIM, LANES), lambda i: (i, 0, 0, 0))
    return pl.pallas_call(
        _wkv_batch_kernel,
        grid=(R_TILES,),
        in_specs=[blk] * N_WKV_OPERANDS + [state],
        out_specs=[blk, state],
        out_shape=[jax.ShapeDtypeStruct((t, R_TILES, LANES, LANES), F32),
                   jax.ShapeDtypeStruct((N_HEADS, HEAD_DIM, HEAD_DIM, LANES), F32)],
        scratch_shapes=[pltpu.VMEM((LANES, LANES), F32)],
        compiler_params=_params(("arbitrary",)),
        name="wkv_batch",
    )(*operands, s0)


def _group_norm_gate(y, bonus, g, ln_w, ln_b):
    ones = _head_ones()
    mu = _head_sum(y, ones) * (1.0 / HEAD_DIM)
    d = y - mu
    var = _head_sum(d * d, ones) * (1.0 / HEAD_DIM)
    yn = d * lax.rsqrt(var + GN_EPS)
    yn = yn * ln_w + ln_b
    return (yn + bonus) * g


def _cast_kernel(x_ref, o_ref):
    o_ref[...] = x_ref[...].astype(o_ref.dtype)


def _cast_bf16(w):
    r, c = w.shape
    tr = 512
    return pl.pallas_call(
        _cast_kernel,
        grid=(r // tr,),
        in_specs=[pl.BlockSpec((tr, c), lambda i: (i, 0))],
        out_specs=pl.BlockSpec((tr, c), lambda i: (i, 0)),
        out_shape=jax.ShapeDtypeStruct((r, c), BF16),
        compiler_params=_params(("arbitrary",)),
        name="cast_bf16",
    )(w)


def _route(lg):
    lane = lax.broadcasted_iota(I32, lg.shape, 1)
    neg = -jnp.inf
    big = jnp.int32(1 << 20)

    is_g = lane < N_GROUPS
    gl = jnp.where(is_g, lg, neg)
    gmax = jnp.max(gl, axis=1, keepdims=True)
    gidx = jnp.min(jnp.where(gl == gmax, lane, big), axis=1, keepdims=True)
    gsum = jnp.sum(jnp.where(is_g, jnp.exp(gl - gmax), 0.0), axis=1, keepdims=True)
    gprob = 1.0 / gsum

    e_lane = lane - N_GROUPS
    in_grp = (e_lane >= 0) & (e_lane < N_EXPERTS) & ((e_lane >> 3) == gidx)
    el = jnp.where(in_grp, lg, neg)
    emax = jnp.max(el, axis=1, keepdims=True)
    ex = jnp.where(in_grp, jnp.exp(el - emax), 0.0)
    prob = ex / jnp.sum(ex, axis=1, keepdims=True)
    prob = jnp.where(in_grp, prob, -1.0)
    p1 = jnp.max(prob, axis=1, keepdims=True)
    i1 = jnp.min(jnp.where(prob == p1, lane, big), axis=1, keepdims=True)
    prob2 = jnp.where(lane == i1, -1.0, prob)
    p2 = jnp.max(prob2, axis=1, keepdims=True)
    i2 = jnp.min(jnp.where(prob2 == p2, lane, big), axis=1, keepdims=True)
    psum = p1 + p2
    w1 = p1 / psum * gprob
    w2 = p2 / psum * gprob
    eid = jnp.where(lane == 0, i1 - N_GROUPS, jnp.where(lane == 1, i2 - N_GROUPS, 0))
    wts = jnp.where(lane == 0, w1, jnp.where(lane == 1, w2, 0.0))
    return eid, wts


def _outproj_kernel(n_main, has_tail, time_major, yc_ref, y_ref, bonus_ref, gate_ref, lnw_ref, lnb_ref, wt_ref,
                    wb_ref, x_ref, ga1_ref, sh2_ref, sc2_ref, g2_ref, wrh_ref, wrl_ref, br_ref, *rest):
    rest = list(rest)
    tail_ref = rest.pop(0) if has_tail else None
    x1_o, h2_o, eid_o, wts_o = rest[:4]
    i = pl.program_id(0)

    @pl.when(i < n_main)
    def _():
        nb, tb, d = x_ref.shape
        rows = nb * tb
        if time_major:
            y = _load_time_major(y_ref, rest[4], nb, tb)
        else:
            y = _load_row_tiled(y_ref, rows, R_TILES, R_TILES)
        yr = _group_norm_gate(y, bonus_ref[...], gate_ref[...], lnw_ref[...], lnb_ref[...]).astype(BF16)
        mix = (jnp.dot(yc_ref[...], wt_ref[...], preferred_element_type=F32)
               + jnp.dot(yr, wb_ref[...], preferred_element_type=F32))
        x1 = x_ref[...] + ga1_ref[...] * mix.reshape(nb, tb, d)
        ms = jnp.mean(x1 * x1, axis=-1, keepdims=True)
        h2 = x1 * lax.rsqrt(ms + RMS_EPS) * g2_ref[...]
        h2 = (h2 * (1.0 + sc2_ref[...]) + sh2_ref[...]).reshape(rows, d)
        x1_o[...] = x1.reshape(rows, d)
        _store_row_tiled(h2_o, h2, D_TILES)
        h_hi, h_lo = _split_bf16(h2)
        w_hi = wrh_ref[...]
        logits = (jnp.dot(h_hi, w_hi, preferred_element_type=F32)
                  + jnp.dot(h_lo, w_hi, preferred_element_type=F32)
                  + jnp.dot(h_hi, wrl_ref[...], preferred_element_type=F32)) + br_ref[...]
        eid_o[...], wts_o[...] = _route(logits)

    if has_tail:
        @pl.when(i >= n_main)
        def _():
            h2_o[...] = tail_ref[...]


def _outproj(yc, y_wkv, bonus, gate, ln_w, ln_b, w_out_bf16, x, mod, g2, wr_hi, wr_lo, b_router, nb, tb,
             time_major, h2_tail=None):
    b, t, d = x.shape
    nt = t // tb
    rows = nb * tb
    m = b * t
    half = D_CONV
    n_main = (b // nb) * nt
    n_tail = 0 if h2_tail is None else h2_tail.shape[0] // (rows * D_TILES)
    blk = lambda i: jnp.minimum(i, n_main - 1)
    row = lambda width: pl.BlockSpec((rows, width), lambda i: (blk(i), 0))
    modspec = lambda col: pl.BlockSpec((nb, 1, d), lambda i: (blk(i) // nt, 0, col))
    full = lambda a: pl.BlockSpec(a.shape, lambda i: (0, 0))
    if time_major:
        assert nt == 1 and h2_tail is None
        y_spec = pl.BlockSpec((tb, R_TILES, nb, LANES), lambda i: (0, 0, blk(i), 0))
        scratch = [pltpu.VMEM((nb * (tb * R_TILES + SUBLANES), LANES), F32)]
    else:
        y_spec = pl.BlockSpec((rows * R_TILES, LANES), lambda i: (blk(i), 0))
        scratch = []
    in_specs = [row(half), y_spec, row(half), row(half),
                full(ln_w), full(ln_b),
                pl.BlockSpec((half, d), lambda i: (0, 0)),
                pl.BlockSpec((half, d), lambda i: (1, 0)),
                pl.BlockSpec((nb, tb, d), lambda i: (blk(i) // nt, blk(i) % nt, 0)),
                modspec(2), modspec(3), modspec(4),
                full(g2), full(wr_hi), full(wr_lo), full(b_router)]
    args = [yc, y_wkv, bonus, gate, ln_w, ln_b, w_out_bf16, w_out_bf16, x, mod, mod, mod, g2, wr_hi, wr_lo,
            b_router]
    if h2_tail is not None:
        in_specs.append(pl.BlockSpec((rows * D_TILES, LANES), lambda i: (jnp.maximum(i - n_main, 0), 0)))
        args.append(h2_tail)
    return pl.pallas_call(
        functools.partial(_outproj_kernel, n_main, h2_tail is not None, time_major),
        scratch_shapes=scratch,
        grid=(n_main + n_tail,),
        in_specs=in_specs,
        out_specs=[row(d), pl.BlockSpec((rows * D_TILES, LANES), lambda i: (i, 0)), row(LANES), row(LANES)],
        out_shape=[jax.ShapeDtypeStruct((m, d), F32),
                   jax.ShapeDtypeStruct(((n_main + n_tail) * rows * D_TILES, LANES), F32),
                   jax.ShapeDtypeStruct((m, LANES), I32), jax.ShapeDtypeStruct((m, LANES), F32)],
        compiler_params=_params(("arbitrary",)),
        name="outproj",
    )(*args)


def _row_copy(src_hbm, src_row, dst_ref, dst_row, sem):
    src = src_row * D_TILES
    dst = dst_row * GATHER_PITCH
    if not isinstance(src, int):
        src = pl.multiple_of(src, D_TILES)
    if not isinstance(dst, int):
        dst = pl.multiple_of(dst, SUBLANES)
    return pltpu.make_async_copy(src_hbm.at[pl.ds(src, D_TILES)], dst_ref.at[pl.ds(dst, D_TILES)], sem)


def _start_row_gather(index_of, src_hbm, dst_ref, sem, n):
    for r in range(n):
        _row_copy(src_hbm, index_of(r), dst_ref, r, sem).start(priority=GATHER_DMA_PRIORITY)


def _wait_row_gather(src_hbm, dst_ref, sem, n):
    def body(r, c):
        _row_copy(src_hbm, 0, dst_ref, r, sem).wait()
        return c
    lax.fori_loop(0, n, body, 0, unroll=8)


def _experts_kernel(te_ref, nu_ref, nxt_ref, wsl_ref, idx_ref, idxn_ref, h_hbm, wg_hbm, wu_hbm, wd_hbm, o_ref,
                    xbuf, wgf, wuf, wdf, wgb, wub, wdb, sems, wsems):
    i = pl.program_id(0)
    n_used = nu_ref[0]
    slot = i & 1

    def weight_copies(e, ws):
        return (pltpu.make_async_copy(wg_hbm.at[e], wgf.at[ws], wsems.at[ws]),
                pltpu.make_async_copy(wu_hbm.at[e], wuf.at[ws], wsems.at[ws]),
                pltpu.make_async_copy(wd_hbm.at[e], wdf.at[ws], wsems.at[ws]))

    @pl.when(i == 0)
    def _():
        for c in weight_copies(te_ref[0], wsl_ref[0]):
            c.start(priority=WEIGHT_DMA_PRIORITY)
        _start_row_gather(lambda r: idx_ref[0, r], h_hbm, xbuf.at[0], sems.at[0], EXPERT_TILE)

    @pl.when(i + 1 < n_used)
    def _():
        _start_row_gather(lambda r: idxn_ref[0, r], h_hbm, xbuf.at[1 - slot], sems.at[1 - slot], EXPERT_TILE)

    @pl.when(i < n_used)
    def _():
        new_expert = jnp.logical_or(i == 0, te_ref[i] != te_ref[jnp.maximum(i - 1, 0)])

        @pl.when(new_expert)
        def _():
            ws = wsl_ref[i]
            for c in weight_copies(te_ref[i], ws):
                c.wait()
            wgb[...] = wgf[ws].astype(BF16)
            wub[...] = wuf[ws].astype(BF16)
            wdb[...] = wdf[ws].astype(BF16)

            @pl.when(nxt_ref[i] >= 0)
            def _():
                for c in weight_copies(nxt_ref[i], 1 - ws):
                    c.start(priority=WEIGHT_DMA_PRIORITY)

        _wait_row_gather(h_hbm, xbuf.at[slot], sems.at[slot], EXPERT_TILE)
        x = _load_row_tiled(xbuf.at[slot], EXPERT_TILE, D_TILES, GATHER_PITCH).astype(BF16)
        gate = jnp.dot(x, wgb[...], preferred_element_type=F32)
        up = jnp.dot(x, wub[...], preferred_element_type=F32)
        act = (gate * _sigmoid(gate) * up).astype(BF16)
        _store_row_tiled(o_ref, jnp.dot(act, wdb[...], preferred_element_type=F32), D_TILES)

    @pl.when(i >= n_used)
    def _():
        o_ref[...] = jnp.zeros_like(o_ref)


def _experts(plan, h2_tiled, w_gate, w_up, w_down):
    tile_expert, n_used, next_expert, weight_slot, src_rows = plan
    n_tiles = tile_expert.shape[0]
    d = w_gate.shape[1]
    idx3 = src_rows.reshape(n_tiles, 1, EXPERT_TILE)
    smem_idx = lambda f: pl.BlockSpec((None, 1, EXPERT_TILE), f, memory_space=pltpu.SMEM)
    hbm = pl.BlockSpec(memory_space=pl.ANY)
    grid_spec = pltpu.PrefetchScalarGridSpec(
        num_scalar_prefetch=4,
        grid=(n_tiles,),
        in_specs=[smem_idx(lambda i, *_: (i, 0, 0)),
                  smem_idx(lambda i, *_: (jnp.minimum(i + 1, n_tiles - 1), 0, 0)),
                  hbm, hbm, hbm, hbm],
        out_specs=pl.BlockSpec((EXPERT_TILE * D_TILES, LANES), lambda i, *_: (i, 0)),
        scratch_shapes=[pltpu.VMEM((2, EXPERT_TILE * GATHER_PITCH, LANES), F32),
                        pltpu.VMEM((2, d, D_EXPERT), F32), pltpu.VMEM((2, d, D_EXPERT), F32),
                        pltpu.VMEM((2, D_EXPERT, d), F32),
                        pltpu.VMEM((d, D_EXPERT), BF16), pltpu.VMEM((d, D_EXPERT), BF16),
                        pltpu.VMEM((D_EXPERT, d), BF16),
                        pltpu.SemaphoreType.DMA((2,)), pltpu.SemaphoreType.DMA((2,))],
    )
    return pl.pallas_call(
        _experts_kernel,
        grid_spec=grid_spec,
        out_shape=jax.ShapeDtypeStruct((n_tiles * EXPERT_TILE * D_TILES, LANES), F32),
        compiler_params=_params(("arbitrary",)),
        name="experts",
    )(tile_expert, n_used, next_expert, weight_slot, idx3, idx3, h2_tiled, w_gate, w_up, w_down)


def _combine_kernel(idx_ref, idxn_ref, wt_ref, x1_ref, ga2_ref, gf_ref, ys_hbm, o_ref, ybuf, sems):
    i = pl.program_id(0)
    n_steps = pl.num_programs(0)
    slot = i & 1
    nb, tb, d = o_ref.shape
    rows = nb * tb

    def fetch(idx, s):
        for j in range(2):
            _start_row_gather(lambda r: idx[0, 2 * r + j], ys_hbm, ybuf.at[s, j], sems.at[s], rows)

    @pl.when(i == 0)
    def _():
        fetch(idx_ref, 0)

    @pl.when(i + 1 < n_steps)
    def _():
        fetch(idxn_ref, 1 - slot)

    for j in range(2):
        _wait_row_gather(ys_hbm, ybuf.at[slot, j], sems.at[slot], rows)

    wt = wt_ref[...]
    moe = (wt[:, 0:1] * _load_row_tiled(ybuf.at[slot, 0], rows, D_TILES, GATHER_PITCH)
           + wt[:, 1:2] * _load_row_tiled(ybuf.at[slot, 1], rows, D_TILES, GATHER_PITCH))
    x2 = x1_ref[...].reshape(nb, tb, d) + ga2_ref[...] * moe.reshape(nb, tb, d)
    ms = jnp.mean(x2 * x2, axis=-1, keepdims=True)
    o_ref[...] = x2 * lax.rsqrt(ms + RMS_EPS) * gf_ref[...]


def _combine(dest, wts, x1, mod, final_g, ys, b, t, nb, tb):
    d = x1.shape[1]
    nt = t // tb
    rows = nb * tb
    n_steps = (b // nb) * nt
    idx3 = dest.reshape(n_steps, 1, 2 * rows)
    smem_idx = lambda f: pl.BlockSpec((None, 1, 2 * rows), f, memory_space=pltpu.SMEM)
    return pl.pallas_call(
        _combine_kernel,
        grid=(n_steps,),
        in_specs=[smem_idx(lambda i: (i, 0, 0)),
                  smem_idx(lambda i: (jnp.minimum(i + 1, n_steps - 1), 0, 0)),
                  pl.BlockSpec((rows, LANES), lambda i: (i, 0)),
                  pl.BlockSpec((rows, d), lambda i: (i, 0)),
                  pl.BlockSpec((nb, 1, d), lambda i: (i // nt, 0, 5)),
                  pl.BlockSpec((1, d), lambda i: (0, 0)),
                  pl.BlockSpec(memory_space=pl.ANY)],
        out_specs=pl.BlockSpec((nb, tb, d), lambda i: (i // nt, i % nt, 0)),
        out_shape=jax.ShapeDtypeStruct((b, t, d), F32),
        scratch_shapes=[pltpu.VMEM((2, 2, rows * GATHER_PITCH, LANES), F32), pltpu.SemaphoreType.DMA((2,))],
        compiler_params=_params(("arbitrary",)),
        name="combine",
    )(idx3, idx3, wts, x1, mod, final_g, ys)


def _dispatch_plan(eid):
    n_pairs = eid.shape[0] * 2
    n_tiles = n_pairs // EXPERT_TILE + N_EXPERTS
    flat = eid.reshape(-1)
    onehot = (flat[:, None] == jnp.arange(N_EXPERTS, dtype=I32)[None, :]).astype(I32)
    csum = jnp.cumsum(onehot, axis=0)
    rank = jnp.sum((csum - onehot) * onehot, axis=1)
    counts = csum[-1]
    tiles_per = (counts + EXPERT_TILE - 1) // EXPERT_TILE
    tile_end = jnp.cumsum(tiles_per)
    row_start = (tile_end - tiles_per) * EXPERT_TILE
    dest = jnp.sum(onehot * row_start[None, :], axis=1) + rank
    n_used = tile_end[-1]
    tile_ids = jnp.arange(n_tiles, dtype=I32)
    tile_expert = jnp.sum((tile_ids[:, None] >= tile_end[None, :]).astype(I32), axis=1)
    last_expert = jnp.max(jnp.where(counts > 0, jnp.arange(N_EXPERTS, dtype=I32), 0))
    tile_expert = jnp.where(tile_ids < n_used, tile_expert, last_expert).astype(I32)
    token = jnp.arange(n_pairs, dtype=I32) // 2
    filler = jnp.arange(n_tiles * EXPERT_TILE, dtype=I32) % (n_pairs // 2)
    src_rows = filler.at[dest].set(token)
    experts = jnp.arange(N_EXPERTS, dtype=I32)
    used = counts > 0
    later_used = used[None, :] & (experts[None, :] > experts[:, None])
    next_used = jnp.min(jnp.where(later_used, experts[None, :], N_EXPERTS), axis=1)
    next_used = jnp.where(next_used < N_EXPERTS, next_used, -1).astype(I32)
    slot_of = ((jnp.cumsum(used.astype(I32)) - 1) & 1).astype(I32)
    plan = (tile_expert, n_used.reshape(1).astype(I32), next_used[tile_expert], slot_of[tile_expert], src_rows)
    return dest.astype(I32), plan


def _stream_mix(x, mod, conv_state, shift_state, wkv_state, lp, nb, tb, tt, h2_tail=None):
    b, t, d = x.shape
    g = b // WKV_BATCHES
    h = _norm1(x, mod, lp["norm1_g"], nb, tb)
    proj = _inproj(h, lp["w_in"])
    batch_lanes = b == LANES and tb == t
    outs = _mixprep(proj.reshape(b, t, D_IN_PROJ), conv_state, shift_state, lp, nb, tb, batch_lanes)
    yc, gate, bonus, new_conv, new_shift = outs[N_WKV_OPERANDS:]
    if batch_lanes:
        y_wkv, s_out = _wkv_batch(outs[:N_WKV_OPERANDS], jnp.transpose(wkv_state, (1, 2, 3, 0)))
        new_wkv = jnp.transpose(s_out, (3, 0, 1, 2))
    else:
        operands = [z.reshape(b, t * R_TILES, LANES) for z in outs[:N_WKV_OPERANDS]]
        y_wkv, new_wkv = _wkv(operands, wkv_state, tt)
        y_wkv = y_wkv.reshape(b * t * R_TILES, LANES)
    x1, h2_tiled, eid, wts = _outproj(yc, y_wkv, bonus, gate, lp["ln_x_w"], lp["ln_x_b"], lp["w_out_bf16"], x, mod,
                                      lp["norm2_g"], lp["wr_hi"], lp["wr_lo"], lp["b_router"], nb, tb, batch_lanes,
                                      h2_tail)
    return x1, h2_tiled, eid, wts, new_conv, new_shift.reshape(b, D_RWKV_PROJ), new_wkv


def kernel(x_prompt, x_sample, c_prompt, c_sample, state_conv, state_shift, state_wkv, w_ada, b_ada, norm1_g, w_in, conv_w, mix_mu, w0, w_decay_up, a0, w_aaa_up, w_gate_up, k_k, k_a, r_k, ln_x_w, ln_x_b, w_out, norm2_g, w_router_group, b_router_group, w_router_expert, b_router_expert, w_gate, w_up, w_down, final_g):
    depth = w_ada.shape[0]
    assert depth == 1
    bp, tp, d = x_prompt.shape
    bs, ts, _ = x_sample.shape
    l = 0

    row = lambda z: z.reshape(1, -1)
    zeros_lora = jnp.zeros((LORA_DECAY, D_RWKV), F32)
    n_route = N_GROUPS + N_EXPERTS
    w_router = jnp.pad(jnp.concatenate([w_router_group[l], w_router_expert[l]], axis=1),
                       ((0, 0), (0, LANES - n_route)))
    b_router = jnp.concatenate([b_router_group[l], b_router_expert[l]])
    wr_hi = w_router.astype(BF16)
    lp = dict(
        norm1_g=row(norm1_g[l]), w_in=w_in[l], conv_w=conv_w[l], mix_mu=row(mix_mu[l]), w0=row(w0[l]),
        wd_pad=jnp.concatenate([w_decay_up[l], zeros_lora], axis=0), a0=row(a0[l]),
        wa_pad=jnp.concatenate([zeros_lora, w_aaa_up[l]], axis=0), w_gate_up=w_gate_up[l],
        k_k=row(k_k[l]), k_a=row(k_a[l]), r_k=row(r_k[l]), ln_x_w=row(ln_x_w[l]), ln_x_b=row(ln_x_b[l]),
        w_out_bf16=_cast_bf16(w_out[l]), norm2_g=row(norm2_g[l]),
        wr_hi=wr_hi, wr_lo=(w_router - wr_hi.astype(F32)).astype(BF16),
        b_router=jnp.pad(b_router, (0, LANES - n_route)).reshape(1, LANES),
    )

    n_c = bp + bs
    c_all = jnp.concatenate([c_prompt, c_sample], axis=0)
    c_all = jnp.pad(c_all, ((0, (-n_c) % SUBLANES), (0, 0)))
    mod = _ada(c_all, w_ada[l], row(b_ada[l]))
    mod_p = mod[:bp].reshape(bp, 1, N_MOD * d)
    mod_s = mod[bp:n_c].reshape(bs, 1, N_MOD * d)

    zc = jnp.zeros((bp, 2, D_CONV), F32)
    zs = jnp.zeros((bp, 1, D_RWKV_PROJ), F32)
    zw = jnp.zeros((bp, N_HEADS, HEAD_DIM, HEAD_DIM), F32)
    n_p = bp * tp
    x1_s, h2_s, eid_s, wts_s, conv_s, shift_s, wkv_s = _stream_mix(
        x_sample, mod_s, state_conv[l], state_shift[l].reshape(bs, 1, D_RWKV_PROJ), state_wkv[l], lp,
        nb=ROW_TILE // ts, tb=ts, tt=ts)
    x1_p, h2, eid_p, wts_p, conv_p, shift_p, wkv_p = _stream_mix(
        x_prompt, mod_p, zc, zs, zw, lp, nb=1, tb=ROW_TILE, tt=128, h2_tail=h2_s)

    eid = jnp.concatenate([eid_p[:, :2], eid_s[:, :2]], axis=0)
    dest, plan = _dispatch_plan(eid)
    ys = _experts(plan, h2, w_gate[l], w_up[l], w_down[l])
    y_prompt = _combine(dest[:2 * n_p], wts_p, x1_p, mod_p, row(final_g), ys, bp, tp, nb=1, tb=ROW_TILE)
    y_sample = _combine(dest[2 * n_p:], wts_s, x1_s, mod_s, row(final_g), ys, bs, ts, nb=ROW_TILE // ts, tb=ts)

    return (y_prompt, y_sample, conv_p[None], shift_p[None], wkv_p[None],
            conv_s[None], shift_s[None], wkv_s[None])
```

```python
import functools

import jax
import jax.numpy as jnp
from jax import lax
from jax.experimental import pallas as pl
from jax.experimental.pallas import tpu as pltpu

F32 = jnp.float32
BF16 = jnp.bfloat16
I32 = jnp.int32

D_MODEL = 2048
D_CONV = 1024
D_RWKV = 1024
HEAD_DIM = 64
N_HEADS = 16
LORA_DECAY = 64
LORA_AAA = 64
LORA_GATE = 128
D_RWKV_PROJ = 3 * D_RWKV + LORA_DECAY + LORA_AAA + LORA_GATE
D_IN_PROJ = 3 * D_CONV + D_RWKV_PROJ
N_GROUPS = 4
EXPERTS_PER_GROUP = 8
N_EXPERTS = 32
D_EXPERT = 512
N_MOD = 6
RMS_EPS = 1e-6
GN_EPS = HEAD_DIM * 1e-5

SUBLANES = 8
LANES = 128
VMEM_LIMIT_BYTES = 56 * 1024 * 1024

ROW_TILE = 256
EXPERT_TILE = 256
COMBINE_CHUNKS = 4
EXPERT_ISSUE_GROUPS = 7
SEG = 256
D_TILES = D_MODEL // LANES
R_TILES = D_RWKV // LANES
GATHER_PITCH = D_TILES + SUBLANES
GATHER_DMA_PRIORITY = 0
WEIGHT_DMA_PRIORITY = 1

WKV_BATCHES = 4
V_LO = LANES // (WKV_BATCHES * R_TILES)
V_HI = HEAD_DIM // V_LO
N_WKV_OPERANDS = 6
STATE_PITCH = HEAD_DIM + SUBLANES


def _params(semantics):
    return pltpu.CompilerParams(dimension_semantics=semantics, vmem_limit_bytes=VMEM_LIMIT_BYTES)


def _sigmoid(x):
    return 1.0 / (1.0 + jnp.exp(-x))


def _store_row_tiled(ref, x, pitch):
    rows = x.shape[0]
    for s in range(x.shape[1] // LANES):
        ref[pl.ds(s, rows, stride=pitch), :] = x[:, s * LANES:(s + 1) * LANES]


def _load_row_tiled(ref, rows, n, pitch):
    return jnp.concatenate([ref[pl.ds(s, rows, stride=pitch), :] for s in range(n)], axis=-1)


def _head_ones():
    r = lax.broadcasted_iota(I32, (SEG, SEG), 0) // HEAD_DIM
    c = lax.broadcasted_iota(I32, (SEG, SEG), 1) // HEAD_DIM
    return jnp.where(r == c, 1.0, 0.0).astype(BF16)


def _split_bf16(x):
    hi = x.astype(BF16)
    return hi, (x - hi.astype(F32)).astype(BF16)


def _head_sum(x, ones):
    outs = []
    for s in range(x.shape[-1] // SEG):
        hi, lo = _split_bf16(x[:, s * SEG:(s + 1) * SEG])
        outs.append(jnp.dot(hi, ones, preferred_element_type=F32) + jnp.dot(lo, ones, preferred_element_type=F32))
    return jnp.concatenate(outs, axis=-1)


def _ada_kernel(c_ref, w_ref, b_ref, o_ref):
    c = c_ref[...]
    s = c * _sigmoid(c)
    o_ref[...] = jnp.dot(s.astype(BF16), w_ref[...].astype(BF16), preferred_element_type=F32) + b_ref[...]


def _ada(c_all, w_ada, b_ada):
    mb = c_all.shape[0]
    n = w_ada.shape[1]
    tn = 1024
    return pl.pallas_call(
        _ada_kernel,
        grid=(n // tn,),
        in_specs=[pl.BlockSpec((mb, D_MODEL), lambda j: (0, 0)),
                  pl.BlockSpec((D_MODEL, tn), lambda j: (0, j)),
                  pl.BlockSpec((1, tn), lambda j: (0, j))],
        out_specs=pl.BlockSpec((mb, tn), lambda j: (0, j)),
        out_shape=jax.ShapeDtypeStruct((mb, n), F32),
        compiler_params=_params(("arbitrary",)),
        name="ada",
    )(c_all, w_ada, b_ada)


def _norm1_kernel(x_ref, g_ref, sh_ref, sc_ref, o_ref):
    x = x_ref[...]
    nb, tb, d = x.shape
    ms = jnp.mean(x * x, axis=-1, keepdims=True)
    xn = x * lax.rsqrt(ms + RMS_EPS) * g_ref[...]
    h = xn * (1.0 + sc_ref[...]) + sh_ref[...]
    o_ref[...] = h.reshape(nb * tb, d).astype(BF16)


def _norm1(x, mod, g, nb, tb):
    b, t, d = x.shape
    nt = t // tb
    return pl.pallas_call(
        _norm1_kernel,
        grid=(b // nb, nt),
        in_specs=[pl.BlockSpec((nb, tb, d), lambda i, j: (i, j, 0)),
                  pl.BlockSpec((1, d), lambda i, j: (0, 0)),
                  pl.BlockSpec((nb, 1, d), lambda i, j: (i, 0, 0)),
                  pl.BlockSpec((nb, 1, d), lambda i, j: (i, 0, 1))],
        out_specs=pl.BlockSpec((nb * tb, d), lambda i, j: (i * nt + j, 0)),
        out_shape=jax.ShapeDtypeStruct((b * t, d), BF16),
        compiler_params=_params(("arbitrary", "arbitrary")),
        name="norm1",
    )(x, g, mod, mod)


def _inproj_kernel(h_ref, w_ref, o_ref, wb_ref):
    @pl.when(pl.program_id(1) == 0)
    def _():
        wb_ref[...] = w_ref[...].astype(BF16)

    o_ref[...] = jnp.dot(h_ref[...], wb_ref[...], preferred_element_type=F32)


def _inproj(h, w_in):
    m, k = h.shape
    n = w_in.shape[1]
    tm = min(m, 1024)
    tn = 1280
    return pl.pallas_call(
        _inproj_kernel,
        grid=(n // tn, m // tm),
        in_specs=[pl.BlockSpec((tm, k), lambda j, i: (i, 0)),
                  pl.BlockSpec((k, tn), lambda j, i: (0, j))],
        out_specs=pl.BlockSpec((tm, tn), lambda j, i: (i, j)),
        out_shape=jax.ShapeDtypeStruct((m, n), F32),
        scratch_shapes=[pltpu.VMEM((k, tn), BF16)],
        compiler_params=_params(("arbitrary", "arbitrary")),
        name="inproj",
    )(h, w_in)


def _store_time_major(ref, x, stage, nb, tb):
    pitch = tb * R_TILES + SUBLANES
    for bb in range(nb):
        xb = x[bb * tb:(bb + 1) * tb]
        for s in range(R_TILES):
            stage[pl.ds(bb * pitch + s, tb, stride=R_TILES), :] = xb[:, s * LANES:(s + 1) * LANES]
    for t in range(tb):
        for s in range(R_TILES):
            ref[t, s] = stage[pl.ds(t * R_TILES + s, nb, stride=pitch), :]


def _load_time_major(ref, stage, nb, tb):
    pitch = tb * R_TILES + SUBLANES
    for t in range(tb):
        for s in range(R_TILES):
            stage[pl.ds(t * R_TILES + s, nb, stride=pitch), :] = ref[t, s]
    rows = []
    for bb in range(nb):
        rows.append(jnp.concatenate(
            [stage[pl.ds(bb * pitch + s, tb, stride=R_TILES), :] for s in range(R_TILES)], axis=1))
    return jnp.concatenate(rows, axis=0)


def _mixprep_kernel(time_major, proj_ref, cst_ref, sst_ref, convw_ref, mu_ref, w0_ref, wd_ref, a0_ref, wa_ref,
                    wg_ref, kk_ref, ka_ref, rk_ref,
                    a_o, w_o, b_o, k_o, r_o, v_o, yc_o, g_o, bonus_o, nc_o, ns_o, *stage):
    @pl.when(pl.program_id(1) == 0)
    def _():
        nc_o[...] = cst_ref[...]
        ns_o[...] = sst_ref[...]

    nb, tb, _ = proj_ref.shape
    rows = nb * tb
    p = proj_ref[...].reshape(rows, D_IN_PROJ)
    tidx = lax.broadcasted_iota(I32, (rows, 1), 0) & (tb - 1)

    def per_row(c):
        return jnp.broadcast_to(c, (nb, tb, c.shape[-1])).reshape(rows, c.shape[-1])

    gate_b = p[:, 0:D_CONV]
    gate_c = p[:, D_CONV:2 * D_CONV]
    h_conv = p[:, 2 * D_CONV:3 * D_CONV]
    pr = p[:, 3 * D_CONV:]

    u = gate_c * h_conv
    carry = nc_o[...]
    c2 = per_row(carry[:, 0:1, :])
    c1 = per_row(carry[:, 1:2, :])
    u1 = jnp.where(tidx == 0, c1, pltpu.roll(u, 1, 0))
    u2 = jnp.where(tidx == 0, c2, jnp.where(tidx == 1, c1, pltpu.roll(u, 2, 0)))
    cw = convw_ref[...]
    conv_y = u * cw[2:3, :] + u2 * cw[0:1, :] + u1 * cw[1:2, :]
    yc_o[...] = (gate_b * conv_y).astype(BF16)
    nc_o[...] = u.reshape(nb, tb, D_CONV)[:, tb - 2:tb, :]

    p_prev = jnp.where(tidx == 0, per_row(ns_o[...]), pltpu.roll(pr, 1, 0))
    ns_o[...] = pr.reshape(nb, tb, D_RWKV_PROJ)[:, tb - 1:tb, :]
    pm = pr + (p_prev - pr) * mu_ref[...]
    r = pm[:, 0:D_RWKV]
    k = pm[:, D_RWKV:2 * D_RWKV]
    v = pm[:, 2 * D_RWKV:3 * D_RWKV]
    x_lora = pm[:, 3 * D_RWKV:3 * D_RWKV + LORA_DECAY + LORA_AAA]
    xg = pm[:, 3 * D_RWKV + LORA_DECAY + LORA_AAA:]

    d_pre = w0_ref[...] + jnp.dot(jnp.tanh(x_lora).astype(BF16), wd_ref[...].astype(BF16),
                                  preferred_element_type=F32)
    a_pre = a0_ref[...] + jnp.dot(x_lora.astype(BF16), wa_ref[...].astype(BF16), preferred_element_type=F32)
    g = jnp.dot(_sigmoid(xg).astype(BF16), wg_ref[...].astype(BF16), preferred_element_type=F32)
    z = -d_pre
    softplus = jnp.maximum(z, 0.0) + jnp.log(1.0 + jnp.exp(-jnp.abs(z)))
    w_log = -softplus - 0.5
    decay = jnp.exp(-jnp.exp(w_log))
    a_h = _sigmoid(a_pre)

    ones = _head_ones()
    kk = k * kk_ref[...]
    kk = kk * lax.rsqrt(jnp.maximum(_head_sum(kk * kk, ones), 1e-24))
    k_h = k * (1.0 + (a_h - 1.0) * ka_ref[...])
    bonus = _head_sum(r * k_h * rk_ref[...], ones) * v

    operands = ((a_o, -kk), (w_o, decay), (b_o, kk * a_h), (k_o, k_h), (r_o, r), (v_o, v))
    for ref, val in operands:
        if time_major:
            _store_time_major(ref, val, stage[0], nb, tb)
        else:
            _store_row_tiled(ref, val, R_TILES)
    g_o[...] = g
    bonus_o[...] = bonus


def _mixprep(proj3, conv_state, shift_state, lp, nb, tb, time_major):
    b, t, _ = proj3.shape
    nt = t // tb
    rows = nb * tb
    m = b * t
    assert not time_major or nt == 1

    def row_spec(width):
        return pl.BlockSpec((rows, width), lambda i, j: (i * nt + j, 0))

    def full2(a):
        return pl.BlockSpec(a.shape, lambda i, j: (0, 0))

    consts = (lp["conv_w"], lp["mix_mu"], lp["w0"], lp["wd_pad"], lp["a0"], lp["wa_pad"], lp["w_gate_up"],
              lp["k_k"], lp["k_a"], lp["r_k"])
    if time_major:
        tiled_spec = pl.BlockSpec((tb, R_TILES, nb, LANES), lambda i, j: (0, 0, i, 0))
        tiled_out = jax.ShapeDtypeStruct((t, R_TILES, b, LANES), F32)
        scratch = [pltpu.VMEM((nb * (tb * R_TILES + SUBLANES), LANES), F32)]
    else:
        tiled_spec = pl.BlockSpec((rows * R_TILES, LANES), lambda i, j: (i * nt + j, 0))
        tiled_out = jax.ShapeDtypeStruct((m * R_TILES, LANES), F32)
        scratch = []
    f32_out = jax.ShapeDtypeStruct((m, D_RWKV), F32)
    return pl.pallas_call(
        functools.partial(_mixprep_kernel, time_major),
        scratch_shapes=scratch,
        grid=(b // nb, nt),
        in_specs=[pl.BlockSpec((nb, tb, D_IN_PROJ), lambda i, j: (i, j, 0)),
                  pl.BlockSpec((nb, 2, D_CONV), lambda i, j: (i, 0, 0)),
                  pl.BlockSpec((nb, 1, D_RWKV_PROJ), lambda i, j: (i, 0, 0))] + [full2(c) for c in consts],
        out_specs=[tiled_spec] * N_WKV_OPERANDS + [row_spec(D_CONV), row_spec(D_RWKV), row_spec(D_RWKV),
                                                   pl.BlockSpec((nb, 2, D_CONV), lambda i, j: (i, 0, 0)),
                                                   pl.BlockSpec((nb, 1, D_RWKV_PROJ), lambda i, j: (i, 0, 0))],
        out_shape=[tiled_out] * N_WKV_OPERANDS + [jax.ShapeDtypeStruct((m, D_CONV), BF16), f32_out, f32_out,
                                                  jax.ShapeDtypeStruct((b, 2, D_CONV), F32),
                                                  jax.ShapeDtypeStruct((b, 1, D_RWKV_PROJ), F32)],
        compiler_params=_params(("arbitrary", "arbitrary")),
        name="mixprep",
    )(proj3, conv_state, shift_state, *consts)


def _wkv_kernel(a_ref, w_ref, b_ref, k_ref, r_ref, v_ref, s0_ref, y_ref, sout_ref, s_ref, tiles, yq, cbuf):
    n_pairs = WKV_BATCHES * R_TILES

    @pl.when(pl.program_id(1) == 0)
    def _():
        for bp in range(n_pairs):
            bb, hp = divmod(bp, R_TILES)
            cbuf[pl.ds(bp * STATE_PITCH, HEAD_DIM), :] = jnp.concatenate(
                [s0_ref[bb, 2 * hp], s0_ref[bb, 2 * hp + 1]], axis=1)
        for vh in range(V_HI):
            m = jnp.concatenate([cbuf[pl.ds(vh * V_LO + vl, n_pairs, stride=STATE_PITCH), :]
                                 for vl in range(V_LO)], axis=0)
            mt = m.T
            s_ref[0, vh] = mt[:HEAD_DIM]
            s_ref[1, vh] = mt[HEAD_DIM:]

    tt = a_ref.shape[1] // R_TILES
    srcs = (a_ref, w_ref, b_ref, k_ref, r_ref, v_ref)
    n_chain = WKV_BATCHES * R_TILES
    sub8 = lax.broadcasted_iota(I32, (SUBLANES, LANES), 0)
    grp8 = lax.broadcasted_iota(I32, (SUBLANES, LANES), 1) // n_chain
    diag8 = (sub8 % V_LO) == grp8
    sub4 = lax.broadcasted_iota(I32, (V_LO, LANES), 0)
    grp4 = lax.broadcasted_iota(I32, (V_LO, LANES), 1) // n_chain
    diag4 = sub4 == grp4

    def produce(t, slot):
        row0 = pl.multiple_of(t * R_TILES, R_TILES)
        for ai, ref in enumerate(srcs):
            m = jnp.concatenate([ref[bb, pl.ds(row0, R_TILES), :] for bb in range(WKV_BATCHES)], axis=0)
            xt = jnp.concatenate([m] * V_LO, axis=0).T
            if ai < N_WKV_OPERANDS - 1:
                tiles[slot, ai] = xt
            else:
                for q in range(LANES // SUBLANES):
                    z = jnp.where(diag8, xt[q * SUBLANES:(q + 1) * SUBLANES], 0.0)
                    z = z + pltpu.roll(z, 1, 0)
                    z = z + pltpu.roll(z, 2, 0)
                    tiles[slot, ai, q * SUBLANES:(q + 1) * SUBLANES, :] = z

    def consume(slot):
        for par in range(2):
            lo = par * HEAD_DIM
            a = tiles[slot, 0, lo:lo + HEAD_DIM, :]
            w = tiles[slot, 1, lo:lo + HEAD_DIM, :]
            b = tiles[slot, 2, lo:lo + HEAD_DIM, :]
            k = tiles[slot, 3, lo:lo + HEAD_DIM, :]
            r = tiles[slot, 4, lo:lo + HEAD_DIM, :]
            for vh in range(V_HI):
                s = s_ref[par, vh]
                vrow = tiles[slot, 5, lo + vh * V_LO + V_LO - 1:lo + (vh + 1) * V_LO, :]
                sa = jnp.sum(s * a, axis=0, keepdims=True)
                s = s * w + sa * b + vrow * k
                s_ref[par, vh] = s
                yrow = jnp.sum(s * r, axis=0, keepdims=True)
                yq[slot, lo + vh * V_LO:lo + (vh + 1) * V_LO, :] = jnp.where(
                    diag4, jnp.broadcast_to(yrow, (V_LO, LANES)), 0.0)

    def finalize(t, slot):
        qt = yq[slot].T
        res = qt[0:n_chain]
        for j in range(1, V_LO):
            res = res + qt[j * n_chain:(j + 1) * n_chain]
        row0 = pl.multiple_of(t * R_TILES, R_TILES)
        for bb in range(WKV_BATCHES):
            y_ref[bb, pl.ds(row0, R_TILES), :] = res[bb * R_TILES:(bb + 1) * R_TILES]

    produce(0, 0)
    produce(1, 1)
    consume(0)

    def body(i, c):
        t1 = 2 * i + 1
        produce(t1 + 1, 0)
        consume(1)
        finalize(t1 - 1, 0)
        produce(jnp.minimum(t1 + 2, tt - 1), 1)
        consume(0)
        finalize(t1, 1)
        return c

    lax.fori_loop(0, tt // 2 - 1, body, 0)
    consume(1)
    finalize(tt - 2, 0)
    finalize(tt - 1, 1)

    @pl.when(pl.program_id(1) == pl.num_programs(1) - 1)
    def _():
        for vh in range(V_HI):
            m = jnp.concatenate([s_ref[0, vh], s_ref[1, vh]], axis=0).T
            for vl in range(V_LO):
                cbuf[pl.ds(vh * V_LO + vl, n_pairs, stride=STATE_PITCH), :] = m[vl * n_pairs:(vl + 1) * n_pairs]
        for bp in range(n_pairs):
            bb, hp = divmod(bp, R_TILES)
            c = cbuf[pl.ds(bp * STATE_PITCH, HEAD_DIM), :]
            sout_ref[bb, 2 * hp] = c[:, :HEAD_DIM]
            sout_ref[bb, 2 * hp + 1] = c[:, HEAD_DIM:]


def _wkv(operands, s0, tt):
    bsz, rows, _ = operands[0].shape
    t = rows // R_TILES
    g = bsz // WKV_BATCHES
    assert tt % 2 == 0 and tt >= 4 and t % tt == 0
    blk = pl.BlockSpec((WKV_BATCHES, tt * R_TILES, LANES), lambda i, j: (i, j, 0))
    state = pl.BlockSpec((WKV_BATCHES, N_HEADS, HEAD_DIM, HEAD_DIM), lambda i, j: (i, 0, 0, 0))
    return pl.pallas_call(
        _wkv_kernel,
        grid=(g, t // tt),
        in_specs=[blk] * N_WKV_OPERANDS + [state],
        out_specs=[blk, state],
        out_shape=[jax.ShapeDtypeStruct((bsz, rows, LANES), F32),
                   jax.ShapeDtypeStruct((bsz, N_HEADS, HEAD_DIM, HEAD_DIM), F32)],
        scratch_shapes=[pltpu.VMEM((2, V_HI, HEAD_DIM, LANES), F32),
                        pltpu.VMEM((2, N_WKV_OPERANDS, LANES, LANES), F32), pltpu.VMEM((2, LANES, LANES), F32),
                        pltpu.VMEM((WKV_BATCHES * R_TILES * STATE_PITCH, LANES), F32)],
        compiler_params=_params(("arbitrary", "arbitrary")),
        name="wkv",
    )(*operands, s0)


def _wkv_batch_kernel(a_ref, w_ref, b_ref, k_ref, r_ref, v_ref, s0_ref, y_ref, s_ref, yq):
    s_ref[...] = s0_ref[...]
    tt = a_ref.shape[0]

    def step(t, carry):
        a, w, b, k, r, v = (ref[t, 0].T for ref in (a_ref, w_ref, b_ref, k_ref, r_ref, v_ref))
        for par in range(2):
            lo = par * HEAD_DIM
            ap, wp, bp, kp, rp = (z[lo:lo + HEAD_DIM] for z in (a, w, b, k, r))
            for vi in range(HEAD_DIM):
                s = s_ref[par, vi]
                sa = jnp.sum(s * ap, axis=0, keepdims=True)
                s = s * wp + sa * bp + v[lo + vi:lo + vi + 1] * kp
                s_ref[par, vi] = s
                yq[lo + vi:lo + vi + 1, :] = jnp.sum(s * rp, axis=0, keepdims=True)
        y_ref[t, 0] = yq[...].T
        return carry

    lax.fori_loop(0, tt, step, 0)


def _wkv_batch(operands, s0):
    t = operands[0].shape[0]
    assert operands[0].shape[2] == LANES and s0.shape == (N_HEADS, HEAD_DIM, HEAD_DIM, LANES)
    blk = pl.BlockSpec((t, 1, LANES, LANES), lambda i: (0, i, 0, 0))
    state = pl.BlockSpec((2, HEAD_DIM, HEAD_DIM, LANES), lambda i: (i, 0, 0, 0))
    return pl.pallas_call(
        _wkv_batch_kernel,
        grid=(R_TILES,),
        in_specs=[blk] * N_WKV_OPERANDS + [state],
        out_specs=[blk, state],
        out_shape=[jax.ShapeDtypeStruct((t, R_TILES, LANES, LANES), F32),
                   jax.ShapeDtypeStruct((N_HEADS, HEAD_DIM, HEAD_DIM, LANES), F32)],
        scratch_shapes=[pltpu.VMEM((LANES, LANES), F32)],
        compiler_params=_params(("arbitrary",)),
        name="wkv_batch",
    )(*operands, s0)


def _group_norm_gate(y, bonus, g, ln_w, ln_b):
    ones = _head_ones()
    mu = _head_sum(y, ones) * (1.0 / HEAD_DIM)
    d = y - mu
    var = _head_sum(d * d, ones) * (1.0 / HEAD_DIM)
    yn = d * lax.rsqrt(var + GN_EPS)
    yn = yn * ln_w + ln_b
    return (yn + bonus) * g


def _cast_kernel(x_ref, o_ref):
    o_ref[...] = x_ref[...].astype(o_ref.dtype)


def _cast_bf16(w):
    r, c = w.shape
    tr = 512
    return pl.pallas_call(
        _cast_kernel,
        grid=(r // tr,),
        in_specs=[pl.BlockSpec((tr, c), lambda i: (i, 0))],
        out_specs=pl.BlockSpec((tr, c), lambda i: (i, 0)),
        out_shape=jax.ShapeDtypeStruct((r, c), BF16),
        compiler_params=_params(("arbitrary",)),
        name="cast_bf16",
    )(w)


def _route(lg):
    lane = lax.broadcasted_iota(I32, lg.shape, 1)
    neg = -jnp.inf
    big = jnp.int32(1 << 20)

    is_g = lane < N_GROUPS
    gl = jnp.where(is_g, lg, neg)
    gmax = jnp.max(gl, axis=1, keepdims=True)
    gidx = jnp.min(jnp.where(gl == gmax, lane, big), axis=1, keepdims=True)
    gsum = jnp.sum(jnp.where(is_g, jnp.exp(gl - gmax), 0.0), axis=1, keepdims=True)
    gprob = 1.0 / gsum

    e_lane = lane - N_GROUPS
    in_grp = (e_lane >= 0) & (e_lane < N_EXPERTS) & ((e_lane >> 3) == gidx)
    el = jnp.where(in_grp, lg, neg)
    emax = jnp.max(el, axis=1, keepdims=True)
    ex = jnp.where(in_grp, jnp.exp(el - emax), 0.0)
    prob = ex / jnp.sum(ex, axis=1, keepdims=True)
    prob = jnp.where(in_grp, prob, -1.0)
    p1 = jnp.max(prob, axis=1, keepdims=True)
    i1 = jnp.min(jnp.where(prob == p1, lane, big), axis=1, keepdims=True)
    prob2 = jnp.where(lane == i1, -1.0, prob)
    p2 = jnp.max(prob2, axis=1, keepdims=True)
    i2 = jnp.min(jnp.where(prob2 == p2, lane, big), axis=1, keepdims=True)
    psum = p1 + p2
    w1 = p1 / psum * gprob
    w2 = p2 / psum * gprob
    eid = jnp.where(lane == 0, i1 - N_GROUPS, jnp.where(lane == 1, i2 - N_GROUPS, 0))
    wts = jnp.where(lane == 0, w1, jnp.where(lane == 1, w2, 0.0))
    return eid, wts


def _outproj_kernel(n_main, has_tail, time_major, yc_ref, y_ref, bonus_ref, gate_ref, lnw_ref, lnb_ref, wt_ref,
                    wb_ref, x_ref, ga1_ref, sh2_ref, sc2_ref, g2_ref, wrh_ref, wrl_ref, br_ref, *rest):
    rest = list(rest)
    tail_ref = rest.pop(0) if has_tail else None
    x1_o, h2_o, eid_o, wts_o = rest[:4]
    i = pl.program_id(0)

    @pl.when(i < n_main)
    def _():
        nb, tb, d = x_ref.shape
        rows = nb * tb
        if time_major:
            y = _load_time_major(y_ref, rest[4], nb, tb)
        else:
            y = _load_row_tiled(y_ref, rows, R_TILES, R_TILES)
        yr = _group_norm_gate(y, bonus_ref[...], gate_ref[...], lnw_ref[...], lnb_ref[...]).astype(BF16)
        mix = (jnp.dot(yc_ref[...], wt_ref[...], preferred_element_type=F32)
               + jnp.dot(yr, wb_ref[...], preferred_element_type=F32))
        x1 = x_ref[...] + ga1_ref[...] * mix.reshape(nb, tb, d)
        ms = jnp.mean(x1 * x1, axis=-1, keepdims=True)
        h2 = x1 * lax.rsqrt(ms + RMS_EPS) * g2_ref[...]
        h2 = (h2 * (1.0 + sc2_ref[...]) + sh2_ref[...]).reshape(rows, d)
        x1_o[...] = x1.reshape(rows, d)
        _store_row_tiled(h2_o, h2, D_TILES)
        h_hi, h_lo = _split_bf16(h2)
        w_hi = wrh_ref[...]
        logits = (jnp.dot(h_hi, w_hi, preferred_element_type=F32)
                  + jnp.dot(h_lo, w_hi, preferred_element_type=F32)
                  + jnp.dot(h_hi, wrl_ref[...], preferred_element_type=F32)) + br_ref[...]
        eid_o[...], wts_o[...] = _route(logits)

    if has_tail:
        @pl.when(i >= n_main)
        def _():
            h2_o[...] = tail_ref[...]


def _outproj(yc, y_wkv, bonus, gate, ln_w, ln_b, w_out_bf16, x, mod, g2, wr_hi, wr_lo, b_router, nb, tb,
             time_major, h2_tail=None):
    b, t, d = x.shape
    nt = t // tb
    rows = nb * tb
    m = b * t
    half = D_CONV
    n_main = (b // nb) * nt
    n_tail = 0 if h2_tail is None else h2_tail.shape[0] // (rows * D_TILES)
    blk = lambda i: jnp.minimum(i, n_main - 1)
    row = lambda width: pl.BlockSpec((rows, width), lambda i: (blk(i), 0))
    modspec = lambda col: pl.BlockSpec((nb, 1, d), lambda i: (blk(i) // nt, 0, col))
    full = lambda a: pl.BlockSpec(a.shape, lambda i: (0, 0))
    if time_major:
        assert nt == 1 and h2_tail is None
        y_spec = pl.BlockSpec((tb, R_TILES, nb, LANES), lambda i: (0, 0, blk(i), 0))
        scratch = [pltpu.VMEM((nb * (tb * R_TILES + SUBLANES), LANES), F32)]
    else:
        y_spec = pl.BlockSpec((rows * R_TILES, LANES), lambda i: (blk(i), 0))
        scratch = []
    in_specs = [row(half), y_spec, row(half), row(half),
                full(ln_w), full(ln_b),
                pl.BlockSpec((half, d), lambda i: (0, 0)),
                pl.BlockSpec((half, d), lambda i: (1, 0)),
                pl.BlockSpec((nb, tb, d), lambda i: (blk(i) // nt, blk(i) % nt, 0)),
                modspec(2), modspec(3), modspec(4),
                full(g2), full(wr_hi), full(wr_lo), full(b_router)]
    args = [yc, y_wkv, bonus, gate, ln_w, ln_b, w_out_bf16, w_out_bf16, x, mod, mod, mod, g2, wr_hi, wr_lo,
            b_router]
    if h2_tail is not None:
        in_specs.append(pl.BlockSpec((rows * D_TILES, LANES), lambda i: (jnp.maximum(i - n_main, 0), 0)))
        args.append(h2_tail)
    return pl.pallas_call(
        functools.partial(_outproj_kernel, n_main, h2_tail is not None, time_major),
        scratch_shapes=scratch,
        grid=(n_main + n_tail,),
        in_specs=in_specs,
        out_specs=[row(d), pl.BlockSpec((rows * D_TILES, LANES), lambda i: (i, 0)), row(LANES), row(LANES)],
        out_shape=[jax.ShapeDtypeStruct((m, d), F32),
                   jax.ShapeDtypeStruct(((n_main + n_tail) * rows * D_TILES, LANES), F32),
                   jax.ShapeDtypeStruct((m, LANES), I32), jax.ShapeDtypeStruct((m, LANES), F32)],
        compiler_params=_params(("arbitrary",)),
        name="outproj",
    )(*args)


def _row_copy(src_hbm, src_row, dst_ref, dst_row, sem):
    src = src_row * D_TILES
    dst = dst_row * GATHER_PITCH
    if not isinstance(src, int):
        src = pl.multiple_of(src, D_TILES)
    if not isinstance(dst, int):
        dst = pl.multiple_of(dst, SUBLANES)
    return pltpu.make_async_copy(src_hbm.at[pl.ds(src, D_TILES)], dst_ref.at[pl.ds(dst, D_TILES)], sem)


def _start_row_gather(index_of, src_hbm, dst_ref, sem, n):
    for r in range(n):
        _row_copy(src_hbm, index_of(r), dst_ref, r, sem).start(priority=GATHER_DMA_PRIORITY)


def _wait_row_gather(src_hbm, dst_ref, sem, n):
    def body(r, c):
        _row_copy(src_hbm, 0, dst_ref, r, sem).wait()
        return c
    lax.fori_loop(0, n, body, 0, unroll=8)


def _experts_kernel(n_tiles, te_ref, nu_ref, nxt_ref, wsl_ref, idx_ref, idxn_ref, h_hbm, wg_hbm, wu_hbm, wd_hbm,
                    o_ref, xbuf, wgf, wuf, wdf, wgb, wub, wdb, xs, act, sems, wsems):
    i = pl.program_id(0)
    n_used = nu_ref[0]
    slot = i & 1
    bounds = [(EXPERT_TILE * g) // EXPERT_ISSUE_GROUPS for g in range(EXPERT_ISSUE_GROUPS + 1)]

    def issue_next(g):
        for r in range(bounds[g], bounds[g + 1]):
            _row_copy(h_hbm, idxn_ref[0, r], xbuf.at[1 - slot], r, sems.at[1 - slot]).start(
                priority=GATHER_DMA_PRIORITY)

    def weight_copies(e, ws):
        return (pltpu.make_async_copy(wg_hbm.at[e], wgf.at[ws], wsems.at[ws]),
                pltpu.make_async_copy(wu_hbm.at[e], wuf.at[ws], wsems.at[ws]),
                pltpu.make_async_copy(wd_hbm.at[e], wdf.at[ws], wsems.at[ws]))

    @pl.when(i == 0)
    def _():
        for c in weight_copies(te_ref[0], wsl_ref[0]):
            c.start(priority=WEIGHT_DMA_PRIORITY)
        _start_row_gather(lambda r: idx_ref[0, r], h_hbm, xbuf.at[0], sems.at[0], EXPERT_TILE)

    @pl.when(i == n_used)
    def _():
        _wait_row_gather(h_hbm, xbuf.at[slot], sems.at[slot], EXPERT_TILE)

    @pl.when(i < n_used)
    def _():
        new_expert = jnp.logical_or(i == 0, te_ref[i] != te_ref[jnp.maximum(i - 1, 0)])

        @pl.when(new_expert)
        def _():
            ws = wsl_ref[i]
            for c in weight_copies(te_ref[i], ws):
                c.wait()
            wgb[...] = wgf[ws].astype(BF16)
            wub[...] = wuf[ws].astype(BF16)
            wdb[...] = wdf[ws].astype(BF16)

            @pl.when(nxt_ref[i] >= 0)
            def _():
                for c in weight_copies(nxt_ref[i], 1 - ws):
                    c.start(priority=WEIGHT_DMA_PRIORITY)

        _wait_row_gather(h_hbm, xbuf.at[slot], sems.at[slot], EXPERT_TILE)
        xs[...] = _load_row_tiled(xbuf.at[slot], EXPERT_TILE, D_TILES, GATHER_PITCH).astype(BF16)
        issue_next(0)
        half = D_EXPERT // 2
        for c in range(2):
            cols = slice(c * half, (c + 1) * half)
            gate = jnp.dot(xs[...], wgb[:, cols], preferred_element_type=F32)
            up = jnp.dot(xs[...], wub[:, cols], preferred_element_type=F32)
            act[:, cols] = (gate * _sigmoid(gate) * up).astype(BF16)
            issue_next(1 + c)
        n_chunks = EXPERT_ISSUE_GROUPS - 3
        width = D_MODEL // n_chunks
        for c in range(n_chunks):
            out = jnp.dot(act[...], wdb[:, c * width:(c + 1) * width], preferred_element_type=F32)
            for s in range(width // LANES):
                o_ref[pl.ds(c * (width // LANES) + s, EXPERT_TILE, stride=D_TILES), :] = out[:, s * LANES:(s + 1) * LANES]
            issue_next(3 + c)

    @pl.when(jnp.logical_and(i >= n_used, i < n_tiles))
    def _():
        o_ref[...] = jnp.zeros_like(o_ref)


def _experts(plan, h2_tiled, w_gate, w_up, w_down):
    tile_expert, n_used, next_expert, weight_slot, src_rows = plan
    n_tiles = tile_expert.shape[0]
    d = w_gate.shape[1]
    idx3 = src_rows.reshape(n_tiles, 1, EXPERT_TILE)
    smem_idx = lambda f: pl.BlockSpec((None, 1, EXPERT_TILE), f, memory_space=pltpu.SMEM)
    hbm = pl.BlockSpec(memory_space=pl.ANY)
    last = n_tiles - 1
    grid_spec = pltpu.PrefetchScalarGridSpec(
        num_scalar_prefetch=4,
        grid=(n_tiles + 1,),
        in_specs=[smem_idx(lambda i, *_: (jnp.minimum(i, last), 0, 0)),
                  smem_idx(lambda i, *_: (jnp.minimum(i + 1, last), 0, 0)),
                  hbm, hbm, hbm, hbm],
        out_specs=pl.BlockSpec((EXPERT_TILE * D_TILES, LANES), lambda i, *_: (jnp.minimum(i, last), 0)),
        scratch_shapes=[pltpu.VMEM((2, EXPERT_TILE * GATHER_PITCH, LANES), F32),
                        pltpu.VMEM((2, d, D_EXPERT), F32), pltpu.VMEM((2, d, D_EXPERT), F32),
                        pltpu.VMEM((2, D_EXPERT, d), F32),
                        pltpu.VMEM((d, D_EXPERT), BF16), pltpu.VMEM((d, D_EXPERT), BF16),
                        pltpu.VMEM((D_EXPERT, d), BF16),
                        pltpu.VMEM((EXPERT_TILE, d), BF16), pltpu.VMEM((EXPERT_TILE, D_EXPERT), BF16),
                        pltpu.SemaphoreType.DMA((2,)), pltpu.SemaphoreType.DMA((2,))],
    )
    return pl.pallas_call(
        functools.partial(_experts_kernel, n_tiles),
        grid_spec=grid_spec,
        out_shape=jax.ShapeDtypeStruct((n_tiles * EXPERT_TILE * D_TILES, LANES), F32),
        compiler_params=_params(("arbitrary",)),
        name="experts",
    )(tile_expert, n_used, next_expert, weight_slot, idx3, idx3, h2_tiled, w_gate, w_up, w_down)


def _combine_kernel(idx_ref, idxn_ref, wt_ref, x1_ref, ga2_ref, gf_ref, ys_hbm, o_ref, ybuf, sems):
    i = pl.program_id(0)
    n_steps = pl.num_programs(0)
    slot = i & 1
    nb, tb, d = o_ref.shape
    rows = nb * tb

    def fetch(idx, s):
        for j in range(2):
            _start_row_gather(lambda r: idx[0, 2 * r + j], ys_hbm, ybuf.at[s, j], sems.at[s], rows)

    @pl.when(i == 0)
    def _():
        fetch(idx_ref, 0)

    for j in range(2):
        _wait_row_gather(ys_hbm, ybuf.at[slot, j], sems.at[slot], rows)

    crow = rows // COMBINE_CHUNKS
    for c in range(COMBINE_CHUNKS):
        r0 = c * crow
        window = pl.ds(r0 * GATHER_PITCH, crow * GATHER_PITCH)
        wt = wt_ref[r0:r0 + crow, :]
        moe = (wt[:, 0:1] * _load_row_tiled(ybuf.at[slot, 0, window], crow, D_TILES, GATHER_PITCH)
               + wt[:, 1:2] * _load_row_tiled(ybuf.at[slot, 1, window], crow, D_TILES, GATHER_PITCH))
        if nb >= COMBINE_CHUNKS:
            sel = (slice(c * nb // COMBINE_CHUNKS, (c + 1) * nb // COMBINE_CHUNKS), slice(None))
            shape = (nb // COMBINE_CHUNKS, tb, d)
        else:
            sel = (slice(None), slice(c * tb // COMBINE_CHUNKS, (c + 1) * tb // COMBINE_CHUNKS))
            shape = (nb, tb // COMBINE_CHUNKS, d)
        ga2 = ga2_ref[sel[0]]
        x2 = x1_ref[r0:r0 + crow, :].reshape(shape) + ga2 * moe.reshape(shape)
        ms = jnp.mean(x2 * x2, axis=-1, keepdims=True)
        o_ref[sel[0], sel[1], :] = x2 * lax.rsqrt(ms + RMS_EPS) * gf_ref[...]
        for r in range(r0, r0 + crow):
            for j in range(2):
                _row_copy(ys_hbm, idxn_ref[0, 2 * r + j], ybuf.at[1 - slot, j], r, sems.at[1 - slot]).start(
                    priority=GATHER_DMA_PRIORITY)

    @pl.when(i == n_steps - 1)
    def _():
        for j in range(2):
            _wait_row_gather(ys_hbm, ybuf.at[1 - slot, j], sems.at[1 - slot], rows)


def _combine(dest, wts, x1, mod, final_g, ys, b, t, nb, tb):
    d = x1.shape[1]
    nt = t // tb
    rows = nb * tb
    n_steps = (b // nb) * nt
    idx3 = dest.reshape(n_steps, 1, 2 * rows)
    smem_idx = lambda f: pl.BlockSpec((None, 1, 2 * rows), f, memory_space=pltpu.SMEM)
    return pl.pallas_call(
        _combine_kernel,
        grid=(n_steps,),
        in_specs=[smem_idx(lambda i: (i, 0, 0)),
                  smem_idx(lambda i: (jnp.minimum(i + 1, n_steps - 1), 0, 0)),
                  pl.BlockSpec((rows, LANES), lambda i: (i, 0)),
                  pl.BlockSpec((rows, d), lambda i: (i, 0)),
                  pl.BlockSpec((nb, 1, d), lambda i: (i // nt, 0, 5)),
                  pl.BlockSpec((1, d), lambda i: (0, 0)),
                  pl.BlockSpec(memory_space=pl.ANY)],
        out_specs=pl.BlockSpec((nb, tb, d), lambda i: (i // nt, i % nt, 0)),
        out_shape=jax.ShapeDtypeStruct((b, t, d), F32),
        scratch_shapes=[pltpu.VMEM((2, 2, rows * GATHER_PITCH, LANES), F32), pltpu.SemaphoreType.DMA((2,))],
        compiler_params=_params(("arbitrary",)),
        name="combine",
    )(idx3, idx3, wts, x1, mod, final_g, ys)


def _dispatch_plan(eid):
    n_pairs = eid.shape[0] * 2
    n_tiles = n_pairs // EXPERT_TILE + N_EXPERTS
    flat = eid.reshape(-1)
    onehot = (flat[:, None] == jnp.arange(N_EXPERTS, dtype=I32)[None, :]).astype(I32)
    csum = jnp.cumsum(onehot, axis=0)
    rank = jnp.sum((csum - onehot) * onehot, axis=1)
    counts = csum[-1]
    tiles_per = (counts + EXPERT_TILE - 1) // EXPERT_TILE
    tile_end = jnp.cumsum(tiles_per)
    row_start = (tile_end - tiles_per) * EXPERT_TILE
    dest = jnp.sum(onehot * row_start[None, :], axis=1) + rank
    n_used = tile_end[-1]
    tile_ids = jnp.arange(n_tiles, dtype=I32)
    tile_expert = jnp.sum((tile_ids[:, None] >= tile_end[None, :]).astype(I32), axis=1)
    last_expert = jnp.max(jnp.where(counts > 0, jnp.arange(N_EXPERTS, dtype=I32), 0))
    tile_expert = jnp.where(tile_ids < n_used, tile_expert, last_expert).astype(I32)
    token = jnp.arange(n_pairs, dtype=I32) // 2
    filler = jnp.arange(n_tiles * EXPERT_TILE, dtype=I32) % (n_pairs // 2)
    src_rows = filler.at[dest].set(token)
    experts = jnp.arange(N_EXPERTS, dtype=I32)
    used = counts > 0
    later_used = used[None, :] & (experts[None, :] > experts[:, None])
    next_used = jnp.min(jnp.where(later_used, experts[None, :], N_EXPERTS), axis=1)
    next_used = jnp.where(next_used < N_EXPERTS, next_used, -1).astype(I32)
    slot_of = ((jnp.cumsum(used.astype(I32)) - 1) & 1).astype(I32)
    plan = (tile_expert, n_used.reshape(1).astype(I32), next_used[tile_expert], slot_of[tile_expert], src_rows)
    return dest.astype(I32), plan


def _stream_mix(x, mod, conv_state, shift_state, wkv_state, lp, nb, tb, tt, h2_tail=None):
    b, t, d = x.shape
    g = b // WKV_BATCHES
    h = _norm1(x, mod, lp["norm1_g"], nb, tb)
    proj = _inproj(h, lp["w_in"])
    batch_lanes = b == LANES and tb == t
    outs = _mixprep(proj.reshape(b, t, D_IN_PROJ), conv_state, shift_state, lp, nb, tb, batch_lanes)
    yc, gate, bonus, new_conv, new_shift = outs[N_WKV_OPERANDS:]
    if batch_lanes:
        y_wkv, s_out = _wkv_batch(outs[:N_WKV_OPERANDS], jnp.transpose(wkv_state, (1, 2, 3, 0)))
        new_wkv = jnp.transpose(s_out, (3, 0, 1, 2))
    else:
        operands = [z.reshape(b, t * R_TILES, LANES) for z in outs[:N_WKV_OPERANDS]]
        y_wkv, new_wkv = _wkv(operands, wkv_state, tt)
        y_wkv = y_wkv.reshape(b * t * R_TILES, LANES)
    x1, h2_tiled, eid, wts = _outproj(yc, y_wkv, bonus, gate, lp["ln_x_w"], lp["ln_x_b"], lp["w_out_bf16"], x, mod,
                                      lp["norm2_g"], lp["wr_hi"], lp["wr_lo"], lp["b_router"], nb, tb, batch_lanes,
                                      h2_tail)
    return x1, h2_tiled, eid, wts, new_conv, new_shift.reshape(b, D_RWKV_PROJ), new_wkv


def kernel(x_prompt, x_sample, c_prompt, c_sample, state_conv, state_shift, state_wkv, w_ada, b_ada, norm1_g, w_in, conv_w, mix_mu, w0, w_decay_up, a0, w_aaa_up, w_gate_up, k_k, k_a, r_k, ln_x_w, ln_x_b, w_out, norm2_g, w_router_group, b_router_group, w_router_expert, b_router_expert, w_gate, w_up, w_down, final_g):
    depth = w_ada.shape[0]
    assert depth == 1
    bp, tp, d = x_prompt.shape
    bs, ts, _ = x_sample.shape
    l = 0

    row = lambda z: z.reshape(1, -1)
    zeros_lora = jnp.zeros((LORA_DECAY, D_RWKV), F32)
    n_route = N_GROUPS + N_EXPERTS
    w_router = jnp.pad(jnp.concatenate([w_router_group[l], w_router_expert[l]], axis=1),
                       ((0, 0), (0, LANES - n_route)))
    b_router = jnp.concatenate([b_router_group[l], b_router_expert[l]])
    wr_hi = w_router.astype(BF16)
    lp = dict(
        norm1_g=row(norm1_g[l]), w_in=w_in[l], conv_w=conv_w[l], mix_mu=row(mix_mu[l]), w0=row(w0[l]),
        wd_pad=jnp.concatenate([w_decay_up[l], zeros_lora], axis=0), a0=row(a0[l]),
        wa_pad=jnp.concatenate([zeros_lora, w_aaa_up[l]], axis=0), w_gate_up=w_gate_up[l],
        k_k=row(k_k[l]), k_a=row(k_a[l]), r_k=row(r_k[l]), ln_x_w=row(ln_x_w[l]), ln_x_b=row(ln_x_b[l]),
        w_out_bf16=_cast_bf16(w_out[l]), norm2_g=row(norm2_g[l]),
        wr_hi=wr_hi, wr_lo=(w_router - wr_hi.astype(F32)).astype(BF16),
        b_router=jnp.pad(b_router, (0, LANES - n_route)).reshape(1, LANES),
    )

    n_c = bp + bs
    c_all = jnp.concatenate([c_prompt, c_sample], axis=0)
    c_all = jnp.pad(c_all, ((0, (-n_c) % SUBLANES), (0, 0)))
    mod = _ada(c_all, w_ada[l], row(b_ada[l]))
    mod_p = mod[:bp].reshape(bp, 1, N_MOD * d)
    mod_s = mod[bp:n_c].reshape(bs, 1, N_MOD * d)

    zc = jnp.zeros((bp, 2, D_CONV), F32)
    zs = jnp.zeros((bp, 1, D_RWKV_PROJ), F32)
    zw = jnp.zeros((bp, N_HEADS, HEAD_DIM, HEAD_DIM), F32)
    n_p = bp * tp
    x1_s, h2_s, eid_s, wts_s, conv_s, shift_s, wkv_s = _stream_mix(
        x_sample, mod_s, state_conv[l], state_shift[l].reshape(bs, 1, D_RWKV_PROJ), state_wkv[l], lp,
        nb=ROW_TILE // ts, tb=ts, tt=ts)
    x1_p, h2, eid_p, wts_p, conv_p, shift_p, wkv_p = _stream_mix(
        x_prompt, mod_p, zc, zs, zw, lp, nb=1, tb=ROW_TILE, tt=128, h2_tail=h2_s)

    eid = jnp.concatenate([eid_p[:, :2], eid_s[:, :2]], axis=0)
    dest, plan = _dispatch_plan(eid)
    ys = _experts(plan, h2, w_gate[l], w_up[l], w_down[l])
    y_prompt = _combine(dest[:2 * n_p], wts_p, x1_p, mod_p, row(final_g), ys, bp, tp, nb=1, tb=ROW_TILE)
    y_sample = _combine(dest[2 * n_p:], wts_s, x1_s, mod_s, row(final_g), ys, bs, ts, nb=ROW_TILE // ts, tb=ts)

    return (y_prompt, y_sample, conv_p[None], shift_p[None], wkv_p[None],
            conv_s[None], shift_s[None], wkv_s[None])
```

```python
import functools

import jax
import jax.numpy as jnp
from jax import lax
from jax.experimental import pallas as pl
from jax.experimental.pallas import tpu as pltpu

F32 = jnp.float32
BF16 = jnp.bfloat16
I32 = jnp.int32

D_MODEL = 2048
D_CONV = 1024
D_RWKV = 1024
HEAD_DIM = 64
N_HEADS = 16
LORA_DECAY = 64
LORA_AAA = 64
LORA_GATE = 128
D_RWKV_PROJ = 3 * D_RWKV + LORA_DECAY + LORA_AAA + LORA_GATE
D_IN_PROJ = 3 * D_CONV + D_RWKV_PROJ
N_GROUPS = 4
EXPERTS_PER_GROUP = 8
N_EXPERTS = 32
D_EXPERT = 512
N_MOD = 6
RMS_EPS = 1e-6
GN_EPS = HEAD_DIM * 1e-5

SUBLANES = 8
LANES = 128
VMEM_LIMIT_BYTES = 56 * 1024 * 1024

ROW_TILE = 256
EXPERT_TILE = 256
EXPERT_SLOTS = 3
EXPERT_ISSUE_GROUPS = 7
SEG = 256
D_TILES = D_MODEL // LANES
R_TILES = D_RWKV // LANES
GATHER_PITCH = D_TILES + SUBLANES
GATHER_DMA_PRIORITY = 0
WEIGHT_DMA_PRIORITY = 1

WKV_BATCHES = 4
V_LO = LANES // (WKV_BATCHES * R_TILES)
V_HI = HEAD_DIM // V_LO
N_WKV_OPERANDS = 6
STATE_PITCH = HEAD_DIM + SUBLANES


def _params(semantics):
    return pltpu.CompilerParams(dimension_semantics=semantics, vmem_limit_bytes=VMEM_LIMIT_BYTES)


def _sigmoid(x):
    return 1.0 / (1.0 + jnp.exp(-x))


def _store_row_tiled(ref, x, pitch):
    rows = x.shape[0]
    for s in range(x.shape[1] // LANES):
        ref[pl.ds(s, rows, stride=pitch), :] = x[:, s * LANES:(s + 1) * LANES]


def _load_row_tiled(ref, rows, n, pitch):
    return jnp.concatenate([ref[pl.ds(s, rows, stride=pitch), :] for s in range(n)], axis=-1)


def _head_ones():
    r = lax.broadcasted_iota(I32, (SEG, SEG), 0) // HEAD_DIM
    c = lax.broadcasted_iota(I32, (SEG, SEG), 1) // HEAD_DIM
    return jnp.where(r == c, 1.0, 0.0).astype(BF16)


def _split_bf16(x):
    hi = x.astype(BF16)
    return hi, (x - hi.astype(F32)).astype(BF16)


def _head_sum(x, ones):
    outs = []
    for s in range(x.shape[-1] // SEG):
        hi, lo = _split_bf16(x[:, s * SEG:(s + 1) * SEG])
        outs.append(jnp.dot(hi, ones, preferred_element_type=F32) + jnp.dot(lo, ones, preferred_element_type=F32))
    return jnp.concatenate(outs, axis=-1)


def _ada_kernel(c_ref, w_ref, b_ref, o_ref):
    c = c_ref[...]
    s = c * _sigmoid(c)
    o_ref[...] = jnp.dot(s.astype(BF16), w_ref[...].astype(BF16), preferred_element_type=F32) + b_ref[...]


def _ada(c_all, w_ada, b_ada):
    mb = c_all.shape[0]
    n = w_ada.shape[1]
    tn = 1024
    return pl.pallas_call(
        _ada_kernel,
        grid=(n // tn,),
        in_specs=[pl.BlockSpec((mb, D_MODEL), lambda j: (0, 0)),
                  pl.BlockSpec((D_MODEL, tn), lambda j: (0, j)),
                  pl.BlockSpec((1, tn), lambda j: (0, j))],
        out_specs=pl.BlockSpec((mb, tn), lambda j: (0, j)),
        out_shape=jax.ShapeDtypeStruct((mb, n), F32),
        compiler_params=_params(("arbitrary",)),
        name="ada",
    )(c_all, w_ada, b_ada)


def _norm1_kernel(x_ref, g_ref, sh_ref, sc_ref, o_ref):
    x = x_ref[...]
    nb, tb, d = x.shape
    ms = jnp.mean(x * x, axis=-1, keepdims=True)
    xn = x * lax.rsqrt(ms + RMS_EPS) * g_ref[...]
    h = xn * (1.0 + sc_ref[...]) + sh_ref[...]
    o_ref[...] = h.reshape(nb * tb, d).astype(BF16)


def _norm1(x, mod, g, nb, tb):
    b, t, d = x.shape
    nt = t // tb
    return pl.pallas_call(
        _norm1_kernel,
        grid=(b // nb, nt),
        in_specs=[pl.BlockSpec((nb, tb, d), lambda i, j: (i, j, 0)),
                  pl.BlockSpec((1, d), lambda i, j: (0, 0)),
                  pl.BlockSpec((nb, 1, d), lambda i, j: (i, 0, 0)),
                  pl.BlockSpec((nb, 1, d), lambda i, j: (i, 0, 1))],
        out_specs=pl.BlockSpec((nb * tb, d), lambda i, j: (i * nt + j, 0)),
        out_shape=jax.ShapeDtypeStruct((b * t, d), BF16),
        compiler_params=_params(("arbitrary", "arbitrary")),
        name="norm1",
    )(x, g, mod, mod)


def _inproj_kernel(h_ref, w_ref, o_ref, wb_ref):
    @pl.when(pl.program_id(1) == 0)
    def _():
        wb_ref[...] = w_ref[...].astype(BF16)

    o_ref[...] = jnp.dot(h_ref[...], wb_ref[...], preferred_element_type=F32)


def _inproj(h, w_in):
    m, k = h.shape
    n = w_in.shape[1]
    tm = min(m, 1024)
    tn = 1280
    return pl.pallas_call(
        _inproj_kernel,
        grid=(n // tn, m // tm),
        in_specs=[pl.BlockSpec((tm, k), lambda j, i: (i, 0)),
                  pl.BlockSpec((k, tn), lambda j, i: (0, j))],
        out_specs=pl.BlockSpec((tm, tn), lambda j, i: (i, j)),
        out_shape=jax.ShapeDtypeStruct((m, n), F32),
        scratch_shapes=[pltpu.VMEM((k, tn), BF16)],
        compiler_params=_params(("arbitrary", "arbitrary")),
        name="inproj",
    )(h, w_in)


def _store_time_major(ref, x, stage, nb, tb):
    pitch = tb * R_TILES + SUBLANES
    for bb in range(nb):
        xb = x[bb * tb:(bb + 1) * tb]
        for s in range(R_TILES):
            stage[pl.ds(bb * pitch + s, tb, stride=R_TILES), :] = xb[:, s * LANES:(s + 1) * LANES]
    for t in range(tb):
        for s in range(R_TILES):
            ref[t, s] = stage[pl.ds(t * R_TILES + s, nb, stride=pitch), :]


def _load_time_major(ref, stage, nb, tb):
    pitch = tb * R_TILES + SUBLANES
    for t in range(tb):
        for s in range(R_TILES):
            stage[pl.ds(t * R_TILES + s, nb, stride=pitch), :] = ref[t, s]
    rows = []
    for bb in range(nb):
        rows.append(jnp.concatenate(
            [stage[pl.ds(bb * pitch + s, tb, stride=R_TILES), :] for s in range(R_TILES)], axis=1))
    return jnp.concatenate(rows, axis=0)


def _mixprep_kernel(time_major, proj_ref, cst_ref, sst_ref, convw_ref, mu_ref, w0_ref, wd_ref, a0_ref, wa_ref,
                    wg_ref, kk_ref, ka_ref, rk_ref,
                    a_o, w_o, b_o, k_o, r_o, v_o, yc_o, g_o, bonus_o, nc_o, ns_o, *stage):
    @pl.when(pl.program_id(1) == 0)
    def _():
        nc_o[...] = cst_ref[...]
        ns_o[...] = sst_ref[...]

    nb, tb, _ = proj_ref.shape
    rows = nb * tb
    p = proj_ref[...].reshape(rows, D_IN_PROJ)
    tidx = lax.broadcasted_iota(I32, (rows, 1), 0) & (tb - 1)

    def per_row(c):
        return jnp.broadcast_to(c, (nb, tb, c.shape[-1])).reshape(rows, c.shape[-1])

    gate_b = p[:, 0:D_CONV]
    gate_c = p[:, D_CONV:2 * D_CONV]
    h_conv = p[:, 2 * D_CONV:3 * D_CONV]
    pr = p[:, 3 * D_CONV:]

    u = gate_c * h_conv
    carry = nc_o[...]
    c2 = per_row(carry[:, 0:1, :])
    c1 = per_row(carry[:, 1:2, :])
    u1 = jnp.where(tidx == 0, c1, pltpu.roll(u, 1, 0))
    u2 = jnp.where(tidx == 0, c2, jnp.where(tidx == 1, c1, pltpu.roll(u, 2, 0)))
    cw = convw_ref[...]
    conv_y = u * cw[2:3, :] + u2 * cw[0:1, :] + u1 * cw[1:2, :]
    yc_o[...] = (gate_b * conv_y).astype(BF16)
    nc_o[...] = u.reshape(nb, tb, D_CONV)[:, tb - 2:tb, :]

    p_prev = jnp.where(tidx == 0, per_row(ns_o[...]), pltpu.roll(pr, 1, 0))
    ns_o[...] = pr.reshape(nb, tb, D_RWKV_PROJ)[:, tb - 1:tb, :]
    pm = pr + (p_prev - pr) * mu_ref[...]
    r = pm[:, 0:D_RWKV]
    k = pm[:, D_RWKV:2 * D_RWKV]
    v = pm[:, 2 * D_RWKV:3 * D_RWKV]
    x_lora = pm[:, 3 * D_RWKV:3 * D_RWKV + LORA_DECAY + LORA_AAA]
    xg = pm[:, 3 * D_RWKV + LORA_DECAY + LORA_AAA:]

    d_pre = w0_ref[...] + jnp.dot(jnp.tanh(x_lora).astype(BF16), wd_ref[...].astype(BF16),
                                  preferred_element_type=F32)
    a_pre = a0_ref[...] + jnp.dot(x_lora.astype(BF16), wa_ref[...].astype(BF16), preferred_element_type=F32)
    g = jnp.dot(_sigmoid(xg).astype(BF16), wg_ref[...].astype(BF16), preferred_element_type=F32)
    z = -d_pre
    softplus = jnp.maximum(z, 0.0) + jnp.log(1.0 + jnp.exp(-jnp.abs(z)))
    w_log = -softplus - 0.5
    decay = jnp.exp(-jnp.exp(w_log))
    a_h = _sigmoid(a_pre)

    ones = _head_ones()
    kk = k * kk_ref[...]
    kk = kk * lax.rsqrt(jnp.maximum(_head_sum(kk * kk, ones), 1e-24))
    k_h = k * (1.0 + (a_h - 1.0) * ka_ref[...])
    bonus = _head_sum(r * k_h * rk_ref[...], ones) * v

    operands = ((a_o, -kk), (w_o, decay), (b_o, kk * a_h), (k_o, k_h), (r_o, r), (v_o, v))
    for ref, val in operands:
        if time_major:
            _store_time_major(ref, val, stage[0], nb, tb)
        else:
            _store_row_tiled(ref, val, R_TILES)
    g_o[...] = g
    bonus_o[...] = bonus


def _mixprep(proj3, conv_state, shift_state, lp, nb, tb, time_major):
    b, t, _ = proj3.shape
    nt = t // tb
    rows = nb * tb
    m = b * t
    assert not time_major or nt == 1

    def row_spec(width):
        return pl.BlockSpec((rows, width), lambda i, j: (i * nt + j, 0))

    def full2(a):
        return pl.BlockSpec(a.shape, lambda i, j: (0, 0))

    consts = (lp["conv_w"], lp["mix_mu"], lp["w0"], lp["wd_pad"], lp["a0"], lp["wa_pad"], lp["w_gate_up"],
              lp["k_k"], lp["k_a"], lp["r_k"])
    if time_major:
        tiled_spec = pl.BlockSpec((tb, R_TILES, nb, LANES), lambda i, j: (0, 0, i, 0))
        tiled_out = jax.ShapeDtypeStruct((t, R_TILES, b, LANES), F32)
        scratch = [pltpu.VMEM((nb * (tb * R_TILES + SUBLANES), LANES), F32)]
    else:
        tiled_spec = pl.BlockSpec((rows * R_TILES, LANES), lambda i, j: (i * nt + j, 0))
        tiled_out = jax.ShapeDtypeStruct((m * R_TILES, LANES), F32)
        scratch = []
    f32_out = jax.ShapeDtypeStruct((m, D_RWKV), F32)
    return pl.pallas_call(
        functools.partial(_mixprep_kernel, time_major),
        scratch_shapes=scratch,
        grid=(b // nb, nt),
        in_specs=[pl.BlockSpec((nb, tb, D_IN_PROJ), lambda i, j: (i, j, 0)),
                  pl.BlockSpec((nb, 2, D_CONV), lambda i, j: (i, 0, 0)),
                  pl.BlockSpec((nb, 1, D_RWKV_PROJ), lambda i, j: (i, 0, 0))] + [full2(c) for c in consts],
        out_specs=[tiled_spec] * N_WKV_OPERANDS + [row_spec(D_CONV), row_spec(D_RWKV), row_spec(D_RWKV),
                                                   pl.BlockSpec((nb, 2, D_CONV), lambda i, j: (i, 0, 0)),
                                                   pl.BlockSpec((nb, 1, D_RWKV_PROJ), lambda i, j: (i, 0, 0))],
        out_shape=[tiled_out] * N_WKV_OPERANDS + [jax.ShapeDtypeStruct((m, D_CONV), BF16), f32_out, f32_out,
                                                  jax.ShapeDtypeStruct((b, 2, D_CONV), F32),
                                                  jax.ShapeDtypeStruct((b, 1, D_RWKV_PROJ), F32)],
        compiler_params=_params(("arbitrary", "arbitrary")),
        name="mixprep",
    )(proj3, conv_state, shift_state, *consts)


def _wkv_kernel(a_ref, w_ref, b_ref, k_ref, r_ref, v_ref, s0_ref, y_ref, sout_ref, s_ref, tiles, yq, cbuf):
    n_pairs = WKV_BATCHES * R_TILES

    @pl.when(pl.program_id(1) == 0)
    def _():
        for bp in range(n_pairs):
            bb, hp = divmod(bp, R_TILES)
            cbuf[pl.ds(bp * STATE_PITCH, HEAD_DIM), :] = jnp.concatenate(
                [s0_ref[bb, 2 * hp], s0_ref[bb, 2 * hp + 1]], axis=1)
        for vh in range(V_HI):
            m = jnp.concatenate([cbuf[pl.ds(vh * V_LO + vl, n_pairs, stride=STATE_PITCH), :]
                                 for vl in range(V_LO)], axis=0)
            mt = m.T
            s_ref[0, vh] = mt[:HEAD_DIM]
            s_ref[1, vh] = mt[HEAD_DIM:]

    tt = a_ref.shape[1] // R_TILES
    srcs = (a_ref, w_ref, b_ref, k_ref, r_ref, v_ref)
    n_chain = WKV_BATCHES * R_TILES
    sub8 = lax.broadcasted_iota(I32, (SUBLANES, LANES), 0)
    grp8 = lax.broadcasted_iota(I32, (SUBLANES, LANES), 1) // n_chain
    diag8 = (sub8 % V_LO) == grp8
    sub4 = lax.broadcasted_iota(I32, (V_LO, LANES), 0)
    grp4 = lax.broadcasted_iota(I32, (V_LO, LANES), 1) // n_chain
    diag4 = sub4 == grp4

    def produce(t, slot):
        row0 = pl.multiple_of(t * R_TILES, R_TILES)
        for ai, ref in enumerate(srcs):
            m = jnp.concatenate([ref[bb, pl.ds(row0, R_TILES), :] for bb in range(WKV_BATCHES)], axis=0)
            xt = jnp.concatenate([m] * V_LO, axis=0).T
            if ai < N_WKV_OPERANDS - 1:
                tiles[slot, ai] = xt
            else:
                for q in range(LANES // SUBLANES):
                    z = jnp.where(diag8, xt[q * SUBLANES:(q + 1) * SUBLANES], 0.0)
                    z = z + pltpu.roll(z, 1, 0)
                    z = z + pltpu.roll(z, 2, 0)
                    tiles[slot, ai, q * SUBLANES:(q + 1) * SUBLANES, :] = z

    def consume(slot):
        for par in range(2):
            lo = par * HEAD_DIM
            a = tiles[slot, 0, lo:lo + HEAD_DIM, :]
            w = tiles[slot, 1, lo:lo + HEAD_DIM, :]
            b = tiles[slot, 2, lo:lo + HEAD_DIM, :]
            k = tiles[slot, 3, lo:lo + HEAD_DIM, :]
            r = tiles[slot, 4, lo:lo + HEAD_DIM, :]
            for vh in range(V_HI):
                s = s_ref[par, vh]
                vrow = tiles[slot, 5, lo + vh * V_LO + V_LO - 1:lo + (vh + 1) * V_LO, :]
                sa = jnp.sum(s * a, axis=0, keepdims=True)
                s = s * w + sa * b + vrow * k
                s_ref[par, vh] = s
                yrow = jnp.sum(s * r, axis=0, keepdims=True)
                yq[slot, lo + vh * V_LO:lo + (vh + 1) * V_LO, :] = jnp.where(
                    diag4, jnp.broadcast_to(yrow, (V_LO, LANES)), 0.0)

    def finalize(t, slot):
        qt = yq[slot].T
        res = qt[0:n_chain]
        for j in range(1, V_LO):
            res = res + qt[j * n_chain:(j + 1) * n_chain]
        row0 = pl.multiple_of(t * R_TILES, R_TILES)
        for bb in range(WKV_BATCHES):
            y_ref[bb, pl.ds(row0, R_TILES), :] = res[bb * R_TILES:(bb + 1) * R_TILES]

    produce(0, 0)
    produce(1, 1)
    consume(0)

    def body(i, c):
        t1 = 2 * i + 1
        produce(t1 + 1, 0)
        consume(1)
        finalize(t1 - 1, 0)
        produce(jnp.minimum(t1 + 2, tt - 1), 1)
        consume(0)
        finalize(t1, 1)
        return c

    lax.fori_loop(0, tt // 2 - 1, body, 0)
    consume(1)
    finalize(tt - 2, 0)
    finalize(tt - 1, 1)

    @pl.when(pl.program_id(1) == pl.num_programs(1) - 1)
    def _():
        for vh in range(V_HI):
            m = jnp.concatenate([s_ref[0, vh], s_ref[1, vh]], axis=0).T
            for vl in range(V_LO):
                cbuf[pl.ds(vh * V_LO + vl, n_pairs, stride=STATE_PITCH), :] = m[vl * n_pairs:(vl + 1) * n_pairs]
        for bp in range(n_pairs):
            bb, hp = divmod(bp, R_TILES)
            c = cbuf[pl.ds(bp * STATE_PITCH, HEAD_DIM), :]
            sout_ref[bb, 2 * hp] = c[:, :HEAD_DIM]
            sout_ref[bb, 2 * hp + 1] = c[:, HEAD_DIM:]


def _wkv(operands, s0, tt):
    bsz, rows, _ = operands[0].shape
    t = rows // R_TILES
    g = bsz // WKV_BATCHES
    assert tt % 2 == 0 and tt >= 4 and t % tt == 0
    blk = pl.BlockSpec((WKV_BATCHES, tt * R_TILES, LANES), lambda i, j: (i, j, 0))
    state = pl.BlockSpec((WKV_BATCHES, N_HEADS, HEAD_DIM, HEAD_DIM), lambda i, j: (i, 0, 0, 0))
    return pl.pallas_call(
        _wkv_kernel,
        grid=(g, t // tt),
        in_specs=[blk] * N_WKV_OPERANDS + [state],
        out_specs=[blk, state],
        out_shape=[jax.ShapeDtypeStruct((bsz, rows, LANES), F32),
                   jax.ShapeDtypeStruct((bsz, N_HEADS, HEAD_DIM, HEAD_DIM), F32)],
        scratch_shapes=[pltpu.VMEM((2, V_HI, HEAD_DIM, LANES), F32),
                        pltpu.VMEM((2, N_WKV_OPERANDS, LANES, LANES), F32), pltpu.VMEM((2, LANES, LANES), F32),
                        pltpu.VMEM((WKV_BATCHES * R_TILES * STATE_PITCH, LANES), F32)],
        compiler_params=_params(("arbitrary", "arbitrary")),
        name="wkv",
    )(*operands, s0)


def _wkv_batch_kernel(a_ref, w_ref, b_ref, k_ref, r_ref, v_ref, s0_ref, y_ref, s_ref, yq):
    s_ref[...] = s0_ref[...]
    tt = a_ref.shape[0]

    def step(t, carry):
        a, w, b, k, r, v = (ref[t, 0].T for ref in (a_ref, w_ref, b_ref, k_ref, r_ref, v_ref))
        for par in range(2):
            lo = par * HEAD_DIM
            ap, wp, bp, kp, rp = (z[lo:lo + HEAD_DIM] for z in (a, w, b, k, r))
            for vi in range(HEAD_DIM):
                s = s_ref[par, vi]
                sa = jnp.sum(s * ap, axis=0, keepdims=True)
                s = s * wp + sa * bp + v[lo + vi:lo + vi + 1] * kp
                s_ref[par, vi] = s
                yq[lo + vi:lo + vi + 1, :] = jnp.sum(s * rp, axis=0, keepdims=True)
        y_ref[t, 0] = yq[...].T
        return carry

    lax.fori_loop(0, tt, step, 0)


def _wkv_batch(operands, s0):
    t = operands[0].shape[0]
    assert operands[0].shape[2] == LANES and s0.shape == (N_HEADS, HEAD_DIM, HEAD_DIM, LANES)
    blk = pl.BlockSpec((t, 1, LANES, LANES), lambda i: (0, i, 0, 0))
    state = pl.BlockSpec((2, HEAD_DIM, HEAD_DIM, LANES), lambda i: (i, 0, 0, 0))
    return pl.pallas_call(
        _wkv_batch_kernel,
        grid=(R_TILES,),
        in_specs=[blk] * N_WKV_OPERANDS + [state],
        out_specs=[blk, state],
        out_shape=[jax.ShapeDtypeStruct((t, R_TILES, LANES, LANES), F32),
                   jax.ShapeDtypeStruct((N_HEADS, HEAD_DIM, HEAD_DIM, LANES), F32)],
        scratch_shapes=[pltpu.VMEM((LANES, LANES), F32)],
        compiler_params=_params(("arbitrary",)),
        name="wkv_batch",
    )(*operands, s0)


def _group_norm_gate(y, bonus, g, ln_w, ln_b):
    ones = _head_ones()
    mu = _head_sum(y, ones) * (1.0 / HEAD_DIM)
    d = y - mu
    var = _head_sum(d * d, ones) * (1.0 / HEAD_DIM)
    yn = d * lax.rsqrt(var + GN_EPS)
    yn = yn * ln_w + ln_b
    return (yn + bonus) * g


def _cast_kernel(x_ref, o_ref):
    o_ref[...] = x_ref[...].astype(o_ref.dtype)


def _cast_bf16(w):
    r, c = w.shape
    tr = 512
    return pl.pallas_call(
        _cast_kernel,
        grid=(r // tr,),
        in_specs=[pl.BlockSpec((tr, c), lambda i: (i, 0))],
        out_specs=pl.BlockSpec((tr, c), lambda i: (i, 0)),
        out_shape=jax.ShapeDtypeStruct((r, c), BF16),
        compiler_params=_params(("arbitrary",)),
        name="cast_bf16",
    )(w)


def _route(lg):
    lane = lax.broadcasted_iota(I32, lg.shape, 1)
    neg = -jnp.inf
    big = jnp.int32(1 << 20)

    is_g = lane < N_GROUPS
    gl = jnp.where(is_g, lg, neg)
    gmax = jnp.max(gl, axis=1, keepdims=True)
    gidx = jnp.min(jnp.where(gl == gmax, lane, big), axis=1, keepdims=True)
    gsum = jnp.sum(jnp.where(is_g, jnp.exp(gl - gmax), 0.0), axis=1, keepdims=True)
    gprob = 1.0 / gsum

    e_lane = lane - N_GROUPS
    in_grp = (e_lane >= 0) & (e_lane < N_EXPERTS) & ((e_lane >> 3) == gidx)
    el = jnp.where(in_grp, lg, neg)
    emax = jnp.max(el, axis=1, keepdims=True)
    ex = jnp.where(in_grp, jnp.exp(el - emax), 0.0)
    prob = ex / jnp.sum(ex, axis=1, keepdims=True)
    prob = jnp.where(in_grp, prob, -1.0)
    p1 = jnp.max(prob, axis=1, keepdims=True)
    i1 = jnp.min(jnp.where(prob == p1, lane, big), axis=1, keepdims=True)
    prob2 = jnp.where(lane == i1, -1.0, prob)
    p2 = jnp.max(prob2, axis=1, keepdims=True)
    i2 = jnp.min(jnp.where(prob2 == p2, lane, big), axis=1, keepdims=True)
    psum = p1 + p2
    w1 = p1 / psum * gprob
    w2 = p2 / psum * gprob
    eid = jnp.where(lane == 0, i1 - N_GROUPS, jnp.where(lane == 1, i2 - N_GROUPS, 0))
    wts = jnp.where(lane == 0, w1, jnp.where(lane == 1, w2, 0.0))
    return eid, wts


def _outproj_kernel(n_main, has_tail, time_major, yc_ref, y_ref, bonus_ref, gate_ref, lnw_ref, lnb_ref, wt_ref,
                    wb_ref, x_ref, ga1_ref, sh2_ref, sc2_ref, g2_ref, wrh_ref, wrl_ref, br_ref, *rest):
    rest = list(rest)
    tail_ref = rest.pop(0) if has_tail else None
    x1_o, h2_o, eid_o, wts_o = rest[:4]
    i = pl.program_id(0)

    @pl.when(i < n_main)
    def _():
        nb, tb, d = x_ref.shape
        rows = nb * tb
        if time_major:
            y = _load_time_major(y_ref, rest[4], nb, tb)
        else:
            y = _load_row_tiled(y_ref, rows, R_TILES, R_TILES)
        yr = _group_norm_gate(y, bonus_ref[...], gate_ref[...], lnw_ref[...], lnb_ref[...]).astype(BF16)
        mix = (jnp.dot(yc_ref[...], wt_ref[...], preferred_element_type=F32)
               + jnp.dot(yr, wb_ref[...], preferred_element_type=F32))
        x1 = x_ref[...] + ga1_ref[...] * mix.reshape(nb, tb, d)
        ms = jnp.mean(x1 * x1, axis=-1, keepdims=True)
        h2 = x1 * lax.rsqrt(ms + RMS_EPS) * g2_ref[...]
        h2 = (h2 * (1.0 + sc2_ref[...]) + sh2_ref[...]).reshape(rows, d)
        x1_o[...] = x1.reshape(rows, d)
        _store_row_tiled(h2_o, h2, D_TILES)
        h_hi, h_lo = _split_bf16(h2)
        w_hi = wrh_ref[...]
        logits = (jnp.dot(h_hi, w_hi, preferred_element_type=F32)
                  + jnp.dot(h_lo, w_hi, preferred_element_type=F32)
                  + jnp.dot(h_hi, wrl_ref[...], preferred_element_type=F32)) + br_ref[...]
        eid_o[...], wts_o[...] = _route(logits)

    if has_tail:
        @pl.when(i >= n_main)
        def _():
            h2_o[...] = tail_ref[...]


def _outproj(yc, y_wkv, bonus, gate, ln_w, ln_b, w_out_bf16, x, mod, g2, wr_hi, wr_lo, b_router, nb, tb,
             time_major, h2_tail=None):
    b, t, d = x.shape
    nt = t // tb
    rows = nb * tb
    m = b * t
    half = D_CONV
    n_main = (b // nb) * nt
    n_tail = 0 if h2_tail is None else h2_tail.shape[0] // (rows * D_TILES)
    blk = lambda i: jnp.minimum(i, n_main - 1)
    row = lambda width: pl.BlockSpec((rows, width), lambda i: (blk(i), 0))
    modspec = lambda col: pl.BlockSpec((nb, 1, d), lambda i: (blk(i) // nt, 0, col))
    full = lambda a: pl.BlockSpec(a.shape, lambda i: (0, 0))
    if time_major:
        assert nt == 1 and h2_tail is None
        y_spec = pl.BlockSpec((tb, R_TILES, nb, LANES), lambda i: (0, 0, blk(i), 0))
        scratch = [pltpu.VMEM((nb * (tb * R_TILES + SUBLANES), LANES), F32)]
    else:
        y_spec = pl.BlockSpec((rows * R_TILES, LANES), lambda i: (blk(i), 0))
        scratch = []
    in_specs = [row(half), y_spec, row(half), row(half),
                full(ln_w), full(ln_b),
                pl.BlockSpec((half, d), lambda i: (0, 0)),
                pl.BlockSpec((half, d), lambda i: (1, 0)),
                pl.BlockSpec((nb, tb, d), lambda i: (blk(i) // nt, blk(i) % nt, 0)),
                modspec(2), modspec(3), modspec(4),
                full(g2), full(wr_hi), full(wr_lo), full(b_router)]
    args = [yc, y_wkv, bonus, gate, ln_w, ln_b, w_out_bf16, w_out_bf16, x, mod, mod, mod, g2, wr_hi, wr_lo,
            b_router]
    if h2_tail is not None:
        in_specs.append(pl.BlockSpec((rows * D_TILES, LANES), lambda i: (jnp.maximum(i - n_main, 0), 0)))
        args.append(h2_tail)
    return pl.pallas_call(
        functools.partial(_outproj_kernel, n_main, h2_tail is not None, time_major),
        scratch_shapes=scratch,
        grid=(n_main + n_tail,),
        in_specs=in_specs,
        out_specs=[row(d), pl.BlockSpec((rows * D_TILES, LANES), lambda i: (i, 0)), row(LANES), row(LANES)],
        out_shape=[jax.ShapeDtypeStruct((m, d), F32),
                   jax.ShapeDtypeStruct(((n_main + n_tail) * rows * D_TILES, LANES), F32),
                   jax.ShapeDtypeStruct((m, LANES), I32), jax.ShapeDtypeStruct((m, LANES), F32)],
        compiler_params=_params(("arbitrary",)),
        name="outproj",
    )(*args)


def _row_copy(src_hbm, src_row, dst_ref, dst_row, sem):
    src = src_row * D_TILES
    dst = dst_row * GATHER_PITCH
    if not isinstance(src, int):
        src = pl.multiple_of(src, D_TILES)
    if not isinstance(dst, int):
        dst = pl.multiple_of(dst, SUBLANES)
    return pltpu.make_async_copy(src_hbm.at[pl.ds(src, D_TILES)], dst_ref.at[pl.ds(dst, D_TILES)], sem)


def _start_row_gather(index_of, src_hbm, dst_ref, sem, n):
    for r in range(n):
        _row_copy(src_hbm, index_of(r), dst_ref, r, sem).start(priority=GATHER_DMA_PRIORITY)


def _wait_row_gather(src_hbm, dst_ref, sem, n):
    def body(r, c):
        _row_copy(src_hbm, 0, dst_ref, r, sem).wait()
        return c
    lax.fori_loop(0, n, body, 0, unroll=8)


def _experts_kernel(n_tiles, te_ref, nu_ref, nxt_ref, wsl_ref, idx_ref, idx1_ref, idx2_ref, h_hbm, wg_hbm, wu_hbm,
                    wd_hbm, o_ref, xbuf, wgf, wuf, wdf, wgb, wub, wdb, xs, act, sems, wsems):
    i = pl.program_id(0)
    n_used = nu_ref[0]
    slot = lax.rem(i, EXPERT_SLOTS)
    slot2 = lax.rem(i + 2, EXPERT_SLOTS)
    bounds = [(EXPERT_TILE * g) // EXPERT_ISSUE_GROUPS for g in range(EXPERT_ISSUE_GROUPS + 1)]

    def issue_next(g):
        for r in range(bounds[g], bounds[g + 1]):
            _row_copy(h_hbm, idx2_ref[0, r], xbuf.at[slot2], r, sems.at[slot2]).start(priority=GATHER_DMA_PRIORITY)

    def weight_copies(e, ws):
        return (pltpu.make_async_copy(wg_hbm.at[e], wgf.at[ws], wsems.at[ws]),
                pltpu.make_async_copy(wu_hbm.at[e], wuf.at[ws], wsems.at[ws]),
                pltpu.make_async_copy(wd_hbm.at[e], wdf.at[ws], wsems.at[ws]))

    @pl.when(i == 0)
    def _():
        for c in weight_copies(te_ref[0], wsl_ref[0]):
            c.start(priority=WEIGHT_DMA_PRIORITY)
        _start_row_gather(lambda r: idx_ref[0, r], h_hbm, xbuf.at[0], sems.at[0], EXPERT_TILE)
        _start_row_gather(lambda r: idx1_ref[0, r], h_hbm, xbuf.at[1], sems.at[1], EXPERT_TILE)

    @pl.when(jnp.logical_and(i >= n_used, i < n_used + 2))
    def _():
        _wait_row_gather(h_hbm, xbuf.at[slot], sems.at[slot], EXPERT_TILE)

    @pl.when(i < n_used)
    def _():
        new_expert = jnp.logical_or(i == 0, te_ref[i] != te_ref[jnp.maximum(i - 1, 0)])

        @pl.when(new_expert)
        def _():
            ws = wsl_ref[i]
            for c in weight_copies(te_ref[i], ws):
                c.wait()
            wgb[...] = wgf[ws].astype(BF16)
            wub[...] = wuf[ws].astype(BF16)
            wdb[...] = wdf[ws].astype(BF16)

            @pl.when(nxt_ref[i] >= 0)
            def _():
                for c in weight_copies(nxt_ref[i], 1 - ws):
                    c.start(priority=WEIGHT_DMA_PRIORITY)

        _wait_row_gather(h_hbm, xbuf.at[slot], sems.at[slot], EXPERT_TILE)
        xs[...] = _load_row_tiled(xbuf.at[slot], EXPERT_TILE, D_TILES, GATHER_PITCH).astype(BF16)
        issue_next(0)
        half = D_EXPERT // 2
        for c in range(2):
            cols = slice(c * half, (c + 1) * half)
            gate = jnp.dot(xs[...], wgb[:, cols], preferred_element_type=F32)
            up = jnp.dot(xs[...], wub[:, cols], preferred_element_type=F32)
            act[:, cols] = (gate * _sigmoid(gate) * up).astype(BF16)
            issue_next(1 + c)
        n_chunks = EXPERT_ISSUE_GROUPS - 3
        width = D_MODEL // n_chunks
        for c in range(n_chunks):
            out = jnp.dot(act[...], wdb[:, c * width:(c + 1) * width], preferred_element_type=F32)
            for s in range(width // LANES):
                o_ref[pl.ds(c * (width // LANES) + s, EXPERT_TILE, stride=D_TILES), :] = out[:, s * LANES:(s + 1) * LANES]
            issue_next(3 + c)

    @pl.when(jnp.logical_and(i >= n_used, i < n_tiles))
    def _():
        o_ref[...] = jnp.zeros_like(o_ref)


def _experts(plan, h2_tiled, w_gate, w_up, w_down):
    tile_expert, n_used, next_expert, weight_slot, src_rows = plan
    n_tiles = tile_expert.shape[0]
    d = w_gate.shape[1]
    idx3 = src_rows.reshape(n_tiles, 1, EXPERT_TILE)
    smem_idx = lambda f: pl.BlockSpec((None, 1, EXPERT_TILE), f, memory_space=pltpu.SMEM)
    hbm = pl.BlockSpec(memory_space=pl.ANY)
    last = n_tiles - 1
    grid_spec = pltpu.PrefetchScalarGridSpec(
        num_scalar_prefetch=4,
        grid=(n_tiles + 2,),
        in_specs=[smem_idx(lambda i, *_: (jnp.minimum(i, last), 0, 0)),
                  smem_idx(lambda i, *_: (jnp.minimum(i + 1, last), 0, 0)),
                  smem_idx(lambda i, *_: (jnp.minimum(i + 2, last), 0, 0)),
                  hbm, hbm, hbm, hbm],
        out_specs=pl.BlockSpec((EXPERT_TILE * D_TILES, LANES), lambda i, *_: (jnp.minimum(i, last), 0)),
        scratch_shapes=[pltpu.VMEM((EXPERT_SLOTS, EXPERT_TILE * GATHER_PITCH, LANES), F32),
                        pltpu.VMEM((2, d, D_EXPERT), F32), pltpu.VMEM((2, d, D_EXPERT), F32),
                        pltpu.VMEM((2, D_EXPERT, d), F32),
                        pltpu.VMEM((d, D_EXPERT), BF16), pltpu.VMEM((d, D_EXPERT), BF16),
                        pltpu.VMEM((D_EXPERT, d), BF16),
                        pltpu.VMEM((EXPERT_TILE, d), BF16), pltpu.VMEM((EXPERT_TILE, D_EXPERT), BF16),
                        pltpu.SemaphoreType.DMA((EXPERT_SLOTS,)), pltpu.SemaphoreType.DMA((2,))],
    )
    return pl.pallas_call(
        functools.partial(_experts_kernel, n_tiles),
        grid_spec=grid_spec,
        out_shape=jax.ShapeDtypeStruct((n_tiles * EXPERT_TILE * D_TILES, LANES), F32),
        compiler_params=_params(("arbitrary",)),
        name="experts",
    )(tile_expert, n_used, next_expert, weight_slot, idx3, idx3, idx3, h2_tiled, w_gate, w_up, w_down)


def _combine_kernel(idx_ref, idxn_ref, wt_ref, x1_ref, ga2_ref, gf_ref, ys_hbm, o_ref, ybuf, sems):
    i = pl.program_id(0)
    n_steps = pl.num_programs(0)
    slot = i & 1
    nb, tb, d = o_ref.shape
    rows = nb * tb

    def fetch(idx, s):
        for j in range(2):
            _start_row_gather(lambda r: idx[0, 2 * r + j], ys_hbm, ybuf.at[s, j], sems.at[s], rows)

    @pl.when(i == 0)
    def _():
        fetch(idx_ref, 0)

    @pl.when(i + 1 < n_steps)
    def _():
        fetch(idxn_ref, 1 - slot)

    for j in range(2):
        _wait_row_gather(ys_hbm, ybuf.at[slot, j], sems.at[slot], rows)

    wt = wt_ref[...]
    moe = (wt[:, 0:1] * _load_row_tiled(ybuf.at[slot, 0], rows, D_TILES, GATHER_PITCH)
           + wt[:, 1:2] * _load_row_tiled(ybuf.at[slot, 1], rows, D_TILES, GATHER_PITCH))
    x2 = x1_ref[...].reshape(nb, tb, d) + ga2_ref[...] * moe.reshape(nb, tb, d)
    ms = jnp.mean(x2 * x2, axis=-1, keepdims=True)
    o_ref[...] = x2 * lax.rsqrt(ms + RMS_EPS) * gf_ref[...]


def _combine(dest, wts, x1, mod, final_g, ys, b, t, nb, tb):
    d = x1.shape[1]
    nt = t // tb
    rows = nb * tb
    n_steps = (b // nb) * nt
    idx3 = dest.reshape(n_steps, 1, 2 * rows)
    smem_idx = lambda f: pl.BlockSpec((None, 1, 2 * rows), f, memory_space=pltpu.SMEM)
    return pl.pallas_call(
        _combine_kernel,
        grid=(n_steps,),
        in_specs=[smem_idx(lambda i: (i, 0, 0)),
                  smem_idx(lambda i: (jnp.minimum(i + 1, n_steps - 1), 0, 0)),
                  pl.BlockSpec((rows, LANES), lambda i: (i, 0)),
                  pl.BlockSpec((rows, d), lambda i: (i, 0)),
                  pl.BlockSpec((nb, 1, d), lambda i: (i // nt, 0, 5)),
                  pl.BlockSpec((1, d), lambda i: (0, 0)),
                  pl.BlockSpec(memory_space=pl.ANY)],
        out_specs=pl.BlockSpec((nb, tb, d), lambda i: (i // nt, i % nt, 0)),
        out_shape=jax.ShapeDtypeStruct((b, t, d), F32),
        scratch_shapes=[pltpu.VMEM((2, 2, rows * GATHER_PITCH, LANES), F32), pltpu.SemaphoreType.DMA((2,))],
        compiler_params=_params(("arbitrary",)),
        name="combine",
    )(idx3, idx3, wts, x1, mod, final_g, ys)


def _dispatch_plan(eid):
    n_pairs = eid.shape[0] * 2
    n_tiles = n_pairs // EXPERT_TILE + N_EXPERTS
    flat = eid.reshape(-1)
    onehot = (flat[:, None] == jnp.arange(N_EXPERTS, dtype=I32)[None, :]).astype(I32)
    csum = jnp.cumsum(onehot, axis=0)
    rank = jnp.sum((csum - onehot) * onehot, axis=1)
    counts = csum[-1]
    tiles_per = (counts + EXPERT_TILE - 1) // EXPERT_TILE
    tile_end = jnp.cumsum(tiles_per)
    row_start = (tile_end - tiles_per) * EXPERT_TILE
    dest = jnp.sum(onehot * row_start[None, :], axis=1) + rank
    n_used = tile_end[-1]
    tile_ids = jnp.arange(n_tiles, dtype=I32)
    tile_expert = jnp.sum((tile_ids[:, None] >= tile_end[None, :]).astype(I32), axis=1)
    last_expert = jnp.max(jnp.where(counts > 0, jnp.arange(N_EXPERTS, dtype=I32), 0))
    tile_expert = jnp.where(tile_ids < n_used, tile_expert, last_expert).astype(I32)
    token = jnp.arange(n_pairs, dtype=I32) // 2
    filler = jnp.arange(n_tiles * EXPERT_TILE, dtype=I32) % (n_pairs // 2)
    src_rows = filler.at[dest].set(token)
    experts = jnp.arange(N_EXPERTS, dtype=I32)
    used = counts > 0
    later_used = used[None, :] & (experts[None, :] > experts[:, None])
    next_used = jnp.min(jnp.where(later_used, experts[None, :], N_EXPERTS), axis=1)
    next_used = jnp.where(next_used < N_EXPERTS, next_used, -1).astype(I32)
    slot_of = ((jnp.cumsum(used.astype(I32)) - 1) & 1).astype(I32)
    plan = (tile_expert, n_used.reshape(1).astype(I32), next_used[tile_expert], slot_of[tile_expert], src_rows)
    return dest.astype(I32), plan


def _stream_mix(x, mod, conv_state, shift_state, wkv_state, lp, nb, tb, tt, h2_tail=None):
    b, t, d = x.shape
    g = b // WKV_BATCHES
    h = _norm1(x, mod, lp["norm1_g"], nb, tb)
    proj = _inproj(h, lp["w_in"])
    batch_lanes = b == LANES and tb == t
    outs = _mixprep(proj.reshape(b, t, D_IN_PROJ), conv_state, shift_state, lp, nb, tb, batch_lanes)
    yc, gate, bonus, new_conv, new_shift = outs[N_WKV_OPERANDS:]
    if batch_lanes:
        y_wkv, s_out = _wkv_batch(outs[:N_WKV_OPERANDS], jnp.transpose(wkv_state, (1, 2, 3, 0)))
        new_wkv = jnp.transpose(s_out, (3, 0, 1, 2))
    else:
        operands = [z.reshape(b, t * R_TILES, LANES) for z in outs[:N_WKV_OPERANDS]]
        y_wkv, new_wkv = _wkv(operands, wkv_state, tt)
        y_wkv = y_wkv.reshape(b * t * R_TILES, LANES)
    x1, h2_tiled, eid, wts = _outproj(yc, y_wkv, bonus, gate, lp["ln_x_w"], lp["ln_x_b"], lp["w_out_bf16"], x, mod,
                                      lp["norm2_g"], lp["wr_hi"], lp["wr_lo"], lp["b_router"], nb, tb, batch_lanes,
                                      h2_tail)
    return x1, h2_tiled, eid, wts, new_conv, new_shift.reshape(b, D_RWKV_PROJ), new_wkv


def kernel(x_prompt, x_sample, c_prompt, c_sample, state_conv, state_shift, state_wkv, w_ada, b_ada, norm1_g, w_in, conv_w, mix_mu, w0, w_decay_up, a0, w_aaa_up, w_gate_up, k_k, k_a, r_k, ln_x_w, ln_x_b, w_out, norm2_g, w_router_group, b_router_group, w_router_expert, b_router_expert, w_gate, w_up, w_down, final_g):
    depth = w_ada.shape[0]
    assert depth == 1
    bp, tp, d = x_prompt.shape
    bs, ts, _ = x_sample.shape
    l = 0

    row = lambda z: z.reshape(1, -1)
    zeros_lora = jnp.zeros((LORA_DECAY, D_RWKV), F32)
    n_route = N_GROUPS + N_EXPERTS
    w_router = jnp.pad(jnp.concatenate([w_router_group[l], w_router_expert[l]], axis=1),
                       ((0, 0), (0, LANES - n_route)))
    b_router = jnp.concatenate([b_router_group[l], b_router_expert[l]])
    wr_hi = w_router.astype(BF16)
    lp = dict(
        norm1_g=row(norm1_g[l]), w_in=w_in[l], conv_w=conv_w[l], mix_mu=row(mix_mu[l]), w0=row(w0[l]),
        wd_pad=jnp.concatenate([w_decay_up[l], zeros_lora], axis=0), a0=row(a0[l]),
        wa_pad=jnp.concatenate([zeros_lora, w_aaa_up[l]], axis=0), w_gate_up=w_gate_up[l],
        k_k=row(k_k[l]), k_a=row(k_a[l]), r_k=row(r_k[l]), ln_x_w=row(ln_x_w[l]), ln_x_b=row(ln_x_b[l]),
        w_out_bf16=_cast_bf16(w_out[l]), norm2_g=row(norm2_g[l]),
        wr_hi=wr_hi, wr_lo=(w_router - wr_hi.astype(F32)).astype(BF16),
        b_router=jnp.pad(b_router, (0, LANES - n_route)).reshape(1, LANES),
    )

    n_c = bp + bs
    c_all = jnp.concatenate([c_prompt, c_sample], axis=0)
    c_all = jnp.pad(c_all, ((0, (-n_c) % SUBLANES), (0, 0)))
    mod = _ada(c_all, w_ada[l], row(b_ada[l]))
    mod_p = mod[:bp].reshape(bp, 1, N_MOD * d)
    mod_s = mod[bp:n_c].reshape(bs, 1, N_MOD * d)

    zc = jnp.zeros((bp, 2, D_CONV), F32)
    zs = jnp.zeros((bp, 1, D_RWKV_PROJ), F32)
    zw = jnp.zeros((bp, N_HEADS, HEAD_DIM, HEAD_DIM), F32)
    n_p = bp * tp
    x1_s, h2_s, eid_s, wts_s, conv_s, shift_s, wkv_s = _stream_mix(
        x_sample, mod_s, state_conv[l], state_shift[l].reshape(bs, 1, D_RWKV_PROJ), state_wkv[l], lp,
        nb=ROW_TILE // ts, tb=ts, tt=ts)
    x1_p, h2, eid_p, wts_p, conv_p, shift_p, wkv_p = _stream_mix(
        x_prompt, mod_p, zc, zs, zw, lp, nb=1, tb=ROW_TILE, tt=128, h2_tail=h2_s)

    eid = jnp.concatenate([eid_p[:, :2], eid_s[:, :2]], axis=0)
    dest, plan = _dispatch_plan(eid)
    ys = _experts(plan, h2, w_gate[l], w_up[l], w_down[l])
    y_prompt = _combine(dest[:2 * n_p], wts_p, x1_p, mod_p, row(final_g), ys, bp, tp, nb=1, tb=ROW_TILE)
    y_sample = _combine(dest[2 * n_p:], wts_s, x1_s, mod_s, row(final_g), ys, bs, ts, nb=ROW_TILE // ts, tb=ts)

    return (y_prompt, y_sample, conv_p[None], shift_p[None], wkv_p[None],
            conv_s[None], shift_s[None], wkv_s[None])
```

```python
import functools

import jax
import jax.numpy as jnp
from jax import lax
from jax.experimental import pallas as pl
from jax.experimental.pallas import tpu as pltpu

F32 = jnp.float32
BF16 = jnp.bfloat16
I32 = jnp.int32

D_MODEL = 2048
D_CONV = 1024
D_RWKV = 1024
HEAD_DIM = 64
N_HEADS = 16
LORA_DECAY = 64
LORA_AAA = 64
LORA_GATE = 128
D_RWKV_PROJ = 3 * D_RWKV + LORA_DECAY + LORA_AAA + LORA_GATE
D_IN_PROJ = 3 * D_CONV + D_RWKV_PROJ
N_GROUPS = 4
EXPERTS_PER_GROUP = 8
N_EXPERTS = 32
D_EXPERT = 512
N_MOD = 6
RMS_EPS = 1e-6
GN_EPS = HEAD_DIM * 1e-5

SUBLANES = 8
LANES = 128
VMEM_LIMIT_BYTES = 56 * 1024 * 1024

ROW_TILE = 256
EXPERT_TILE = 256
EXPERT_SLOTS = 3
EXPERT_ISSUE_GROUPS = 7
SEG = 256
D_TILES = D_MODEL // LANES
R_TILES = D_RWKV // LANES
GATHER_PITCH = D_TILES + SUBLANES
GATHER_DMA_PRIORITY = 0
WEIGHT_DMA_PRIORITY = 1

WKV_BATCHES = 4
V_LO = LANES // (WKV_BATCHES * R_TILES)
V_HI = HEAD_DIM // V_LO
N_WKV_OPERANDS = 6
STATE_PITCH = HEAD_DIM + SUBLANES


def _params(semantics):
    return pltpu.CompilerParams(dimension_semantics=semantics, vmem_limit_bytes=VMEM_LIMIT_BYTES)


def _sigmoid(x):
    return 1.0 / (1.0 + jnp.exp(-x))


def _store_row_tiled(ref, x, pitch):
    rows = x.shape[0]
    for s in range(x.shape[1] // LANES):
        ref[pl.ds(s, rows, stride=pitch), :] = x[:, s * LANES:(s + 1) * LANES]


def _load_row_tiled(ref, rows, n, pitch):
    return jnp.concatenate([ref[pl.ds(s, rows, stride=pitch), :] for s in range(n)], axis=-1)


def _head_ones():
    r = lax.broadcasted_iota(I32, (SEG, SEG), 0) // HEAD_DIM
    c = lax.broadcasted_iota(I32, (SEG, SEG), 1) // HEAD_DIM
    return jnp.where(r == c, 1.0, 0.0).astype(BF16)


def _split_bf16(x):
    hi = x.astype(BF16)
    return hi, (x - hi.astype(F32)).astype(BF16)


def _head_sum(x, ones):
    outs = []
    for s in range(x.shape[-1] // SEG):
        hi, lo = _split_bf16(x[:, s * SEG:(s + 1) * SEG])
        outs.append(jnp.dot(hi, ones, preferred_element_type=F32) + jnp.dot(lo, ones, preferred_element_type=F32))
    return jnp.concatenate(outs, axis=-1)


def _ada_kernel(c_ref, w_ref, b_ref, o_ref):
    c = c_ref[...]
    s = c * _sigmoid(c)
    o_ref[...] = jnp.dot(s.astype(BF16), w_ref[...].astype(BF16), preferred_element_type=F32) + b_ref[...]


def _ada(c_all, w_ada, b_ada):
    mb = c_all.shape[0]
    n = w_ada.shape[1]
    tn = 1024
    return pl.pallas_call(
        _ada_kernel,
        grid=(n // tn,),
        in_specs=[pl.BlockSpec((mb, D_MODEL), lambda j: (0, 0)),
                  pl.BlockSpec((D_MODEL, tn), lambda j: (0, j)),
                  pl.BlockSpec((1, tn), lambda j: (0, j))],
        out_specs=pl.BlockSpec((mb, tn), lambda j: (0, j)),
        out_shape=jax.ShapeDtypeStruct((mb, n), F32),
        compiler_params=_params(("arbitrary",)),
        name="ada",
    )(c_all, w_ada, b_ada)


def _norm1_kernel(x_ref, g_ref, sh_ref, sc_ref, o_ref):
    x = x_ref[...]
    nb, tb, d = x.shape
    ms = jnp.mean(x * x, axis=-1, keepdims=True)
    xn = x * lax.rsqrt(ms + RMS_EPS) * g_ref[...]
    h = xn * (1.0 + sc_ref[...]) + sh_ref[...]
    o_ref[...] = h.reshape(nb * tb, d).astype(BF16)


def _norm1(x, mod, g, nb, tb):
    b, t, d = x.shape
    nt = t // tb
    return pl.pallas_call(
        _norm1_kernel,
        grid=(b // nb, nt),
        in_specs=[pl.BlockSpec((nb, tb, d), lambda i, j: (i, j, 0)),
                  pl.BlockSpec((1, d), lambda i, j: (0, 0)),
                  pl.BlockSpec((nb, 1, d), lambda i, j: (i, 0, 0)),
                  pl.BlockSpec((nb, 1, d), lambda i, j: (i, 0, 1))],
        out_specs=pl.BlockSpec((nb * tb, d), lambda i, j: (i * nt + j, 0)),
        out_shape=jax.ShapeDtypeStruct((b * t, d), BF16),
        compiler_params=_params(("arbitrary", "arbitrary")),
        name="norm1",
    )(x, g, mod, mod)


def _inproj_kernel(h_ref, w_ref, o_ref, wb_ref):
    @pl.when(pl.program_id(1) == 0)
    def _():
        wb_ref[...] = w_ref[...].astype(BF16)

    o_ref[...] = jnp.dot(h_ref[...], wb_ref[...], preferred_element_type=F32)


def _inproj(h, w_in):
    m, k = h.shape
    n = w_in.shape[1]
    tm = min(m, 1024)
    tn = 1280
    return pl.pallas_call(
        _inproj_kernel,
        grid=(n // tn, m // tm),
        in_specs=[pl.BlockSpec((tm, k), lambda j, i: (i, 0)),
                  pl.BlockSpec((k, tn), lambda j, i: (0, j))],
        out_specs=pl.BlockSpec((tm, tn), lambda j, i: (i, j)),
        out_shape=jax.ShapeDtypeStruct((m, n), F32),
        scratch_shapes=[pltpu.VMEM((k, tn), BF16)],
        compiler_params=_params(("arbitrary", "arbitrary")),
        name="inproj",
    )(h, w_in)


def _store_time_major(ref, x, stage, nb, tb):
    pitch = tb * R_TILES + SUBLANES
    for bb in range(nb):
        xb = x[bb * tb:(bb + 1) * tb]
        for s in range(R_TILES):
            stage[pl.ds(bb * pitch + s, tb, stride=R_TILES), :] = xb[:, s * LANES:(s + 1) * LANES]
    for t in range(tb):
        for s in range(R_TILES):
            ref[t, s] = stage[pl.ds(t * R_TILES + s, nb, stride=pitch), :]


def _load_time_major(ref, stage, nb, tb):
    pitch = tb * R_TILES + SUBLANES
    for t in range(tb):
        for s in range(R_TILES):
            stage[pl.ds(t * R_TILES + s, nb, stride=pitch), :] = ref[t, s]
    rows = []
    for bb in range(nb):
        rows.append(jnp.concatenate(
            [stage[pl.ds(bb * pitch + s, tb, stride=R_TILES), :] for s in range(R_TILES)], axis=1))
    return jnp.concatenate(rows, axis=0)


def _mixprep_kernel(time_major, proj_ref, cst_ref, sst_ref, convw_ref, mu_ref, w0_ref, wd_ref, a0_ref, wa_ref,
                    wg_ref, kk_ref, ka_ref, rk_ref,
                    a_o, w_o, b_o, k_o, r_o, v_o, yc_o, g_o, bonus_o, nc_o, ns_o, *stage):
    @pl.when(pl.program_id(1) == 0)
    def _():
        nc_o[...] = cst_ref[...]
        ns_o[...] = sst_ref[...]

    nb, tb, _ = proj_ref.shape
    rows = nb * tb
    p = proj_ref[...].reshape(rows, D_IN_PROJ)
    tidx = lax.broadcasted_iota(I32, (rows, 1), 0) & (tb - 1)

    def per_row(c):
        return jnp.broadcast_to(c, (nb, tb, c.shape[-1])).reshape(rows, c.shape[-1])

    gate_b = p[:, 0:D_CONV]
    gate_c = p[:, D_CONV:2 * D_CONV]
    h_conv = p[:, 2 * D_CONV:3 * D_CONV]
    pr = p[:, 3 * D_CONV:]

    u = gate_c * h_conv
    carry = nc_o[...]
    c2 = per_row(carry[:, 0:1, :])
    c1 = per_row(carry[:, 1:2, :])
    u1 = jnp.where(tidx == 0, c1, pltpu.roll(u, 1, 0))
    u2 = jnp.where(tidx == 0, c2, jnp.where(tidx == 1, c1, pltpu.roll(u, 2, 0)))
    cw = convw_ref[...]
    conv_y = u * cw[2:3, :] + u2 * cw[0:1, :] + u1 * cw[1:2, :]
    yc_o[...] = (gate_b * conv_y).astype(BF16)
    nc_o[...] = u.reshape(nb, tb, D_CONV)[:, tb - 2:tb, :]

    p_prev = jnp.where(tidx == 0, per_row(ns_o[...]), pltpu.roll(pr, 1, 0))
    ns_o[...] = pr.reshape(nb, tb, D_RWKV_PROJ)[:, tb - 1:tb, :]
    pm = pr + (p_prev - pr) * mu_ref[...]
    r = pm[:, 0:D_RWKV]
    k = pm[:, D_RWKV:2 * D_RWKV]
    v = pm[:, 2 * D_RWKV:3 * D_RWKV]
    x_lora = pm[:, 3 * D_RWKV:3 * D_RWKV + LORA_DECAY + LORA_AAA]
    xg = pm[:, 3 * D_RWKV + LORA_DECAY + LORA_AAA:]

    d_pre = w0_ref[...] + jnp.dot(jnp.tanh(x_lora).astype(BF16), wd_ref[...].astype(BF16),
                                  preferred_element_type=F32)
    a_pre = a0_ref[...] + jnp.dot(x_lora.astype(BF16), wa_ref[...].astype(BF16), preferred_element_type=F32)
    g = jnp.dot(_sigmoid(xg).astype(BF16), wg_ref[...].astype(BF16), preferred_element_type=F32)
    z = -d_pre
    softplus = jnp.maximum(z, 0.0) + jnp.log(1.0 + jnp.exp(-jnp.abs(z)))
    w_log = -softplus - 0.5
    decay = jnp.exp(-jnp.exp(w_log))
    a_h = _sigmoid(a_pre)

    ones = _head_ones()
    kk = k * kk_ref[...]
    kk = kk * lax.rsqrt(jnp.maximum(_head_sum(kk * kk, ones), 1e-24))
    k_h = k * (1.0 + (a_h - 1.0) * ka_ref[...])
    bonus = _head_sum(r * k_h * rk_ref[...], ones) * v

    operands = ((a_o, -kk), (w_o, decay), (b_o, kk * a_h), (k_o, k_h), (r_o, r), (v_o, v))
    for ref, val in operands:
        if time_major:
            _store_time_major(ref, val, stage[0], nb, tb)
        else:
            _store_row_tiled(ref, val, R_TILES)
    g_o[...] = g
    bonus_o[...] = bonus


def _mixprep(proj3, conv_state, shift_state, lp, nb, tb, time_major):
    b, t, _ = proj3.shape
    nt = t // tb
    rows = nb * tb
    m = b * t
    assert not time_major or nt == 1

    def row_spec(width):
        return pl.BlockSpec((rows, width), lambda i, j: (i * nt + j, 0))

    def full2(a):
        return pl.BlockSpec(a.shape, lambda i, j: (0, 0))

    consts = (lp["conv_w"], lp["mix_mu"], lp["w0"], lp["wd_pad"], lp["a0"], lp["wa_pad"], lp["w_gate_up"],
              lp["k_k"], lp["k_a"], lp["r_k"])
    if time_major:
        tiled_spec = pl.BlockSpec((tb, R_TILES, nb, LANES), lambda i, j: (0, 0, i, 0))
        tiled_out = jax.ShapeDtypeStruct((t, R_TILES, b, LANES), F32)
        scratch = [pltpu.VMEM((nb * (tb * R_TILES + SUBLANES), LANES), F32)]
    else:
        tiled_spec = pl.BlockSpec((rows * R_TILES, LANES), lambda i, j: (i * nt + j, 0))
        tiled_out = jax.ShapeDtypeStruct((m * R_TILES, LANES), F32)
        scratch = []
    f32_out = jax.ShapeDtypeStruct((m, D_RWKV), F32)
    return pl.pallas_call(
        functools.partial(_mixprep_kernel, time_major),
        scratch_shapes=scratch,
        grid=(b // nb, nt),
        in_specs=[pl.BlockSpec((nb, tb, D_IN_PROJ), lambda i, j: (i, j, 0)),
                  pl.BlockSpec((nb, 2, D_CONV), lambda i, j: (i, 0, 0)),
                  pl.BlockSpec((nb, 1, D_RWKV_PROJ), lambda i, j: (i, 0, 0))] + [full2(c) for c in consts],
        out_specs=[tiled_spec] * N_WKV_OPERANDS + [row_spec(D_CONV), row_spec(D_RWKV), row_spec(D_RWKV),
                                                   pl.BlockSpec((nb, 2, D_CONV), lambda i, j: (i, 0, 0)),
                                                   pl.BlockSpec((nb, 1, D_RWKV_PROJ), lambda i, j: (i, 0, 0))],
        out_shape=[tiled_out] * N_WKV_OPERANDS + [jax.ShapeDtypeStruct((m, D_CONV), BF16), f32_out, f32_out,
                                                  jax.ShapeDtypeStruct((b, 2, D_CONV), F32),
                                                  jax.ShapeDtypeStruct((b, 1, D_RWKV_PROJ), F32)],
        compiler_params=_params(("arbitrary", "arbitrary")),
        name="mixprep",
    )(proj3, conv_state, shift_state, *consts)


def _wkv_kernel(a_ref, w_ref, b_ref, k_ref, r_ref, v_ref, s0_ref, y_ref, sout_ref, s_ref, tiles, yq, cbuf):
    n_pairs = WKV_BATCHES * R_TILES

    @pl.when(pl.program_id(1) == 0)
    def _():
        for bp in range(n_pairs):
            bb, hp = divmod(bp, R_TILES)
            cbuf[pl.ds(bp * STATE_PITCH, HEAD_DIM), :] = jnp.concatenate(
                [s0_ref[bb, 2 * hp], s0_ref[bb, 2 * hp + 1]], axis=1)
        for vh in range(V_HI):
            m = jnp.concatenate([cbuf[pl.ds(vh * V_LO + vl, n_pairs, stride=STATE_PITCH), :]
                                 for vl in range(V_LO)], axis=0)
            mt = m.T
            s_ref[0, vh] = mt[:HEAD_DIM]
            s_ref[1, vh] = mt[HEAD_DIM:]

    tt = a_ref.shape[1] // R_TILES
    srcs = (a_ref, w_ref, b_ref, k_ref, r_ref, v_ref)
    n_chain = WKV_BATCHES * R_TILES
    sub8 = lax.broadcasted_iota(I32, (SUBLANES, LANES), 0)
    grp8 = lax.broadcasted_iota(I32, (SUBLANES, LANES), 1) // n_chain
    diag8 = (sub8 % V_LO) == grp8
    sub4 = lax.broadcasted_iota(I32, (V_LO, LANES), 0)
    grp4 = lax.broadcasted_iota(I32, (V_LO, LANES), 1) // n_chain
    diag4 = sub4 == grp4

    def produce(t, slot):
        row0 = pl.multiple_of(t * R_TILES, R_TILES)
        for ai, ref in enumerate(srcs):
            m = jnp.concatenate([ref[bb, pl.ds(row0, R_TILES), :] for bb in range(WKV_BATCHES)], axis=0)
            xt = jnp.concatenate([m] * V_LO, axis=0).T
            if ai < N_WKV_OPERANDS - 1:
                tiles[slot, ai] = xt
            else:
                for q in range(LANES // SUBLANES):
                    z = jnp.where(diag8, xt[q * SUBLANES:(q + 1) * SUBLANES], 0.0)
                    z = z + pltpu.roll(z, 1, 0)
                    z = z + pltpu.roll(z, 2, 0)
                    tiles[slot, ai, q * SUBLANES:(q + 1) * SUBLANES, :] = z

    def consume(slot):
        for par in range(2):
            lo = par * HEAD_DIM
            a = tiles[slot, 0, lo:lo + HEAD_DIM, :]
            w = tiles[slot, 1, lo:lo + HEAD_DIM, :]
            b = tiles[slot, 2, lo:lo + HEAD_DIM, :]
            k = tiles[slot, 3, lo:lo + HEAD_DIM, :]
            r = tiles[slot, 4, lo:lo + HEAD_DIM, :]
            for vh in range(V_HI):
                s = s_ref[par, vh]
                vrow = tiles[slot, 5, lo + vh * V_LO + V_LO - 1:lo + (vh + 1) * V_LO, :]
                sa = jnp.sum(s * a, axis=0, keepdims=True)
                s = s * w + sa * b + vrow * k
                s_ref[par, vh] = s
                yrow = jnp.sum(s * r, axis=0, keepdims=True)
                yq[slot, lo + vh * V_LO:lo + (vh + 1) * V_LO, :] = jnp.where(
                    diag4, jnp.broadcast_to(yrow, (V_LO, LANES)), 0.0)

    def finalize(t, slot):
        qt = yq[slot].T
        res = qt[0:n_chain]
        for j in range(1, V_LO):
            res = res + qt[j * n_chain:(j + 1) * n_chain]
        row0 = pl.multiple_of(t * R_TILES, R_TILES)
        for bb in range(WKV_BATCHES):
            y_ref[bb, pl.ds(row0, R_TILES), :] = res[bb * R_TILES:(bb + 1) * R_TILES]

    produce(0, 0)
    produce(1, 1)
    consume(0)

    def body(i, c):
        t1 = 2 * i + 1
        produce(t1 + 1, 0)
        consume(1)
        finalize(t1 - 1, 0)
        produce(jnp.minimum(t1 + 2, tt - 1), 1)
        consume(0)
        finalize(t1, 1)
        return c

    lax.fori_loop(0, tt // 2 - 1, body, 0)
    consume(1)
    finalize(tt - 2, 0)
    finalize(tt - 1, 1)

    @pl.when(pl.program_id(1) == pl.num_programs(1) - 1)
    def _():
        for vh in range(V_HI):
            m = jnp.concatenate([s_ref[0, vh], s_ref[1, vh]], axis=0).T
            for vl in range(V_LO):
                cbuf[pl.ds(vh * V_LO + vl, n_pairs, stride=STATE_PITCH), :] = m[vl * n_pairs:(vl + 1) * n_pairs]
        for bp in range(n_pairs):
            bb, hp = divmod(bp, R_TILES)
            c = cbuf[pl.ds(bp * STATE_PITCH, HEAD_DIM), :]
            sout_ref[bb, 2 * hp] = c[:, :HEAD_DIM]
            sout_ref[bb, 2 * hp + 1] = c[:, HEAD_DIM:]


def _wkv(operands, s0, tt):
    bsz, rows, _ = operands[0].shape
    t = rows // R_TILES
    g = bsz // WKV_BATCHES
    assert tt % 2 == 0 and tt >= 4 and t % tt == 0
    blk = pl.BlockSpec((WKV_BATCHES, tt * R_TILES, LANES), lambda i, j: (i, j, 0))
    state = pl.BlockSpec((WKV_BATCHES, N_HEADS, HEAD_DIM, HEAD_DIM), lambda i, j: (i, 0, 0, 0))
    return pl.pallas_call(
        _wkv_kernel,
        grid=(g, t // tt),
        in_specs=[blk] * N_WKV_OPERANDS + [state],
        out_specs=[blk, state],
        out_shape=[jax.ShapeDtypeStruct((bsz, rows, LANES), F32),
                   jax.ShapeDtypeStruct((bsz, N_HEADS, HEAD_DIM, HEAD_DIM), F32)],
        scratch_shapes=[pltpu.VMEM((2, V_HI, HEAD_DIM, LANES), F32),
                        pltpu.VMEM((2, N_WKV_OPERANDS, LANES, LANES), F32), pltpu.VMEM((2, LANES, LANES), F32),
                        pltpu.VMEM((WKV_BATCHES * R_TILES * STATE_PITCH, LANES), F32)],
        compiler_params=_params(("arbitrary", "arbitrary")),
        name="wkv",
    )(*operands, s0)


def _wkv_batch_kernel(a_ref, w_ref, b_ref, k_ref, r_ref, v_ref, s0_ref, y_ref, s_ref, yq):
    s_ref[...] = s0_ref[...]
    tt = a_ref.shape[0]

    def step(t, carry):
        a, w, b, k, r, v = (ref[t, 0].T for ref in (a_ref, w_ref, b_ref, k_ref, r_ref, v_ref))
        for par in range(2):
            lo = par * HEAD_DIM
            ap, wp, bp, kp, rp = (z[lo:lo + HEAD_DIM] for z in (a, w, b, k, r))
            for vi in range(HEAD_DIM):
                s = s_ref[par, vi]
                sa = jnp.sum(s * ap, axis=0, keepdims=True)
                s = s * wp + sa * bp + v[lo + vi:lo + vi + 1] * kp
                s_ref[par, vi] = s
                yq[lo + vi:lo + vi + 1, :] = jnp.sum(s * rp, axis=0, keepdims=True)
        y_ref[t, 0] = yq[...].T
        return carry

    lax.fori_loop(0, tt, step, 0)


def _wkv_batch(operands, s0):
    t = operands[0].shape[0]
    assert operands[0].shape[2] == LANES and s0.shape == (N_HEADS, HEAD_DIM, HEAD_DIM, LANES)
    blk = pl.BlockSpec((t, 1, LANES, LANES), lambda i: (0, i, 0, 0))
    state = pl.BlockSpec((2, HEAD_DIM, HEAD_DIM, LANES), lambda i: (i, 0, 0, 0))
    return pl.pallas_call(
        _wkv_batch_kernel,
        grid=(R_TILES,),
        in_specs=[blk] * N_WKV_OPERANDS + [state],
        out_specs=[blk, state],
        out_shape=[jax.ShapeDtypeStruct((t, R_TILES, LANES, LANES), F32),
                   jax.ShapeDtypeStruct((N_HEADS, HEAD_DIM, HEAD_DIM, LANES), F32)],
        scratch_shapes=[pltpu.VMEM((LANES, LANES), F32)],
        compiler_params=_params(("arbitrary",)),
        name="wkv_batch",
    )(*operands, s0)


def _group_norm_gate(y, bonus, g, ln_w, ln_b):
    ones = _head_ones()
    mu = _head_sum(y, ones) * (1.0 / HEAD_DIM)
    d = y - mu
    var = _head_sum(d * d, ones) * (1.0 / HEAD_DIM)
    yn = d * lax.rsqrt(var + GN_EPS)
    yn = yn * ln_w + ln_b
    return (yn + bonus) * g


def _cast_kernel(x_ref, o_ref):
    o_ref[...] = x_ref[...].astype(o_ref.dtype)


def _cast_bf16(w):
    r, c = w.shape
    tr = 512
    return pl.pallas_call(
        _cast_kernel,
        grid=(r // tr,),
        in_specs=[pl.BlockSpec((tr, c), lambda i: (i, 0))],
        out_specs=pl.BlockSpec((tr, c), lambda i: (i, 0)),
        out_shape=jax.ShapeDtypeStruct((r, c), BF16),
        compiler_params=_params(("arbitrary",)),
        name="cast_bf16",
    )(w)


def _route(lg):
    lane = lax.broadcasted_iota(I32, lg.shape, 1)
    neg = -jnp.inf
    big = jnp.int32(1 << 20)

    is_g = lane < N_GROUPS
    gl = jnp.where(is_g, lg, neg)
    gmax = jnp.max(gl, axis=1, keepdims=True)
    gidx = jnp.min(jnp.where(gl == gmax, lane, big), axis=1, keepdims=True)
    gsum = jnp.sum(jnp.where(is_g, jnp.exp(gl - gmax), 0.0), axis=1, keepdims=True)
    gprob = 1.0 / gsum

    e_lane = lane - N_GROUPS
    in_grp = (e_lane >= 0) & (e_lane < N_EXPERTS) & ((e_lane >> 3) == gidx)
    el = jnp.where(in_grp, lg, neg)
    emax = jnp.max(el, axis=1, keepdims=True)
    ex = jnp.where(in_grp, jnp.exp(el - emax), 0.0)
    prob = ex / jnp.sum(ex, axis=1, keepdims=True)
    prob = jnp.where(in_grp, prob, -1.0)
    p1 = jnp.max(prob, axis=1, keepdims=True)
    i1 = jnp.min(jnp.where(prob == p1, lane, big), axis=1, keepdims=True)
    prob2 = jnp.where(lane == i1, -1.0, prob)
    p2 = jnp.max(prob2, axis=1, keepdims=True)
    i2 = jnp.min(jnp.where(prob2 == p2, lane, big), axis=1, keepdims=True)
    psum = p1 + p2
    w1 = p1 / psum * gprob
    w2 = p2 / psum * gprob
    eid = jnp.where(lane == 0, i1 - N_GROUPS, jnp.where(lane == 1, i2 - N_GROUPS, 0))
    wts = jnp.where(lane == 0, w1, jnp.where(lane == 1, w2, 0.0))
    return eid, wts


def _outproj_kernel(n_main, has_tail, time_major, yc_ref, y_ref, bonus_ref, gate_ref, lnw_ref, lnb_ref, wt_ref,
                    wb_ref, x_ref, ga1_ref, sh2_ref, sc2_ref, g2_ref, wrh_ref, wrl_ref, br_ref, *rest):
    rest = list(rest)
    tail_ref = rest.pop(0) if has_tail else None
    x1_o, h2_o, eid_o, wts_o = rest[:4]
    i = pl.program_id(0)

    @pl.when(i < n_main)
    def _():
        nb, tb, d = x_ref.shape
        rows = nb * tb
        if time_major:
            y = _load_time_major(y_ref, rest[4], nb, tb)
        else:
            y = _load_row_tiled(y_ref, rows, R_TILES, R_TILES)
        yr = _group_norm_gate(y, bonus_ref[...], gate_ref[...], lnw_ref[...], lnb_ref[...]).astype(BF16)
        mix = (jnp.dot(yc_ref[...], wt_ref[...], preferred_element_type=F32)
               + jnp.dot(yr, wb_ref[...], preferred_element_type=F32))
        x1 = x_ref[...] + ga1_ref[...] * mix.reshape(nb, tb, d)
        ms = jnp.mean(x1 * x1, axis=-1, keepdims=True)
        h2 = x1 * lax.rsqrt(ms + RMS_EPS) * g2_ref[...]
        h2 = (h2 * (1.0 + sc2_ref[...]) + sh2_ref[...]).reshape(rows, d)
        x1_o[...] = x1.reshape(rows, d)
        _store_row_tiled(h2_o, h2, D_TILES)
        h_hi, h_lo = _split_bf16(h2)
        w_hi = wrh_ref[...]
        logits = (jnp.dot(h_hi, w_hi, preferred_element_type=F32)
                  + jnp.dot(h_lo, w_hi, preferred_element_type=F32)
                  + jnp.dot(h_hi, wrl_ref[...], preferred_element_type=F32)) + br_ref[...]
        eid_o[...], wts_o[...] = _route(logits)

    if has_tail:
        @pl.when(i >= n_main)
        def _():
            h2_o[...] = tail_ref[...]


def _outproj(yc, y_wkv, bonus, gate, ln_w, ln_b, w_out_bf16, x, mod, g2, wr_hi, wr_lo, b_router, nb, tb,
             time_major, h2_tail=None):
    b, t, d = x.shape
    nt = t // tb
    rows = nb * tb
    m = b * t
    half = D_CONV
    n_main = (b // nb) * nt
    n_tail = 0 if h2_tail is None else h2_tail.shape[0] // (rows * D_TILES)
    blk = lambda i: jnp.minimum(i, n_main - 1)
    row = lambda width: pl.BlockSpec((rows, width), lambda i: (blk(i), 0))
    modspec = lambda col: pl.BlockSpec((nb, 1, d), lambda i: (blk(i) // nt, 0, col))
    full = lambda a: pl.BlockSpec(a.shape, lambda i: (0, 0))
    if time_major:
        assert nt == 1 and h2_tail is None
        y_spec = pl.BlockSpec((tb, R_TILES, nb, LANES), lambda i: (0, 0, blk(i), 0))
        scratch = [pltpu.VMEM((nb * (tb * R_TILES + SUBLANES), LANES), F32)]
    else:
        y_spec = pl.BlockSpec((rows * R_TILES, LANES), lambda i: (blk(i), 0))
        scratch = []
    in_specs = [row(half), y_spec, row(half), row(half),
                full(ln_w), full(ln_b),
                pl.BlockSpec((half, d), lambda i: (0, 0)),
                pl.BlockSpec((half, d), lambda i: (1, 0)),
                pl.BlockSpec((nb, tb, d), lambda i: (blk(i) // nt, blk(i) % nt, 0)),
                modspec(2), modspec(3), modspec(4),
                full(g2), full(wr_hi), full(wr_lo), full(b_router)]
    args = [yc, y_wkv, bonus, gate, ln_w, ln_b, w_out_bf16, w_out_bf16, x, mod, mod, mod, g2, wr_hi, wr_lo,
            b_router]
    if h2_tail is not None:
        in_specs.append(pl.BlockSpec((rows * D_TILES, LANES), lambda i: (jnp.maximum(i - n_main, 0), 0)))
        args.append(h2_tail)
    return pl.pallas_call(
        functools.partial(_outproj_kernel, n_main, h2_tail is not None, time_major),
        scratch_shapes=scratch,
        grid=(n_main + n_tail,),
        in_specs=in_specs,
        out_specs=[row(d), pl.BlockSpec((rows * D_TILES, LANES), lambda i: (i, 0)), row(LANES), row(LANES)],
        out_shape=[jax.ShapeDtypeStruct((m, d), F32),
                   jax.ShapeDtypeStruct(((n_main + n_tail) * rows * D_TILES, LANES), F32),
                   jax.ShapeDtypeStruct((m, LANES), I32), jax.ShapeDtypeStruct((m, LANES), F32)],
        compiler_params=_params(("arbitrary",)),
        name="outproj",
    )(*args)


def _row_copy(src_hbm, src_row, dst_ref, dst_row, sem):
    src = src_row * D_TILES
    dst = dst_row * GATHER_PITCH
    if not isinstance(src, int):
        src = pl.multiple_of(src, D_TILES)
    if not isinstance(dst, int):
        dst = pl.multiple_of(dst, SUBLANES)
    return pltpu.make_async_copy(src_hbm.at[pl.ds(src, D_TILES)], dst_ref.at[pl.ds(dst, D_TILES)], sem)


def _start_row_gather(index_of, src_hbm, dst_ref, sem, n):
    for r in range(n):
        _row_copy(src_hbm, index_of(r), dst_ref, r, sem).start(priority=GATHER_DMA_PRIORITY)


def _wait_row_gather(src_hbm, dst_ref, sem, n):
    def body(r, c):
        _row_copy(src_hbm, 0, dst_ref, r, sem).wait()
        return c
    lax.fori_loop(0, n, body, 0, unroll=8)


def _experts_kernel(n_tiles, te_ref, nu_ref, nxt_ref, wsl_ref, idx_ref, idx1_ref, idx2_ref, h_hbm, wg_hbm, wu_hbm,
                    wd_hbm, o_ref, xbuf, wgf, wuf, wdf, wgb, wub, wdb, xs, act, sems, wsems):
    i = pl.program_id(0)
    n_used = nu_ref[0]
    slot = lax.rem(i, EXPERT_SLOTS)
    slot2 = lax.rem(i + 2, EXPERT_SLOTS)
    bounds = [(EXPERT_TILE * g) // EXPERT_ISSUE_GROUPS for g in range(EXPERT_ISSUE_GROUPS + 1)]

    def issue_next(g):
        for r in range(bounds[g], bounds[g + 1]):
            _row_copy(h_hbm, idx2_ref[0, r], xbuf.at[slot2], r, sems.at[slot2]).start(priority=GATHER_DMA_PRIORITY)

    def weight_copies(e, ws):
        return (pltpu.make_async_copy(wg_hbm.at[e], wgf.at[ws], wsems.at[ws]),
                pltpu.make_async_copy(wu_hbm.at[e], wuf.at[ws], wsems.at[ws]),
                pltpu.make_async_copy(wd_hbm.at[e], wdf.at[ws], wsems.at[ws]))

    @pl.when(i == 0)
    def _():
        for c in weight_copies(te_ref[0], wsl_ref[0]):
            c.start(priority=WEIGHT_DMA_PRIORITY)
        _start_row_gather(lambda r: idx_ref[0, r], h_hbm, xbuf.at[0], sems.at[0], EXPERT_TILE)
        _start_row_gather(lambda r: idx1_ref[0, r], h_hbm, xbuf.at[1], sems.at[1], EXPERT_TILE)

    @pl.when(jnp.logical_and(i >= n_used, i < n_used + 2))
    def _():
        _wait_row_gather(h_hbm, xbuf.at[slot], sems.at[slot], EXPERT_TILE)

    @pl.when(i < n_used)
    def _():
        new_expert = jnp.logical_or(i == 0, te_ref[i] != te_ref[jnp.maximum(i - 1, 0)])

        @pl.when(new_expert)
        def _():
            ws = wsl_ref[i]
            for c in weight_copies(te_ref[i], ws):
                c.wait()
            wgb[...] = wgf[ws].astype(BF16)
            wub[...] = wuf[ws].astype(BF16)
            wdb[...] = wdf[ws].astype(BF16)

            @pl.when(nxt_ref[i] >= 0)
            def _():
                for c in weight_copies(nxt_ref[i], 1 - ws):
                    c.start(priority=WEIGHT_DMA_PRIORITY)

        _wait_row_gather(h_hbm, xbuf.at[slot], sems.at[slot], EXPERT_TILE)
        xs[...] = _load_row_tiled(xbuf.at[slot], EXPERT_TILE, D_TILES, GATHER_PITCH).astype(BF16)
        issue_next(0)
        half = D_EXPERT // 2
        for c in range(2):
            cols = slice(c * half, (c + 1) * half)
            gate = jnp.dot(xs[...], wgb[:, cols], preferred_element_type=F32)
            up = jnp.dot(xs[...], wub[:, cols], preferred_element_type=F32)
            act[:, cols] = (gate * _sigmoid(gate) * up).astype(BF16)
            issue_next(1 + c)
        n_chunks = EXPERT_ISSUE_GROUPS - 3
        width = D_MODEL // n_chunks
        for c in range(n_chunks):
            out = jnp.dot(act[...], wdb[:, c * width:(c + 1) * width], preferred_element_type=F32)
            for s in range(width // LANES):
                o_ref[pl.ds(c * (width // LANES) + s, EXPERT_TILE, stride=D_TILES), :] = out[:, s * LANES:(s + 1) * LANES]
            issue_next(3 + c)

    @pl.when(jnp.logical_and(i >= n_used, i < n_tiles))
    def _():
        o_ref[...] = jnp.zeros_like(o_ref)


def _experts(plan, h2_tiled, w_gate, w_up, w_down):
    tile_expert, n_used, next_expert, weight_slot, src_rows = plan
    n_tiles = tile_expert.shape[0]
    d = w_gate.shape[1]
    idx3 = src_rows.reshape(n_tiles, 1, EXPERT_TILE)
    smem_idx = lambda f: pl.BlockSpec((None, 1, EXPERT_TILE), f, memory_space=pltpu.SMEM)
    hbm = pl.BlockSpec(memory_space=pl.ANY)
    last = n_tiles - 1
    grid_spec = pltpu.PrefetchScalarGridSpec(
        num_scalar_prefetch=4,
        grid=(n_tiles + 2,),
        in_specs=[smem_idx(lambda i, *_: (jnp.minimum(i, last), 0, 0)),
                  smem_idx(lambda i, *_: (jnp.minimum(i + 1, last), 0, 0)),
                  smem_idx(lambda i, *_: (jnp.minimum(i + 2, last), 0, 0)),
                  hbm, hbm, hbm, hbm],
        out_specs=pl.BlockSpec((EXPERT_TILE * D_TILES, LANES), lambda i, *_: (jnp.minimum(i, last), 0)),
        scratch_shapes=[pltpu.VMEM((EXPERT_SLOTS, EXPERT_TILE * GATHER_PITCH, LANES), F32),
                        pltpu.VMEM((2, d, D_EXPERT), F32), pltpu.VMEM((2, d, D_EXPERT), F32),
                        pltpu.VMEM((2, D_EXPERT, d), F32),
                        pltpu.VMEM((d, D_EXPERT), BF16), pltpu.VMEM((d, D_EXPERT), BF16),
                        pltpu.VMEM((D_EXPERT, d), BF16),
                        pltpu.VMEM((EXPERT_TILE, d), BF16), pltpu.VMEM((EXPERT_TILE, D_EXPERT), BF16),
                        pltpu.SemaphoreType.DMA((EXPERT_SLOTS,)), pltpu.SemaphoreType.DMA((2,))],
    )
    return pl.pallas_call(
        functools.partial(_experts_kernel, n_tiles),
        grid_spec=grid_spec,
        out_shape=jax.ShapeDtypeStruct((n_tiles * EXPERT_TILE * D_TILES, LANES), F32),
        compiler_params=_params(("arbitrary",)),
        name="experts",
    )(tile_expert, n_used, next_expert, weight_slot, idx3, idx3, idx3, h2_tiled, w_gate, w_up, w_down)


def _combine_kernel(idx_ref, idxn_ref, wt_ref, x1_ref, ga2_ref, gf_ref, ys_hbm, o_ref, ybuf, sems):
    i = pl.program_id(0)
    n_steps = pl.num_programs(0)
    slot = i & 1
    nb, tb, d = o_ref.shape
    rows = nb * tb

    def fetch(idx, s):
        for j in range(2):
            _start_row_gather(lambda r: idx[0, 2 * r + j], ys_hbm, ybuf.at[s, j], sems.at[s], rows)

    @pl.when(i == 0)
    def _():
        fetch(idx_ref, 0)

    @pl.when(i + 1 < n_steps)
    def _():
        fetch(idxn_ref, 1 - slot)

    for j in range(2):
        _wait_row_gather(ys_hbm, ybuf.at[slot, j], sems.at[slot], rows)

    wt = wt_ref[...]
    moe = (wt[:, 0:1] * _load_row_tiled(ybuf.at[slot, 0], rows, D_TILES, GATHER_PITCH)
           + wt[:, 1:2] * _load_row_tiled(ybuf.at[slot, 1], rows, D_TILES, GATHER_PITCH))
    x2 = x1_ref[...].reshape(nb, tb, d) + ga2_ref[...] * moe.reshape(nb, tb, d)
    ms = jnp.mean(x2 * x2, axis=-1, keepdims=True)
    o_ref[...] = x2 * lax.rsqrt(ms + RMS_EPS) * gf_ref[...]


def _combine(dest, wts, x1, mod, final_g, ys, b, t, nb, tb):
    d = x1.shape[1]
    nt = t // tb
    rows = nb * tb
    n_steps = (b // nb) * nt
    idx3 = dest.reshape(n_steps, 1, 2 * rows)
    smem_idx = lambda f: pl.BlockSpec((None, 1, 2 * rows), f, memory_space=pltpu.SMEM)
    return pl.pallas_call(
        _combine_kernel,
        grid=(n_steps,),
        in_specs=[smem_idx(lambda i: (i, 0, 0)),
                  smem_idx(lambda i: (jnp.minimum(i + 1, n_steps - 1), 0, 0)),
                  pl.BlockSpec((rows, LANES), lambda i: (i, 0)),
                  pl.BlockSpec((rows, d), lambda i: (i, 0)),
                  pl.BlockSpec((nb, 1, d), lambda i: (i // nt, 0, 5)),
                  pl.BlockSpec((1, d), lambda i: (0, 0)),
                  pl.BlockSpec(memory_space=pl.ANY)],
        out_specs=pl.BlockSpec((nb, tb, d), lambda i: (i // nt, i % nt, 0)),
        out_shape=jax.ShapeDtypeStruct((b, t, d), F32),
        scratch_shapes=[pltpu.VMEM((2, 2, rows * GATHER_PITCH, LANES), F32), pltpu.SemaphoreType.DMA((2,))],
        compiler_params=_params(("arbitrary",)),
        name="combine",
    )(idx3, idx3, wts, x1, mod, final_g, ys)


def _dispatch_plan(eid):
    n_pairs = eid.shape[0] * 2
    n_tiles = n_pairs // EXPERT_TILE + N_EXPERTS
    flat = eid.reshape(-1)
    onehot = (flat[:, None] == jnp.arange(N_EXPERTS, dtype=I32)[None, :]).astype(I32)
    csum = jnp.cumsum(onehot, axis=0)
    rank = jnp.sum((csum - onehot) * onehot, axis=1)
    counts = csum[-1]
    tiles_per = (counts + EXPERT_TILE - 1) // EXPERT_TILE
    tile_end = jnp.cumsum(tiles_per)
    row_start = (tile_end - tiles_per) * EXPERT_TILE
    dest = jnp.sum(onehot * row_start[None, :], axis=1) + rank
    n_used = tile_end[-1]
    tile_ids = jnp.arange(n_tiles, dtype=I32)
    tile_expert = jnp.sum((tile_ids[:, None] >= tile_end[None, :]).astype(I32), axis=1)
    last_expert = jnp.max(jnp.where(counts > 0, jnp.arange(N_EXPERTS, dtype=I32), 0))
    tile_expert = jnp.where(tile_ids < n_used, tile_expert, last_expert).astype(I32)
    token = jnp.arange(n_pairs, dtype=I32) // 2
    filler = jnp.arange(n_tiles * EXPERT_TILE, dtype=I32) % (n_pairs // 2)
    src_rows = filler.at[dest].set(token, unique_indices=True, mode="promise_in_bounds")
    experts = jnp.arange(N_EXPERTS, dtype=I32)
    used = counts > 0
    later_used = used[None, :] & (experts[None, :] > experts[:, None])
    next_used = jnp.min(jnp.where(later_used, experts[None, :], N_EXPERTS), axis=1)
    next_used = jnp.where(next_used < N_EXPERTS, next_used, -1).astype(I32)
    slot_of = ((jnp.cumsum(used.astype(I32)) - 1) & 1).astype(I32)
    plan = (tile_expert, n_used.reshape(1).astype(I32), next_used[tile_expert], slot_of[tile_expert], src_rows)
    return dest.astype(I32), plan


def _stream_mix(x, mod, conv_state, shift_state, wkv_state, lp, nb, tb, tt, h2_tail=None):
    b, t, d = x.shape
    h = _norm1(x, mod, lp["norm1_g"], nb, tb)
    proj = _inproj(h, lp["w_in"])
    batch_lanes = b == LANES and tb == t
    outs = _mixprep(proj.reshape(b, t, D_IN_PROJ), conv_state, shift_state, lp, nb, tb, batch_lanes)
    yc, gate, bonus, new_conv, new_shift = outs[N_WKV_OPERANDS:]
    if batch_lanes:
        y_wkv, s_out = _wkv_batch(outs[:N_WKV_OPERANDS], jnp.transpose(wkv_state, (1, 2, 3, 0)))
        new_wkv = jnp.transpose(s_out, (3, 0, 1, 2))
    else:
        operands = [z.reshape(b, t * R_TILES, LANES) for z in outs[:N_WKV_OPERANDS]]
        y_wkv, new_wkv = _wkv(operands, wkv_state, tt)
        y_wkv = y_wkv.reshape(b * t * R_TILES, LANES)
    x1, h2_tiled, eid, wts = _outproj(yc, y_wkv, bonus, gate, lp["ln_x_w"], lp["ln_x_b"], lp["w_out_bf16"], x, mod,
                                      lp["norm2_g"], lp["wr_hi"], lp["wr_lo"], lp["b_router"], nb, tb, batch_lanes,
                                      h2_tail)
    return x1, h2_tiled, eid, wts, new_conv, new_shift.reshape(b, D_RWKV_PROJ), new_wkv


def kernel(x_prompt, x_sample, c_prompt, c_sample, state_conv, state_shift, state_wkv, w_ada, b_ada, norm1_g, w_in, conv_w, mix_mu, w0, w_decay_up, a0, w_aaa_up, w_gate_up, k_k, k_a, r_k, ln_x_w, ln_x_b, w_out, norm2_g, w_router_group, b_router_group, w_router_expert, b_router_expert, w_gate, w_up, w_down, final_g):
    depth = w_ada.shape[0]
    assert depth == 1
    bp, tp, d = x_prompt.shape
    bs, ts, _ = x_sample.shape
    l = 0

    row = lambda z: z.reshape(1, -1)
    zeros_lora = jnp.zeros((LORA_DECAY, D_RWKV), F32)
    n_route = N_GROUPS + N_EXPERTS
    w_router = jnp.pad(jnp.concatenate([w_router_group[l], w_router_expert[l]], axis=1),
                       ((0, 0), (0, LANES - n_route)))
    b_router = jnp.concatenate([b_router_group[l], b_router_expert[l]])
    wr_hi = w_router.astype(BF16)
    lp = dict(
        norm1_g=row(norm1_g[l]), w_in=w_in[l], conv_w=conv_w[l], mix_mu=row(mix_mu[l]), w0=row(w0[l]),
        wd_pad=jnp.concatenate([w_decay_up[l], zeros_lora], axis=0), a0=row(a0[l]),
        wa_pad=jnp.concatenate([zeros_lora, w_aaa_up[l]], axis=0), w_gate_up=w_gate_up[l],
        k_k=row(k_k[l]), k_a=row(k_a[l]), r_k=row(r_k[l]), ln_x_w=row(ln_x_w[l]), ln_x_b=row(ln_x_b[l]),
        w_out_bf16=_cast_bf16(w_out[l]), norm2_g=row(norm2_g[l]),
        wr_hi=wr_hi, wr_lo=(w_router - wr_hi.astype(F32)).astype(BF16),
        b_router=jnp.pad(b_router, (0, LANES - n_route)).reshape(1, LANES),
    )

    n_c = bp + bs
    c_all = jnp.concatenate([c_prompt, c_sample], axis=0)
    c_all = jnp.pad(c_all, ((0, (-n_c) % SUBLANES), (0, 0)))
    mod = _ada(c_all, w_ada[l], row(b_ada[l]))
    mod_p = mod[:bp].reshape(bp, 1, N_MOD * d)
    mod_s = mod[bp:n_c].reshape(bs, 1, N_MOD * d)

    zc = jnp.zeros((bp, 2, D_CONV), F32)
    zs = jnp.zeros((bp, 1, D_RWKV_PROJ), F32)
    zw = jnp.zeros((bp, N_HEADS, HEAD_DIM, HEAD_DIM), F32)
    n_p = bp * tp
    x1_s, h2_s, eid_s, wts_s, conv_s, shift_s, wkv_s = _stream_mix(
        x_sample, mod_s, state_conv[l], state_shift[l].reshape(bs, 1, D_RWKV_PROJ), state_wkv[l], lp,
        nb=ROW_TILE // ts, tb=ts, tt=ts)
    x1_p, h2, eid_p, wts_p, conv_p, shift_p, wkv_p = _stream_mix(
        x_prompt, mod_p, zc, zs, zw, lp, nb=1, tb=ROW_TILE, tt=128, h2_tail=h2_s)

    eid = jnp.concatenate([eid_p[:, :2], eid_s[:, :2]], axis=0)
    dest, plan = _dispatch_plan(eid)
    ys = _experts(plan, h2, w_gate[l], w_up[l], w_down[l])
    y_prompt = _combine(dest[:2 * n_p], wts_p, x1_p, mod_p, row(final_g), ys, bp, tp, nb=1, tb=ROW_TILE)
    y_sample = _combine(dest[2 * n_p:], wts_s, x1_s, mod_s, row(final_g), ys, bs, ts, nb=ROW_TILE // ts, tb=ts)

    return (y_prompt, y_sample, conv_p[None], shift_p[None], wkv_p[None],
            conv_s[None], shift_s[None], wkv_s[None])
```

```python
import functools

import jax
import jax.numpy as jnp
from jax import lax
from jax.experimental import pallas as pl
from jax.experimental.pallas import tpu as pltpu

F32 = jnp.float32
BF16 = jnp.bfloat16
I32 = jnp.int32

D_MODEL = 2048
D_CONV = 1024
D_RWKV = 1024
HEAD_DIM = 64
N_HEADS = 16
LORA_DECAY = 64
LORA_AAA = 64
LORA_GATE = 128
D_RWKV_PROJ = 3 * D_RWKV + LORA_DECAY + LORA_AAA + LORA_GATE
D_IN_PROJ = 3 * D_CONV + D_RWKV_PROJ
N_GROUPS = 4
EXPERTS_PER_GROUP = 8
N_EXPERTS = 32
D_EXPERT = 512
N_MOD = 6
RMS_EPS = 1e-6
GN_EPS = HEAD_DIM * 1e-5

SUBLANES = 8
LANES = 128
VMEM_LIMIT_BYTES = 56 * 1024 * 1024

ROW_TILE = 256
EXPERT_TILE = 256
ROUTE_ROWS = 40
EXPERT_SLOTS = 3
EXPERT_ISSUE_GROUPS = 7
SEG = 256
D_TILES = D_MODEL // LANES
R_TILES = D_RWKV // LANES
GATHER_PITCH = D_TILES + SUBLANES
GATHER_DMA_PRIORITY = 0
WEIGHT_DMA_PRIORITY = 1

WKV_BATCHES = 4
V_LO = LANES // (WKV_BATCHES * R_TILES)
V_HI = HEAD_DIM // V_LO
N_WKV_OPERANDS = 6
STATE_PITCH = HEAD_DIM + SUBLANES


def _params(semantics):
    return pltpu.CompilerParams(dimension_semantics=semantics, vmem_limit_bytes=VMEM_LIMIT_BYTES)


def _sigmoid(x):
    return 1.0 / (1.0 + jnp.exp(-x))


def _store_row_tiled(ref, x, pitch):
    rows = x.shape[0]
    for s in range(x.shape[1] // LANES):
        ref[pl.ds(s, rows, stride=pitch), :] = x[:, s * LANES:(s + 1) * LANES]


def _load_row_tiled(ref, rows, n, pitch):
    return jnp.concatenate([ref[pl.ds(s, rows, stride=pitch), :] for s in range(n)], axis=-1)


def _head_ones():
    r = lax.broadcasted_iota(I32, (SEG, SEG), 0) // HEAD_DIM
    c = lax.broadcasted_iota(I32, (SEG, SEG), 1) // HEAD_DIM
    return jnp.where(r == c, 1.0, 0.0).astype(BF16)


def _split_bf16(x):
    hi = x.astype(BF16)
    return hi, (x - hi.astype(F32)).astype(BF16)


def _head_sum(x, ones):
    outs = []
    for s in range(x.shape[-1] // SEG):
        hi, lo = _split_bf16(x[:, s * SEG:(s + 1) * SEG])
        outs.append(jnp.dot(hi, ones, preferred_element_type=F32) + jnp.dot(lo, ones, preferred_element_type=F32))
    return jnp.concatenate(outs, axis=-1)


def _ada_kernel(c_ref, w_ref, b_ref, o_ref):
    c = c_ref[...]
    s = c * _sigmoid(c)
    o_ref[...] = jnp.dot(s.astype(BF16), w_ref[...].astype(BF16), preferred_element_type=F32) + b_ref[...]


def _ada(c_all, w_ada, b_ada):
    mb = c_all.shape[0]
    n = w_ada.shape[1]
    tn = 1024
    return pl.pallas_call(
        _ada_kernel,
        grid=(n // tn,),
        in_specs=[pl.BlockSpec((mb, D_MODEL), lambda j: (0, 0)),
                  pl.BlockSpec((D_MODEL, tn), lambda j: (0, j)),
                  pl.BlockSpec((1, tn), lambda j: (0, j))],
        out_specs=pl.BlockSpec((mb, tn), lambda j: (0, j)),
        out_shape=jax.ShapeDtypeStruct((mb, n), F32),
        compiler_params=_params(("arbitrary",)),
        name="ada",
    )(c_all, w_ada, b_ada)


def _norm1_kernel(x_ref, g_ref, sh_ref, sc_ref, o_ref):
    x = x_ref[...]
    nb, tb, d = x.shape
    ms = jnp.mean(x * x, axis=-1, keepdims=True)
    xn = x * lax.rsqrt(ms + RMS_EPS) * g_ref[...]
    h = xn * (1.0 + sc_ref[...]) + sh_ref[...]
    o_ref[...] = h.reshape(nb * tb, d).astype(BF16)


def _norm1(x, mod, g, nb, tb):
    b, t, d = x.shape
    nt = t // tb
    return pl.pallas_call(
        _norm1_kernel,
        grid=(b // nb, nt),
        in_specs=[pl.BlockSpec((nb, tb, d), lambda i, j: (i, j, 0)),
                  pl.BlockSpec((1, d), lambda i, j: (0, 0)),
                  pl.BlockSpec((nb, 1, d), lambda i, j: (i, 0, 0)),
                  pl.BlockSpec((nb, 1, d), lambda i, j: (i, 0, 1))],
        out_specs=pl.BlockSpec((nb * tb, d), lambda i, j: (i * nt + j, 0)),
        out_shape=jax.ShapeDtypeStruct((b * t, d), BF16),
        compiler_params=_params(("arbitrary", "arbitrary")),
        name="norm1",
    )(x, g, mod, mod)


def _inproj_kernel(h_ref, w_ref, o_ref, wb_ref):
    @pl.when(pl.program_id(1) == 0)
    def _():
        wb_ref[...] = w_ref[...].astype(BF16)

    o_ref[...] = jnp.dot(h_ref[...], wb_ref[...], preferred_element_type=F32)


def _inproj(h, w_in):
    m, k = h.shape
    n = w_in.shape[1]
    tm = min(m, 1024)
    tn = 1280
    return pl.pallas_call(
        _inproj_kernel,
        grid=(n // tn, m // tm),
        in_specs=[pl.BlockSpec((tm, k), lambda j, i: (i, 0)),
                  pl.BlockSpec((k, tn), lambda j, i: (0, j))],
        out_specs=pl.BlockSpec((tm, tn), lambda j, i: (i, j)),
        out_shape=jax.ShapeDtypeStruct((m, n), F32),
        scratch_shapes=[pltpu.VMEM((k, tn), BF16)],
        compiler_params=_params(("arbitrary", "arbitrary")),
        name="inproj",
    )(h, w_in)


def _store_time_major(ref, x, stage, nb, tb):
    pitch = tb * R_TILES + SUBLANES
    for bb in range(nb):
        xb = x[bb * tb:(bb + 1) * tb]
        for s in range(R_TILES):
            stage[pl.ds(bb * pitch + s, tb, stride=R_TILES), :] = xb[:, s * LANES:(s + 1) * LANES]
    for t in range(tb):
        for s in range(R_TILES):
            ref[t, s] = stage[pl.ds(t * R_TILES + s, nb, stride=pitch), :]


def _load_time_major(ref, stage, nb, tb):
    pitch = tb * R_TILES + SUBLANES
    for t in range(tb):
        for s in range(R_TILES):
            stage[pl.ds(t * R_TILES + s, nb, stride=pitch), :] = ref[t, s]
    rows = []
    for bb in range(nb):
        rows.append(jnp.concatenate(
            [stage[pl.ds(bb * pitch + s, tb, stride=R_TILES), :] for s in range(R_TILES)], axis=1))
    return jnp.concatenate(rows, axis=0)


def _mixprep_kernel(time_major, proj_ref, cst_ref, sst_ref, convw_ref, mu_ref, w0_ref, wd_ref, a0_ref, wa_ref,
                    wg_ref, kk_ref, ka_ref, rk_ref,
                    a_o, w_o, b_o, k_o, r_o, v_o, yc_o, g_o, bonus_o, nc_o, ns_o, *stage):
    @pl.when(pl.program_id(1) == 0)
    def _():
        nc_o[...] = cst_ref[...]
        ns_o[...] = sst_ref[...]

    nb, tb, _ = proj_ref.shape
    rows = nb * tb
    p = proj_ref[...].reshape(rows, D_IN_PROJ)
    tidx = lax.broadcasted_iota(I32, (rows, 1), 0) & (tb - 1)

    def per_row(c):
        return jnp.broadcast_to(c, (nb, tb, c.shape[-1])).reshape(rows, c.shape[-1])

    gate_b = p[:, 0:D_CONV]
    gate_c = p[:, D_CONV:2 * D_CONV]
    h_conv = p[:, 2 * D_CONV:3 * D_CONV]
    pr = p[:, 3 * D_CONV:]

    u = gate_c * h_conv
    carry = nc_o[...]
    c2 = per_row(carry[:, 0:1, :])
    c1 = per_row(carry[:, 1:2, :])
    u1 = jnp.where(tidx == 0, c1, pltpu.roll(u, 1, 0))
    u2 = jnp.where(tidx == 0, c2, jnp.where(tidx == 1, c1, pltpu.roll(u, 2, 0)))
    cw = convw_ref[...]
    conv_y = u * cw[2:3, :] + u2 * cw[0:1, :] + u1 * cw[1:2, :]
    yc_o[...] = (gate_b * conv_y).astype(BF16)
    nc_o[...] = u.reshape(nb, tb, D_CONV)[:, tb - 2:tb, :]

    p_prev = jnp.where(tidx == 0, per_row(ns_o[...]), pltpu.roll(pr, 1, 0))
    ns_o[...] = pr.reshape(nb, tb, D_RWKV_PROJ)[:, tb - 1:tb, :]
    pm = pr + (p_prev - pr) * mu_ref[...]
    r = pm[:, 0:D_RWKV]
    k = pm[:, D_RWKV:2 * D_RWKV]
    v = pm[:, 2 * D_RWKV:3 * D_RWKV]
    x_lora = pm[:, 3 * D_RWKV:3 * D_RWKV + LORA_DECAY + LORA_AAA]
    xg = pm[:, 3 * D_RWKV + LORA_DECAY + LORA_AAA:]

    d_pre = w0_ref[...] + jnp.dot(jnp.tanh(x_lora).astype(BF16), wd_ref[...].astype(BF16),
                                  preferred_element_type=F32)
    a_pre = a0_ref[...] + jnp.dot(x_lora.astype(BF16), wa_ref[...].astype(BF16), preferred_element_type=F32)
    g = jnp.dot(_sigmoid(xg).astype(BF16), wg_ref[...].astype(BF16), preferred_element_type=F32)
    z = -d_pre
    softplus = jnp.maximum(z, 0.0) + jnp.log(1.0 + jnp.exp(-jnp.abs(z)))
    w_log = -softplus - 0.5
    decay = jnp.exp(-jnp.exp(w_log))
    a_h = _sigmoid(a_pre)

    ones = _head_ones()
    kk = k * kk_ref[...]
    kk = kk * lax.rsqrt(jnp.maximum(_head_sum(kk * kk, ones), 1e-24))
    k_h = k * (1.0 + (a_h - 1.0) * ka_ref[...])
    bonus = _head_sum(r * k_h * rk_ref[...], ones) * v

    operands = ((a_o, -kk), (w_o, decay), (b_o, kk * a_h), (k_o, k_h), (r_o, r), (v_o, v))
    for ref, val in operands:
        if time_major:
            _store_time_major(ref, val, stage[0], nb, tb)
        else:
            _store_row_tiled(ref, val, R_TILES)
    g_o[...] = g
    bonus_o[...] = bonus


def _mixprep(proj3, conv_state, shift_state, lp, nb, tb, time_major):
    b, t, _ = proj3.shape
    nt = t // tb
    rows = nb * tb
    m = b * t
    assert not time_major or nt == 1

    def row_spec(width):
        return pl.BlockSpec((rows, width), lambda i, j: (i * nt + j, 0))

    def full2(a):
        return pl.BlockSpec(a.shape, lambda i, j: (0, 0))

    consts = (lp["conv_w"], lp["mix_mu"], lp["w0"], lp["wd_pad"], lp["a0"], lp["wa_pad"], lp["w_gate_up"],
              lp["k_k"], lp["k_a"], lp["r_k"])
    if time_major:
        tiled_spec = pl.BlockSpec((tb, R_TILES, nb, LANES), lambda i, j: (0, 0, i, 0))
        tiled_out = jax.ShapeDtypeStruct((t, R_TILES, b, LANES), F32)
        scratch = [pltpu.VMEM((nb * (tb * R_TILES + SUBLANES), LANES), F32)]
    else:
        tiled_spec = pl.BlockSpec((rows * R_TILES, LANES), lambda i, j: (i * nt + j, 0))
        tiled_out = jax.ShapeDtypeStruct((m * R_TILES, LANES), F32)
        scratch = []
    f32_out = jax.ShapeDtypeStruct((m, D_RWKV), F32)
    return pl.pallas_call(
        functools.partial(_mixprep_kernel, time_major),
        scratch_shapes=scratch,
        grid=(b // nb, nt),
        in_specs=[pl.BlockSpec((nb, tb, D_IN_PROJ), lambda i, j: (i, j, 0)),
                  pl.BlockSpec((nb, 2, D_CONV), lambda i, j: (i, 0, 0)),
                  pl.BlockSpec((nb, 1, D_RWKV_PROJ), lambda i, j: (i, 0, 0))] + [full2(c) for c in consts],
        out_specs=[tiled_spec] * N_WKV_OPERANDS + [row_spec(D_CONV), row_spec(D_RWKV), row_spec(D_RWKV),
                                                   pl.BlockSpec((nb, 2, D_CONV), lambda i, j: (i, 0, 0)),
                                                   pl.BlockSpec((nb, 1, D_RWKV_PROJ), lambda i, j: (i, 0, 0))],
        out_shape=[tiled_out] * N_WKV_OPERANDS + [jax.ShapeDtypeStruct((m, D_CONV), BF16), f32_out, f32_out,
                                                  jax.ShapeDtypeStruct((b, 2, D_CONV), F32),
                                                  jax.ShapeDtypeStruct((b, 1, D_RWKV_PROJ), F32)],
        compiler_params=_params(("arbitrary", "arbitrary")),
        name="mixprep",
    )(proj3, conv_state, shift_state, *consts)


def _wkv_kernel(a_ref, w_ref, b_ref, k_ref, r_ref, v_ref, s0_ref, y_ref, sout_ref, s_ref, tiles, yq, cbuf):
    n_pairs = WKV_BATCHES * R_TILES

    @pl.when(pl.program_id(1) == 0)
    def _():
        for bp in range(n_pairs):
            bb, hp = divmod(bp, R_TILES)
            cbuf[pl.ds(bp * STATE_PITCH, HEAD_DIM), :] = jnp.concatenate(
                [s0_ref[bb, 2 * hp], s0_ref[bb, 2 * hp + 1]], axis=1)
        for vh in range(V_HI):
            m = jnp.concatenate([cbuf[pl.ds(vh * V_LO + vl, n_pairs, stride=STATE_PITCH), :]
                                 for vl in range(V_LO)], axis=0)
            mt = m.T
            s_ref[0, vh] = mt[:HEAD_DIM]
            s_ref[1, vh] = mt[HEAD_DIM:]

    tt = a_ref.shape[1] // R_TILES
    srcs = (a_ref, w_ref, b_ref, k_ref, r_ref, v_ref)
    n_chain = WKV_BATCHES * R_TILES
    sub8 = lax.broadcasted_iota(I32, (SUBLANES, LANES), 0)
    grp8 = lax.broadcasted_iota(I32, (SUBLANES, LANES), 1) // n_chain
    diag8 = (sub8 % V_LO) == grp8
    sub4 = lax.broadcasted_iota(I32, (V_LO, LANES), 0)
    grp4 = lax.broadcasted_iota(I32, (V_LO, LANES), 1) // n_chain
    diag4 = sub4 == grp4

    def produce(t, slot):
        row0 = pl.multiple_of(t * R_TILES, R_TILES)
        for ai, ref in enumerate(srcs):
            m = jnp.concatenate([ref[bb, pl.ds(row0, R_TILES), :] for bb in range(WKV_BATCHES)], axis=0)
            xt = jnp.concatenate([m] * V_LO, axis=0).T
            if ai < N_WKV_OPERANDS - 1:
                tiles[slot, ai] = xt
            else:
                for q in range(LANES // SUBLANES):
                    z = jnp.where(diag8, xt[q * SUBLANES:(q + 1) * SUBLANES], 0.0)
                    z = z + pltpu.roll(z, 1, 0)
                    z = z + pltpu.roll(z, 2, 0)
                    tiles[slot, ai, q * SUBLANES:(q + 1) * SUBLANES, :] = z

    def consume(slot):
        for par in range(2):
            lo = par * HEAD_DIM
            a = tiles[slot, 0, lo:lo + HEAD_DIM, :]
            w = tiles[slot, 1, lo:lo + HEAD_DIM, :]
            b = tiles[slot, 2, lo:lo + HEAD_DIM, :]
            k = tiles[slot, 3, lo:lo + HEAD_DIM, :]
            r = tiles[slot, 4, lo:lo + HEAD_DIM, :]
            for vh in range(V_HI):
                s = s_ref[par, vh]
                vrow = tiles[slot, 5, lo + vh * V_LO + V_LO - 1:lo + (vh + 1) * V_LO, :]
                sa = jnp.sum(s * a, axis=0, keepdims=True)
                s = s * w + sa * b + vrow * k
                s_ref[par, vh] = s
                yrow = jnp.sum(s * r, axis=0, keepdims=True)
                yq[slot, lo + vh * V_LO:lo + (vh + 1) * V_LO, :] = jnp.where(
                    diag4, jnp.broadcast_to(yrow, (V_LO, LANES)), 0.0)

    def finalize(t, slot):
        qt = yq[slot].T
        res = qt[0:n_chain]
        for j in range(1, V_LO):
            res = res + qt[j * n_chain:(j + 1) * n_chain]
        row0 = pl.multiple_of(t * R_TILES, R_TILES)
        for bb in range(WKV_BATCHES):
            y_ref[bb, pl.ds(row0, R_TILES), :] = res[bb * R_TILES:(bb + 1) * R_TILES]

    produce(0, 0)
    produce(1, 1)
    consume(0)

    def body(i, c):
        t1 = 2 * i + 1
        produce(t1 + 1, 0)
        consume(1)
        finalize(t1 - 1, 0)
        produce(jnp.minimum(t1 + 2, tt - 1), 1)
        consume(0)
        finalize(t1, 1)
        return c

    lax.fori_loop(0, tt // 2 - 1, body, 0)
    consume(1)
    finalize(tt - 2, 0)
    finalize(tt - 1, 1)

    @pl.when(pl.program_id(1) == pl.num_programs(1) - 1)
    def _():
        for vh in range(V_HI):
            m = jnp.concatenate([s_ref[0, vh], s_ref[1, vh]], axis=0).T
            for vl in range(V_LO):
                cbuf[pl.ds(vh * V_LO + vl, n_pairs, stride=STATE_PITCH), :] = m[vl * n_pairs:(vl + 1) * n_pairs]
        for bp in range(n_pairs):
            bb, hp = divmod(bp, R_TILES)
            c = cbuf[pl.ds(bp * STATE_PITCH, HEAD_DIM), :]
            sout_ref[bb, 2 * hp] = c[:, :HEAD_DIM]
            sout_ref[bb, 2 * hp + 1] = c[:, HEAD_DIM:]


def _wkv(operands, s0, tt):
    bsz, rows, _ = operands[0].shape
    t = rows // R_TILES
    g = bsz // WKV_BATCHES
    assert tt % 2 == 0 and tt >= 4 and t % tt == 0
    blk = pl.BlockSpec((WKV_BATCHES, tt * R_TILES, LANES), lambda i, j: (i, j, 0))
    state = pl.BlockSpec((WKV_BATCHES, N_HEADS, HEAD_DIM, HEAD_DIM), lambda i, j: (i, 0, 0, 0))
    return pl.pallas_call(
        _wkv_kernel,
        grid=(g, t // tt),
        in_specs=[blk] * N_WKV_OPERANDS + [state],
        out_specs=[blk, state],
        out_shape=[jax.ShapeDtypeStruct((bsz, rows, LANES), F32),
                   jax.ShapeDtypeStruct((bsz, N_HEADS, HEAD_DIM, HEAD_DIM), F32)],
        scratch_shapes=[pltpu.VMEM((2, V_HI, HEAD_DIM, LANES), F32),
                        pltpu.VMEM((2, N_WKV_OPERANDS, LANES, LANES), F32), pltpu.VMEM((2, LANES, LANES), F32),
                        pltpu.VMEM((WKV_BATCHES * R_TILES * STATE_PITCH, LANES), F32)],
        compiler_params=_params(("arbitrary", "arbitrary")),
        name="wkv",
    )(*operands, s0)


def _wkv_batch_kernel(a_ref, w_ref, b_ref, k_ref, r_ref, v_ref, s0_ref, y_ref, s_ref, yq):
    s_ref[...] = s0_ref[...]
    tt = a_ref.shape[0]

    def step(t, carry):
        a, w, b, k, r, v = (ref[t, 0].T for ref in (a_ref, w_ref, b_ref, k_ref, r_ref, v_ref))
        for par in range(2):
            lo = par * HEAD_DIM
            ap, wp, bp, kp, rp = (z[lo:lo + HEAD_DIM] for z in (a, w, b, k, r))
            for vi in range(HEAD_DIM):
                s = s_ref[par, vi]
                sa = jnp.sum(s * ap, axis=0, keepdims=True)
                s = s * wp + sa * bp + v[lo + vi:lo + vi + 1] * kp
                s_ref[par, vi] = s
                yq[lo + vi:lo + vi + 1, :] = jnp.sum(s * rp, axis=0, keepdims=True)
        y_ref[t, 0] = yq[...].T
        return carry

    lax.fori_loop(0, tt, step, 0)


def _wkv_batch(operands, s0):
    t = operands[0].shape[0]
    assert operands[0].shape[2] == LANES and s0.shape == (N_HEADS, HEAD_DIM, HEAD_DIM, LANES)
    blk = pl.BlockSpec((t, 1, LANES, LANES), lambda i: (0, i, 0, 0))
    state = pl.BlockSpec((2, HEAD_DIM, HEAD_DIM, LANES), lambda i: (i, 0, 0, 0))
    return pl.pallas_call(
        _wkv_batch_kernel,
        grid=(R_TILES,),
        in_specs=[blk] * N_WKV_OPERANDS + [state],
        out_specs=[blk, state],
        out_shape=[jax.ShapeDtypeStruct((t, R_TILES, LANES, LANES), F32),
                   jax.ShapeDtypeStruct((N_HEADS, HEAD_DIM, HEAD_DIM, LANES), F32)],
        scratch_shapes=[pltpu.VMEM((LANES, LANES), F32)],
        compiler_params=_params(("arbitrary",)),
        name="wkv_batch",
    )(*operands, s0)


def _group_norm_gate(y, bonus, g, ln_w, ln_b):
    ones = _head_ones()
    mu = _head_sum(y, ones) * (1.0 / HEAD_DIM)
    d = y - mu
    var = _head_sum(d * d, ones) * (1.0 / HEAD_DIM)
    yn = d * lax.rsqrt(var + GN_EPS)
    yn = yn * ln_w + ln_b
    return (yn + bonus) * g


def _cast_kernel(x_ref, o_ref):
    o_ref[...] = x_ref[...].astype(o_ref.dtype)


def _cast_bf16(w):
    r, c = w.shape
    tr = 512
    return pl.pallas_call(
        _cast_kernel,
        grid=(r // tr,),
        in_specs=[pl.BlockSpec((tr, c), lambda i: (i, 0))],
        out_specs=pl.BlockSpec((tr, c), lambda i: (i, 0)),
        out_shape=jax.ShapeDtypeStruct((r, c), BF16),
        compiler_params=_params(("arbitrary",)),
        name="cast_bf16",
    )(w)


def _route(lg):
    t = lg.T[:ROUTE_ROWS]
    row = lax.broadcasted_iota(I32, t.shape, 0)
    neg = -jnp.inf
    big = jnp.int32(1 << 20)

    is_g = row < N_GROUPS
    gl = jnp.where(is_g, t, neg)
    gmax = jnp.max(gl, axis=0, keepdims=True)
    gidx = jnp.min(jnp.where(gl == gmax, row, big), axis=0, keepdims=True)
    gsum = jnp.sum(jnp.where(is_g, jnp.exp(gl - gmax), 0.0), axis=0, keepdims=True)
    gprob = 1.0 / gsum

    e_row = row - N_GROUPS
    in_grp = (e_row >= 0) & (e_row < N_EXPERTS) & ((e_row >> 3) == gidx)
    el = jnp.where(in_grp, t, neg)
    emax = jnp.max(el, axis=0, keepdims=True)
    ex = jnp.where(in_grp, jnp.exp(el - emax), 0.0)
    prob = ex / jnp.sum(ex, axis=0, keepdims=True)
    prob = jnp.where(in_grp, prob, -1.0)
    p1 = jnp.max(prob, axis=0, keepdims=True)
    i1 = jnp.min(jnp.where(prob == p1, row, big), axis=0, keepdims=True)
    prob2 = jnp.where(row == i1, -1.0, prob)
    p2 = jnp.max(prob2, axis=0, keepdims=True)
    i2 = jnp.min(jnp.where(prob2 == p2, row, big), axis=0, keepdims=True)
    psum = p1 + p2
    w1 = p1 / psum * gprob
    w2 = p2 / psum * gprob
    sub = lax.broadcasted_iota(I32, (SUBLANES, t.shape[1]), 0)
    eid = jnp.where(sub == 0, i1 - N_GROUPS, jnp.where(sub == 1, i2 - N_GROUPS, 0))
    wts = jnp.where(sub == 0, w1, jnp.where(sub == 1, w2, 0.0))
    return eid, wts


def _outproj_kernel(n_main, has_tail, time_major, yc_ref, y_ref, bonus_ref, gate_ref, lnw_ref, lnb_ref, wt_ref,
                    wb_ref, x_ref, ga1_ref, sh2_ref, sc2_ref, g2_ref, wrh_ref, wrl_ref, br_ref, *rest):
    rest = list(rest)
    tail_ref = rest.pop(0) if has_tail else None
    x1_o, h2_o, eid_o, wts_o = rest[:4]
    i = pl.program_id(0)

    @pl.when(i < n_main)
    def _():
        nb, tb, d = x_ref.shape
        rows = nb * tb
        if time_major:
            y = _load_time_major(y_ref, rest[4], nb, tb)
        else:
            y = _load_row_tiled(y_ref, rows, R_TILES, R_TILES)
        yr = _group_norm_gate(y, bonus_ref[...], gate_ref[...], lnw_ref[...], lnb_ref[...]).astype(BF16)
        mix = (jnp.dot(yc_ref[...], wt_ref[...], preferred_element_type=F32)
               + jnp.dot(yr, wb_ref[...], preferred_element_type=F32))
        x1 = x_ref[...] + ga1_ref[...] * mix.reshape(nb, tb, d)
        ms = jnp.mean(x1 * x1, axis=-1, keepdims=True)
        h2 = x1 * lax.rsqrt(ms + RMS_EPS) * g2_ref[...]
        h2 = (h2 * (1.0 + sc2_ref[...]) + sh2_ref[...]).reshape(rows, d)
        x1_o[...] = x1.reshape(rows, d)
        _store_row_tiled(h2_o, h2, D_TILES)
        h_hi, h_lo = _split_bf16(h2)
        w_hi = wrh_ref[...]
        logits = (jnp.dot(h_hi, w_hi, preferred_element_type=F32)
                  + jnp.dot(h_lo, w_hi, preferred_element_type=F32)
                  + jnp.dot(h_hi, wrl_ref[...], preferred_element_type=F32)) + br_ref[...]
        eid_o[...], wts_o[...] = _route(logits)

    if has_tail:
        @pl.when(i >= n_main)
        def _():
            h2_o[...] = tail_ref[...]


def _outproj(yc, y_wkv, bonus, gate, ln_w, ln_b, w_out_bf16, x, mod, g2, wr_hi, wr_lo, b_router, nb, tb,
             time_major, h2_tail=None):
    b, t, d = x.shape
    nt = t // tb
    rows = nb * tb
    m = b * t
    half = D_CONV
    n_main = (b // nb) * nt
    n_tail = 0 if h2_tail is None else h2_tail.shape[0] // (rows * D_TILES)
    blk = lambda i: jnp.minimum(i, n_main - 1)
    row = lambda width: pl.BlockSpec((rows, width), lambda i: (blk(i), 0))
    modspec = lambda col: pl.BlockSpec((nb, 1, d), lambda i: (blk(i) // nt, 0, col))
    full = lambda a: pl.BlockSpec(a.shape, lambda i: (0, 0))
    if time_major:
        assert nt == 1 and h2_tail is None
        y_spec = pl.BlockSpec((tb, R_TILES, nb, LANES), lambda i: (0, 0, blk(i), 0))
        scratch = [pltpu.VMEM((nb * (tb * R_TILES + SUBLANES), LANES), F32)]
    else:
        y_spec = pl.BlockSpec((rows * R_TILES, LANES), lambda i: (blk(i), 0))
        scratch = []
    in_specs = [row(half), y_spec, row(half), row(half),
                full(ln_w), full(ln_b),
                pl.BlockSpec((half, d), lambda i: (0, 0)),
                pl.BlockSpec((half, d), lambda i: (1, 0)),
                pl.BlockSpec((nb, tb, d), lambda i: (blk(i) // nt, blk(i) % nt, 0)),
                modspec(2), modspec(3), modspec(4),
                full(g2), full(wr_hi), full(wr_lo), full(b_router)]
    args = [yc, y_wkv, bonus, gate, ln_w, ln_b, w_out_bf16, w_out_bf16, x, mod, mod, mod, g2, wr_hi, wr_lo,
            b_router]
    if h2_tail is not None:
        in_specs.append(pl.BlockSpec((rows * D_TILES, LANES), lambda i: (jnp.maximum(i - n_main, 0), 0)))
        args.append(h2_tail)
    return pl.pallas_call(
        functools.partial(_outproj_kernel, n_main, h2_tail is not None, time_major),
        scratch_shapes=scratch,
        grid=(n_main + n_tail,),
        in_specs=in_specs,
        out_specs=[row(d), pl.BlockSpec((rows * D_TILES, LANES), lambda i: (i, 0)),
                   pl.BlockSpec((SUBLANES, rows), lambda i: (0, blk(i))),
                   pl.BlockSpec((SUBLANES, rows), lambda i: (0, blk(i)))],
        out_shape=[jax.ShapeDtypeStruct((m, d), F32),
                   jax.ShapeDtypeStruct(((n_main + n_tail) * rows * D_TILES, LANES), F32),
                   jax.ShapeDtypeStruct((SUBLANES, m), I32), jax.ShapeDtypeStruct((SUBLANES, m), F32)],
        compiler_params=_params(("arbitrary",)),
        name="outproj",
    )(*args)


def _row_copy(src_hbm, src_row, dst_ref, dst_row, sem):
    src = src_row * D_TILES
    dst = dst_row * GATHER_PITCH
    if not isinstance(src, int):
        src = pl.multiple_of(src, D_TILES)
    if not isinstance(dst, int):
        dst = pl.multiple_of(dst, SUBLANES)
    return pltpu.make_async_copy(src_hbm.at[pl.ds(src, D_TILES)], dst_ref.at[pl.ds(dst, D_TILES)], sem)


def _start_row_gather(index_of, src_hbm, dst_ref, sem, n):
    for r in range(n):
        _row_copy(src_hbm, index_of(r), dst_ref, r, sem).start(priority=GATHER_DMA_PRIORITY)


def _wait_row_gather(src_hbm, dst_ref, sem, n):
    def body(r, c):
        _row_copy(src_hbm, 0, dst_ref, r, sem).wait()
        return c
    lax.fori_loop(0, n, body, 0, unroll=8)


def _experts_kernel(n_tiles, te_ref, nu_ref, nxt_ref, wsl_ref, idx_ref, idx1_ref, idx2_ref, h_hbm, wg_hbm, wu_hbm,
                    wd_hbm, o_ref, xbuf, wgf, wuf, wdf, wgb, wub, wdb, xs, act, sems, wsems):
    i = pl.program_id(0)
    n_used = nu_ref[0]
    slot = lax.rem(i, EXPERT_SLOTS)
    slot2 = lax.rem(i + 2, EXPERT_SLOTS)
    bounds = [(EXPERT_TILE * g) // EXPERT_ISSUE_GROUPS for g in range(EXPERT_ISSUE_GROUPS + 1)]

    def issue_next(g):
        for r in range(bounds[g], bounds[g + 1]):
            _row_copy(h_hbm, idx2_ref[0, r], xbuf.at[slot2], r, sems.at[slot2]).start(priority=GATHER_DMA_PRIORITY)

    def weight_copies(e, ws):
        return (pltpu.make_async_copy(wg_hbm.at[e], wgf.at[ws], wsems.at[ws]),
                pltpu.make_async_copy(wu_hbm.at[e], wuf.at[ws], wsems.at[ws]),
                pltpu.make_async_copy(wd_hbm.at[e], wdf.at[ws], wsems.at[ws]))

    @pl.when(i == 0)
    def _():
        for c in weight_copies(te_ref[0], wsl_ref[0]):
            c.start(priority=WEIGHT_DMA_PRIORITY)
        _start_row_gather(lambda r: idx_ref[0, r], h_hbm, xbuf.at[0], sems.at[0], EXPERT_TILE)
        _start_row_gather(lambda r: idx1_ref[0, r], h_hbm, xbuf.at[1], sems.at[1], EXPERT_TILE)

    @pl.when(jnp.logical_and(i >= n_used, i < n_used + 2))
    def _():
        _wait_row_gather(h_hbm, xbuf.at[slot], sems.at[slot], EXPERT_TILE)

    @pl.when(i < n_used)
    def _():
        new_expert = jnp.logical_or(i == 0, te_ref[i] != te_ref[jnp.maximum(i - 1, 0)])

        @pl.when(new_expert)
        def _():
            ws = wsl_ref[i]
            for c in weight_copies(te_ref[i], ws):
                c.wait()
            wgb[...] = wgf[ws].astype(BF16)
            wub[...] = wuf[ws].astype(BF16)
            wdb[...] = wdf[ws].astype(BF16)

            @pl.when(nxt_ref[i] >= 0)
            def _():
                for c in weight_copies(nxt_ref[i], 1 - ws):
                    c.start(priority=WEIGHT_DMA_PRIORITY)

        _wait_row_gather(h_hbm, xbuf.at[slot], sems.at[slot], EXPERT_TILE)
        xs[...] = _load_row_tiled(xbuf.at[slot], EXPERT_TILE, D_TILES, GATHER_PITCH).astype(BF16)
        issue_next(0)
        half = D_EXPERT // 2
        for c in range(2):
            cols = slice(c * half, (c + 1) * half)
            gate = jnp.dot(xs[...], wgb[:, cols], preferred_element_type=F32)
            up = jnp.dot(xs[...], wub[:, cols], preferred_element_type=F32)
            act[:, cols] = (gate * _sigmoid(gate) * up).astype(BF16)
            issue_next(1 + c)
        n_chunks = EXPERT_ISSUE_GROUPS - 3
        width = D_MODEL // n_chunks
        for c in range(n_chunks):
            out = jnp.dot(act[...], wdb[:, c * width:(c + 1) * width], preferred_element_type=F32)
            for s in range(width // LANES):
                o_ref[pl.ds(c * (width // LANES) + s, EXPERT_TILE, stride=D_TILES), :] = out[:, s * LANES:(s + 1) * LANES]
            issue_next(3 + c)

    @pl.when(jnp.logical_and(i >= n_used, i < n_tiles))
    def _():
        o_ref[...] = jnp.zeros_like(o_ref)


def _experts(plan, h2_tiled, w_gate, w_up, w_down):
    tile_expert, n_used, next_expert, weight_slot, src_rows = plan
    n_tiles = tile_expert.shape[0]
    d = w_gate.shape[1]
    idx3 = src_rows.reshape(n_tiles, 1, EXPERT_TILE)
    smem_idx = lambda f: pl.BlockSpec((None, 1, EXPERT_TILE), f, memory_space=pltpu.SMEM)
    hbm = pl.BlockSpec(memory_space=pl.ANY)
    last = n_tiles - 1
    grid_spec = pltpu.PrefetchScalarGridSpec(
        num_scalar_prefetch=4,
        grid=(n_tiles + 2,),
        in_specs=[smem_idx(lambda i, *_: (jnp.minimum(i, last), 0, 0)),
                  smem_idx(lambda i, *_: (jnp.minimum(i + 1, last), 0, 0)),
                  smem_idx(lambda i, *_: (jnp.minimum(i + 2, last), 0, 0)),
                  hbm, hbm, hbm, hbm],
        out_specs=pl.BlockSpec((EXPERT_TILE * D_TILES, LANES), lambda i, *_: (jnp.minimum(i, last), 0)),
        scratch_shapes=[pltpu.VMEM((EXPERT_SLOTS, EXPERT_TILE * GATHER_PITCH, LANES), F32),
                        pltpu.VMEM((2, d, D_EXPERT), F32), pltpu.VMEM((2, d, D_EXPERT), F32),
                        pltpu.VMEM((2, D_EXPERT, d), F32),
                        pltpu.VMEM((d, D_EXPERT), BF16), pltpu.VMEM((d, D_EXPERT), BF16),
                        pltpu.VMEM((D_EXPERT, d), BF16),
                        pltpu.VMEM((EXPERT_TILE, d), BF16), pltpu.VMEM((EXPERT_TILE, D_EXPERT), BF16),
                        pltpu.SemaphoreType.DMA((EXPERT_SLOTS,)), pltpu.SemaphoreType.DMA((2,))],
    )
    return pl.pallas_call(
        functools.partial(_experts_kernel, n_tiles),
        grid_spec=grid_spec,
        out_shape=jax.ShapeDtypeStruct((n_tiles * EXPERT_TILE * D_TILES, LANES), F32),
        compiler_params=_params(("arbitrary",)),
        name="experts",
    )(tile_expert, n_used, next_expert, weight_slot, idx3, idx3, idx3, h2_tiled, w_gate, w_up, w_down)


def _combine_kernel(idx_ref, idxn_ref, wt_ref, x1_ref, ga2_ref, gf_ref, ys_hbm, o_ref, ybuf, sems):
    i = pl.program_id(0)
    n_steps = pl.num_programs(0)
    slot = i & 1
    nb, tb, d = o_ref.shape
    rows = nb * tb

    def fetch(idx, s):
        for j in range(2):
            _start_row_gather(lambda r: idx[0, 2 * r + j], ys_hbm, ybuf.at[s, j], sems.at[s], rows)

    @pl.when(i == 0)
    def _():
        fetch(idx_ref, 0)

    @pl.when(i + 1 < n_steps)
    def _():
        fetch(idxn_ref, 1 - slot)

    for j in range(2):
        _wait_row_gather(ys_hbm, ybuf.at[slot, j], sems.at[slot], rows)

    wt = wt_ref[...]
    moe = (wt[:, 0:1] * _load_row_tiled(ybuf.at[slot, 0], rows, D_TILES, GATHER_PITCH)
           + wt[:, 1:2] * _load_row_tiled(ybuf.at[slot, 1], rows, D_TILES, GATHER_PITCH))
    x2 = x1_ref[...].reshape(nb, tb, d) + ga2_ref[...] * moe.reshape(nb, tb, d)
    ms = jnp.mean(x2 * x2, axis=-1, keepdims=True)
    o_ref[...] = x2 * lax.rsqrt(ms + RMS_EPS) * gf_ref[...]


def _combine(dest, wts, x1, mod, final_g, ys, b, t, nb, tb):
    d = x1.shape[1]
    nt = t // tb
    rows = nb * tb
    n_steps = (b // nb) * nt
    idx3 = dest.reshape(n_steps, 1, 2 * rows)
    smem_idx = lambda f: pl.BlockSpec((None, 1, 2 * rows), f, memory_space=pltpu.SMEM)
    return pl.pallas_call(
        _combine_kernel,
        grid=(n_steps,),
        in_specs=[smem_idx(lambda i: (i, 0, 0)),
                  smem_idx(lambda i: (jnp.minimum(i + 1, n_steps - 1), 0, 0)),
                  pl.BlockSpec((rows, LANES), lambda i: (i, 0)),
                  pl.BlockSpec((rows, d), lambda i: (i, 0)),
                  pl.BlockSpec((nb, 1, d), lambda i: (i // nt, 0, 5)),
                  pl.BlockSpec((1, d), lambda i: (0, 0)),
                  pl.BlockSpec(memory_space=pl.ANY)],
        out_specs=pl.BlockSpec((nb, tb, d), lambda i: (i // nt, i % nt, 0)),
        out_shape=jax.ShapeDtypeStruct((b, t, d), F32),
        scratch_shapes=[pltpu.VMEM((2, 2, rows * GATHER_PITCH, LANES), F32), pltpu.SemaphoreType.DMA((2,))],
        compiler_params=_params(("arbitrary",)),
        name="combine",
    )(idx3, idx3, wts, x1, mod, final_g, ys)


def _dispatch_plan(eid):
    n_pairs = eid.shape[0] * 2
    n_tiles = n_pairs // EXPERT_TILE + N_EXPERTS
    flat = eid.reshape(-1)
    onehot = (flat[:, None] == jnp.arange(N_EXPERTS, dtype=I32)[None, :]).astype(I32)
    csum = jnp.cumsum(onehot, axis=0)
    rank = jnp.sum((csum - onehot) * onehot, axis=1)
    counts = csum[-1]
    tiles_per = (counts + EXPERT_TILE - 1) // EXPERT_TILE
    tile_end = jnp.cumsum(tiles_per)
    row_start = (tile_end - tiles_per) * EXPERT_TILE
    dest = jnp.sum(onehot * row_start[None, :], axis=1) + rank
    n_used = tile_end[-1]
    tile_ids = jnp.arange(n_tiles, dtype=I32)
    tile_expert = jnp.sum((tile_ids[:, None] >= tile_end[None, :]).astype(I32), axis=1)
    last_expert = jnp.max(jnp.where(counts > 0, jnp.arange(N_EXPERTS, dtype=I32), 0))
    tile_expert = jnp.where(tile_ids < n_used, tile_expert, last_expert).astype(I32)
    token = jnp.arange(n_pairs, dtype=I32) // 2
    filler = jnp.arange(n_tiles * EXPERT_TILE, dtype=I32) % (n_pairs // 2)
    src_rows = filler.at[dest].set(token, unique_indices=True, mode="promise_in_bounds")
    experts = jnp.arange(N_EXPERTS, dtype=I32)
    used = counts > 0
    later_used = used[None, :] & (experts[None, :] > experts[:, None])
    next_used = jnp.min(jnp.where(later_used, experts[None, :], N_EXPERTS), axis=1)
    next_used = jnp.where(next_used < N_EXPERTS, next_used, -1).astype(I32)
    slot_of = ((jnp.cumsum(used.astype(I32)) - 1) & 1).astype(I32)
    plan = (tile_expert, n_used.reshape(1).astype(I32), next_used[tile_expert], slot_of[tile_expert], src_rows)
    return dest.astype(I32), plan


def _stream_mix(x, mod, conv_state, shift_state, wkv_state, lp, nb, tb, tt, h2_tail=None):
    b, t, d = x.shape
    h = _norm1(x, mod, lp["norm1_g"], nb, tb)
    proj = _inproj(h, lp["w_in"])
    batch_lanes = b == LANES and tb == t
    outs = _mixprep(proj.reshape(b, t, D_IN_PROJ), conv_state, shift_state, lp, nb, tb, batch_lanes)
    yc, gate, bonus, new_conv, new_shift = outs[N_WKV_OPERANDS:]
    if batch_lanes:
        y_wkv, s_out = _wkv_batch(outs[:N_WKV_OPERANDS], jnp.transpose(wkv_state, (1, 2, 3, 0)))
        new_wkv = jnp.transpose(s_out, (3, 0, 1, 2))
    else:
        operands = [z.reshape(b, t * R_TILES, LANES) for z in outs[:N_WKV_OPERANDS]]
        y_wkv, new_wkv = _wkv(operands, wkv_state, tt)
        y_wkv = y_wkv.reshape(b * t * R_TILES, LANES)
    x1, h2_tiled, eid_t, wts_t = _outproj(yc, y_wkv, bonus, gate, lp["ln_x_w"], lp["ln_x_b"], lp["w_out_bf16"], x,
                                          mod, lp["norm2_g"], lp["wr_hi"], lp["wr_lo"], lp["b_router"], nb, tb,
                                          batch_lanes, h2_tail)
    eid = eid_t[:2].T
    wts = jnp.pad(wts_t[:2].T, ((0, 0), (0, LANES - 2)))
    return x1, h2_tiled, eid, wts, new_conv, new_shift.reshape(b, D_RWKV_PROJ), new_wkv


def kernel(x_prompt, x_sample, c_prompt, c_sample, state_conv, state_shift, state_wkv, w_ada, b_ada, norm1_g, w_in, conv_w, mix_mu, w0, w_decay_up, a0, w_aaa_up, w_gate_up, k_k, k_a, r_k, ln_x_w, ln_x_b, w_out, norm2_g, w_router_group, b_router_group, w_router_expert, b_router_expert, w_gate, w_up, w_down, final_g):
    depth = w_ada.shape[0]
    assert depth == 1
    bp, tp, d = x_prompt.shape
    bs, ts, _ = x_sample.shape
    l = 0

    row = lambda z: z.reshape(1, -1)
    zeros_lora = jnp.zeros((LORA_DECAY, D_RWKV), F32)
    n_route = N_GROUPS + N_EXPERTS
    w_router = jnp.pad(jnp.concatenate([w_router_group[l], w_router_expert[l]], axis=1),
                       ((0, 0), (0, LANES - n_route)))
    b_router = jnp.concatenate([b_router_group[l], b_router_expert[l]])
    wr_hi = w_router.astype(BF16)
    lp = dict(
        norm1_g=row(norm1_g[l]), w_in=w_in[l], conv_w=conv_w[l], mix_mu=row(mix_mu[l]), w0=row(w0[l]),
        wd_pad=jnp.concatenate([w_decay_up[l], zeros_lora], axis=0), a0=row(a0[l]),
        wa_pad=jnp.concatenate([zeros_lora, w_aaa_up[l]], axis=0), w_gate_up=w_gate_up[l],
        k_k=row(k_k[l]), k_a=row(k_a[l]), r_k=row(r_k[l]), ln_x_w=row(ln_x_w[l]), ln_x_b=row(ln_x_b[l]),
        w_out_bf16=_cast_bf16(w_out[l]), norm2_g=row(norm2_g[l]),
        wr_hi=wr_hi, wr_lo=(w_router - wr_hi.astype(F32)).astype(BF16),
        b_router=jnp.pad(b_router, (0, LANES - n_route)).reshape(1, LANES),
    )

    n_c = bp + bs
    c_all = jnp.concatenate([c_prompt, c_sample], axis=0)
    c_all = jnp.pad(c_all, ((0, (-n_c) % SUBLANES), (0, 0)))
    mod = _ada(c_all, w_ada[l], row(b_ada[l]))
    mod_p = mod[:bp].reshape(bp, 1, N_MOD * d)
    mod_s = mod[bp:n_c].reshape(bs, 1, N_MOD * d)

    zc = jnp.zeros((bp, 2, D_CONV), F32)
    zs = jnp.zeros((bp, 1, D_RWKV_PROJ), F32)
    zw = jnp.zeros((bp, N_HEADS, HEAD_DIM, HEAD_DIM), F32)
    n_p = bp * tp
    x1_s, h2_s, eid_s, wts_s, conv_s, shift_s, wkv_s = _stream_mix(
        x_sample, mod_s, state_conv[l], state_shift[l].reshape(bs, 1, D_RWKV_PROJ), state_wkv[l], lp,
        nb=ROW_TILE // ts, tb=ts, tt=ts)
    x1_p, h2, eid_p, wts_p, conv_p, shift_p, wkv_p = _stream_mix(
        x_prompt, mod_p, zc, zs, zw, lp, nb=1, tb=ROW_TILE, tt=128, h2_tail=h2_s)

    eid = jnp.concatenate([eid_p[:, :2], eid_s[:, :2]], axis=0)
    dest, plan = _dispatch_plan(eid)
    ys = _experts(plan, h2, w_gate[l], w_up[l], w_down[l])
    y_prompt = _combine(dest[:2 * n_p], wts_p, x1_p, mod_p, row(final_g), ys, bp, tp, nb=1, tb=ROW_TILE)
    y_sample = _combine(dest[2 * n_p:], wts_s, x1_s, mod_s, row(final_g), ys, bs, ts, nb=ROW_TILE // ts, tb=ts)

    return (y_prompt, y_sample, conv_p[None], shift_p[None], wkv_p[None],
            conv_s[None], shift_s[None], wkv_s[None])
```
